```python
import math
import jax
import jax.numpy as jnp
from jax import lax
import numpy as np

D_MODEL = 1024
BATCH = 8
SEQ = 4096
DEPTH = 1
DEC_BATCH = 8
DEC_SEQ = 64
PAST_LEN = 4096

CHUNK = 64
Q_BLOCK = 128
MIX_WIDTH = D_MODEL
RET_WIDTH = MIX_WIDTH // 2
DIFF_WIDTH = MIX_WIDTH - RET_WIDTH
RET_HEADS = 4
RET_DK = RET_WIDTH // RET_HEADS
RET_DV = RET_WIDTH // RET_HEADS
DIFF_HEADS = 4
DIFF_DV = DIFF_WIDTH // DIFF_HEADS
DIFF_DK = DIFF_DV // 2
ROT_DIM = DIFF_DK // 4
ROPE_THETA = 500000.0
RET_THETA = 10000.0
D_FF = 2816
EPS = 1e-6
IN_SPLITS = [RET_WIDTH, 2 * RET_WIDTH, 3 * RET_WIDTH, 4 * RET_WIDTH,
             4 * RET_WIDTH + DIFF_WIDTH, 4 * RET_WIDTH + 2 * DIFF_WIDTH]
IN_COLS = 4 * RET_WIDTH + 3 * DIFF_WIDTH

kernel_name = "hybrid_retention_diffattn_macaron_stream_step"


def _rmsnorm(x, g):
    xf = x.astype(jnp.float32)
    y = xf * lax.rsqrt(jnp.mean(xf * xf, axis=-1, keepdims=True) + EPS)
    return (y * g.astype(jnp.float32)).astype(x.dtype)


def _head_layernorm(o, g):
    of = o.astype(jnp.float32)
    mu = jnp.mean(of, axis=-1, keepdims=True)
    var = jnp.mean((of - mu) ** 2, axis=-1, keepdims=True)
    return ((of - mu) * lax.rsqrt(var + EPS) * g.astype(jnp.float32)).astype(o.dtype)


def _swiglu(h, wg, wu, wd):
    return (jax.nn.silu(h @ wg) * (h @ wu)) @ wd


def _rope(x, pos, theta, rot_dim):
    half = rot_dim // 2
    inv = jnp.power(jnp.float32(theta), -jnp.arange(half, dtype=jnp.float32) * (2.0 / rot_dim))
    ang = pos.astype(jnp.float32)[:, None] * inv[None, :]
    ang = ang.reshape((1, x.shape[1]) + (1,) * (x.ndim - 3) + (half,))
    cos, sin = jnp.cos(ang), jnp.sin(ang)
    xf = x.astype(jnp.float32)
    x1, x2, rest = xf[..., :half], xf[..., half:rot_dim], xf[..., rot_dim:]
    out = jnp.concatenate([x1 * cos - x2 * sin, x2 * cos + x1 * sin, rest], axis=-1)
    return out.astype(x.dtype)


def _retention(q, k, v, s0, chunk):
    b, l, h, dk = q.shape
    dv = v.shape[-1]
    n = l // chunk
    log_g = jnp.log(1.0 - jnp.power(2.0, -5.0 - jnp.arange(h, dtype=jnp.float32)))
    idx = jnp.arange(chunk, dtype=jnp.float32)
    rel = idx[:, None] - idx[None, :]
    dmask = jnp.where(rel >= 0, jnp.exp(log_g[:, None, None] * jnp.maximum(rel, 0.0)), 0.0)
    qf = q.astype(jnp.float32).reshape(b, n, chunk, h, dk)
    kf = k.astype(jnp.float32).reshape(b, n, chunk, h, dk)
    vf = v.astype(jnp.float32).reshape(b, n, chunk, h, dv)
    scores = jnp.einsum('bnihd,bnjhd->bnhij', qf, kf) * dmask
    o = jnp.einsum('bnhij,bnjhe->bnihe', scores, vf)
    k_dec = jnp.exp(log_g[None, :] * (chunk - 1.0 - idx)[:, None])
    kv = jnp.einsum('bnjhd,bnjhe->nbhde', kf * k_dec[:, :, None], vf)
    blk_dec = jnp.exp(log_g * chunk)[None, :, None, None]

    def step(s, kv_c):
        return blk_dec * s + kv_c, s

    s_last, s_before = lax.scan(step, s0.astype(jnp.float32), kv)
    q_dec = jnp.exp(log_g[None, :] * (idx + 1.0)[:, None])
    o = o + jnp.einsum('bnihd,nbhde->bnihe', qf * q_dec[:, :, None], s_before)
    return o.reshape(b, l, h, dv).astype(q.dtype), s_last


def _diff_attn(q, k, v, q_pos, k_pos, lam):
    s = jnp.einsum('bqhcd,bkhcd->bhcqk', q, k).astype(jnp.float32) * (DIFF_DK ** -0.5)
    mask = (k_pos[None, :] // CHUNK) <= (q_pos[:, None] // CHUNK)
    s = jnp.where(mask, s, -1e30)
    p = jax.nn.softmax(s, axis=-1)
    a = p[:, :, 0] - lam * p[:, :, 1]
    return jnp.einsum('bhqk,bkhe->bqhe', a.astype(v.dtype), v)


def _layer(x, ret_s0, k_past, v_past, f1_pre, f1_wg, f1_wu, f1_wd, f1_post, mix_pre, w_in,
           ret_g, lq1, lk1, lq2, lk2, diff_g, w_out, mix_post, f2_pre, f2_wg, f2_wu, f2_wd,
           f2_post, lam_init):
    b, l, _ = x.shape
    p_len = k_past.shape[1]
    pos = p_len + jnp.arange(l, dtype=jnp.int32)
    x = x + 0.5 * _rmsnorm(_swiglu(_rmsnorm(x, f1_pre), f1_wg, f1_wu, f1_wd), f1_post)
    h = _rmsnorm(x, mix_pre)
    q_r, k_r, v_r, g_r, q_d, k_d, v_d = jnp.split(h @ w_in, IN_SPLITS, axis=-1)
    q_r = _rope(q_r.reshape(b, l, RET_HEADS, RET_DK), pos, RET_THETA, RET_DK)
    k_r = _rope(k_r.reshape(b, l, RET_HEADS, RET_DK), pos, RET_THETA, RET_DK) * (RET_DK ** -0.5)
    v_r = v_r.reshape(b, l, RET_HEADS, RET_DV)
    ret_o, s_new = _retention(q_r, k_r, v_r, ret_s0, min(CHUNK, l))
    ret_y = jax.nn.silu(g_r) * _head_layernorm(ret_o, ret_g).reshape(b, l, RET_WIDTH)
    q_d = _rope(q_d.reshape(b, l, DIFF_HEADS, 2, DIFF_DK), pos, ROPE_THETA, ROT_DIM)
    k_d = _rope(k_d.reshape(b, l, DIFF_HEADS, 2, DIFF_DK), pos, ROPE_THETA, ROT_DIM)
    v_d = v_d.reshape(b, l, DIFF_HEADS, DIFF_DV)
    lam = (jnp.exp(jnp.sum(lq1.astype(jnp.float32) * lk1.astype(jnp.float32)))
           - jnp.exp(jnp.sum(lq2.astype(jnp.float32) * lk2.astype(jnp.float32))) + lam_init)
    k_all = jnp.concatenate([k_past.astype(k_d.dtype), k_d], axis=1)
    v_all = jnp.concatenate([v_past.astype(v_d.dtype), v_d], axis=1)
    k_pos = jnp.arange(p_len + l, dtype=jnp.int32)
    qb = min(Q_BLOCK, l)
    outs = []
    for blk in range(l // qb):
        end = p_len + (blk + 1) * qb
        outs.append(_diff_attn(q_d[:, blk * qb:(blk + 1) * qb], k_all[:, :end], v_all[:, :end],
                               pos[blk * qb:(blk + 1) * qb], k_pos[:end], lam))
    diff_o = jnp.concatenate(outs, axis=1)
    diff_y = (_rmsnorm(diff_o, diff_g) * (1.0 - lam_init)).reshape(b, l, DIFF_WIDTH)
    y = jnp.concatenate([ret_y, diff_y], axis=-1) @ w_out
    x = x + _rmsnorm(y, mix_post)
    x = x + 0.5 * _rmsnorm(_swiglu(_rmsnorm(x, f2_pre), f2_wg, f2_wu, f2_wd), f2_post)
    return x, s_new, k_d, v_d


def setup_inputs(seed: int = 0) -> dict:
    key = jax.random.key(seed)
    ks = iter(jax.random.split(key, 32))

    def nrm(shape, scale):
        return jax.random.normal(next(ks), shape, jnp.float32) * scale

    def gain(shape):
        return 1.0 + nrm(shape, 0.05)

    return {
        'x_prompt': nrm((BATCH, SEQ, D_MODEL), 1.0),
        'x_sample': nrm((DEC_BATCH, DEC_SEQ, D_MODEL), 1.0),
        'state_ret': nrm((DEPTH, DEC_BATCH, RET_HEADS, RET_DK, RET_DV), 1.0),
        'cache_diff_k': nrm((DEPTH, DEC_BATCH, PAST_LEN, DIFF_HEADS, 2, DIFF_DK), 1.0),
        'cache_diff_v': nrm((DEPTH, DEC_BATCH, PAST_LEN, DIFF_HEADS, DIFF_DV), 1.0),
        'ffn1_pre_g': gain((DEPTH, D_MODEL)),
        'ffn1_w_gate': nrm((DEPTH, D_MODEL, D_FF), D_MODEL ** -0.5),
        'ffn1_w_up': nrm((DEPTH, D_MODEL, D_FF), D_MODEL ** -0.5),
        'ffn1_w_down': nrm((DEPTH, D_FF, D_MODEL), D_FF ** -0.5),
        'ffn1_post_g': gain((DEPTH, D_MODEL)),
        'mix_pre_g': gain((DEPTH, D_MODEL)),
        'w_in': nrm((DEPTH, D_MODEL, IN_COLS), D_MODEL ** -0.5),
        'ret_norm_g': gain((DEPTH, RET_HEADS, RET_DV)),
        'diff_lq1': nrm((DEPTH, DIFF_DK), 0.1),
        'diff_lk1': nrm((DEPTH, DIFF_DK), 0.1),
        'diff_lq2': nrm((DEPTH, DIFF_DK), 0.1),
        'diff_lk2': nrm((DEPTH, DIFF_DK), 0.1),
        'diff_norm_g': gain((DEPTH, DIFF_DV)),
        'w_out': nrm((DEPTH, MIX_WIDTH, D_MODEL), MIX_WIDTH ** -0.5),
        'mix_post_g': gain((DEPTH, D_MODEL)),
        'ffn2_pre_g': gain((DEPTH, D_MODEL)),
        'ffn2_w_gate': nrm((DEPTH, D_MODEL, D_FF), D_MODEL ** -0.5),
        'ffn2_w_up': nrm((DEPTH, D_MODEL, D_FF), D_MODEL ** -0.5),
        'ffn2_w_down': nrm((DEPTH, D_FF, D_MODEL), D_FF ** -0.5),
        'ffn2_post_g': gain((DEPTH, D_MODEL)),
    }


def reference(x_prompt, x_sample, state_ret, cache_diff_k, cache_diff_v,
              ffn1_pre_g, ffn1_w_gate, ffn1_w_up, ffn1_w_down, ffn1_post_g,
              mix_pre_g, w_in, ret_norm_g, diff_lq1, diff_lk1, diff_lq2, diff_lk2,
              diff_norm_g, w_out, mix_post_g,
              ffn2_pre_g, ffn2_w_gate, ffn2_w_up, ffn2_w_down, ffn2_post_g):
    xp, xs = x_prompt, x_sample
    rp, kp, vp, rs, ksm, vsm = [], [], [], [], [], []
    for li in range(DEPTH):
        lam_init = 0.8 - 0.6 * math.exp(-0.3 * li)
        w = (ffn1_pre_g[li], ffn1_w_gate[li], ffn1_w_up[li], ffn1_w_down[li], ffn1_post_g[li],
             mix_pre_g[li], w_in[li], ret_norm_g[li], diff_lq1[li], diff_lk1[li], diff_lq2[li],
             diff_lk2[li], diff_norm_g[li], w_out[li], mix_post_g[li],
             ffn2_pre_g[li], ffn2_w_gate[li], ffn2_w_up[li], ffn2_w_down[li], ffn2_post_g[li])
        s0 = jnp.zeros((xp.shape[0], RET_HEADS, RET_DK, RET_DV), jnp.float32)
        k0 = jnp.zeros((xp.shape[0], 0, DIFF_HEADS, 2, DIFF_DK), xp.dtype)
        v0 = jnp.zeros((xp.shape[0], 0, DIFF_HEADS, DIFF_DV), xp.dtype)
        xp, s_p, k_p, v_p = _layer(xp, s0, k0, v0, *w, lam_init)
        xs, s_s, k_s, v_s = _layer(xs, state_ret[li].astype(jnp.float32), cache_diff_k[li],
                                   cache_diff_v[li], *w, lam_init)
        rp.append(s_p.astype(xp.dtype)); kp.append(k_p); vp.append(v_p)
        rs.append(s_s.astype(state_ret.dtype)); ksm.append(k_s); vsm.append(v_s)
    return (xp, xs, jnp.stack(rp), jnp.stack(kp), jnp.stack(vp),
            jnp.stack(rs), jnp.stack(ksm), jnp.stack(vsm))
```

```python
import functools
import math

import jax
import jax.numpy as jnp
from jax import lax
from jax.experimental import pallas as pl
from jax.experimental.pallas import tpu as pltpu

D_MODEL = 1024
D_FF = 2816
CHUNK = 64
HEADS = 4
HEAD_W = 128
GROUP_W = HEADS * HEAD_W
N_GROUPS = 7
IN_COLS = N_GROUPS * GROUP_W
RET_DK = 128
DIFF_DK = 64
ROT_DIM = DIFF_DK // 4
ROPE_THETA = 500000.0
RET_THETA = 10000.0
EPS = 1e-6
MASK_VALUE = -1e30

ROW_TILE = 512
RET_CHUNK = 256
ATTN_TILE = 256
PAST_TILE = 1024
VMEM_LIMIT = 56 * 1024 * 1024

F32 = jnp.float32
BF16 = jnp.bfloat16
_NT = (((1,), (1,)), ((), ()))
_TN = (((0,), (0,)), ((), ()))


def _rms(x, g):
    return x * lax.rsqrt(jnp.mean(x * x, axis=-1, keepdims=True) + EPS) * g


def _silu(x):
    return x * (1.0 / (1.0 + jnp.exp(-x)))


def _resident(shape):
    nd = len(shape)
    return pl.BlockSpec(shape, lambda *_: (0,) * nd, pipeline_mode=pl.Buffered(1))


def _params(*sem):
    return pltpu.CompilerParams(dimension_semantics=sem, vmem_limit_bytes=VMEM_LIMIT)


def _ffn_math(x, pre_g, wg_ref, wu_ref, wd_ref, post_g):
    h = _rms(x, pre_g).astype(BF16)
    gate = jnp.dot(h, wg_ref[...], preferred_element_type=F32)
    up = jnp.dot(h, wu_ref[...], preferred_element_type=F32)
    act = (_silu(gate) * up).astype(BF16)
    y = jnp.dot(act, wd_ref[...], preferred_element_type=F32)
    return x + 0.5 * _rms(y, post_g)


def _ffn_kernel(x_ref, pre_ref, wg_ref, wu_ref, wd_ref, post_ref, o_ref):
    o_ref[...] = _ffn_math(x_ref[...], pre_ref[...], wg_ref, wu_ref, wd_ref, post_ref[...])


def _ffn(x, pre_g, wg, wu, wd, post_g):
    m = x.shape[0]
    tm = min(ROW_TILE, m)
    row = pl.BlockSpec((tm, D_MODEL), lambda i: (i, 0))
    return pl.pallas_call(
        _ffn_kernel,
        grid=(m // tm,),
        in_specs=[row, _resident((1, D_MODEL)), _resident((D_MODEL, D_FF)), _resident((D_MODEL, D_FF)),
                  _resident((D_FF, D_MODEL)), _resident((1, D_MODEL))],
        out_specs=row,
        out_shape=jax.ShapeDtypeStruct((m, D_MODEL), F32),
        compiler_params=_params("parallel"),
        name="ffn",
    )(x, pre_g, wg, wu, wd, post_g)


def _out_ffn_kernel(x_ref, ry_ref, dy_ref, wo_ref, mixpost_ref, pre_ref, wg_ref, wu_ref, wd_ref, post_ref,
                    o_ref):
    y = (jnp.dot(ry_ref[...], wo_ref[:GROUP_W, :], preferred_element_type=F32)
         + jnp.dot(dy_ref[...], wo_ref[GROUP_W:, :], preferred_element_type=F32))
    x2 = x_ref[...] + _rms(y, mixpost_ref[...])
    o_ref[...] = _ffn_math(x2, pre_ref[...], wg_ref, wu_ref, wd_ref, post_ref[...])


def _out_ffn(x, ret_y, diff_y, wo, mixpost_g, pre_g, wg, wu, wd, post_g):
    m = x.shape[0]
    tm = min(ROW_TILE, m)
    row = pl.BlockSpec((tm, D_MODEL), lambda i: (i, 0))
    half = pl.BlockSpec((tm, GROUP_W), lambda i: (i, 0))
    return pl.pallas_call(
        _out_ffn_kernel,
        grid=(m // tm,),
        in_specs=[row, half, half, _resident((D_MODEL, D_MODEL)), _resident((1, D_MODEL)),
                  _resident((1, D_MODEL)), _resident((D_MODEL, D_FF)), _resident((D_MODEL, D_FF)),
                  _resident((D_FF, D_MODEL)), _resident((1, D_MODEL))],
        out_specs=row,
        out_shape=jax.ShapeDtypeStruct((m, D_MODEL), F32),
        compiler_params=_params("parallel"),
        name="out_ffn",
    )(x, ret_y, diff_y, wo, mixpost_g, pre_g, wg, wu, wd, post_g)


def _rope_tables(pos):
    pos = pos.astype(F32)[:, None]
    half = RET_DK // 2
    inv = jnp.power(jnp.float32(RET_THETA), -jnp.arange(half, dtype=F32) * (2.0 / RET_DK))
    ang = pos * inv[None, :]
    cos, sin = jnp.cos(ang), jnp.sin(ang)
    cr = jnp.concatenate([cos, cos], axis=1)
    sr = jnp.concatenate([-sin, sin], axis=1)
    half = ROT_DIM // 2
    inv = jnp.power(jnp.float32(ROPE_THETA), -jnp.arange(half, dtype=F32) * (2.0 / ROT_DIM))
    ang = pos * inv[None, :]
    cos, sin = jnp.cos(ang), jnp.sin(ang)
    n = pos.shape[0]
    ones = jnp.ones((n, DIFF_DK - ROT_DIM), F32)
    zeros = jnp.zeros((n, DIFF_DK - ROT_DIM), F32)
    z8 = jnp.zeros((n, half), F32)
    c_comp = jnp.concatenate([cos, cos, ones], axis=1)
    lo_comp = jnp.concatenate([-sin, z8, zeros], axis=1)
    hi_comp = jnp.concatenate([z8, sin, zeros], axis=1)
    cd = jnp.concatenate([c_comp, c_comp], axis=1)
    s_lo = jnp.concatenate([lo_comp, lo_comp], axis=1)
    s_hi = jnp.concatenate([hi_comp, hi_comp], axis=1)
    return cr, sr, cd, s_lo, s_hi


def _mix_in_kernel(x_ref, g_ref, w_ref, cr_ref, sr_ref, cd_ref, slo_ref, shi_ref, p_ref, kd_ref, vd_ref):
    h = _rms(x_ref[...], g_ref[...]).astype(BF16)
    for grp in range(N_GROUPS):
        pg = jnp.dot(h, w_ref[:, grp * GROUP_W:(grp + 1) * GROUP_W], preferred_element_type=F32)
        for t in range(HEADS):
            cols = slice(t * HEAD_W, (t + 1) * HEAD_W)
            xt = pg[:, cols]
            if grp in (0, 1):
                xt = xt * cr_ref[...] + pltpu.roll(xt, RET_DK // 2, 1) * sr_ref[...]
                if grp == 1:
                    xt = xt * (RET_DK ** -0.5)
            elif grp in (4, 5):
                xt = (xt * cd_ref[...] + pltpu.roll(xt, HEAD_W - ROT_DIM // 2, 1) * slo_ref[...]
                      + pltpu.roll(xt, ROT_DIM // 2, 1) * shi_ref[...])
                if grp == 5:
                    kd_ref[:, cols] = xt
                else:
                    xt = xt * (DIFF_DK ** -0.5)
            elif grp == 6:
                vd_ref[:, cols] = xt
            p_ref[:, grp * GROUP_W + t * HEAD_W:grp * GROUP_W + (t + 1) * HEAD_W] = xt.astype(BF16)


def _mix_in(x, g, w, seq_len, pos0):
    m = x.shape[0]
    tm = min(ROW_TILE, m)
    n_tab = max(seq_len, tm)
    pos = pos0 + (jnp.arange(n_tab, dtype=jnp.int32) % seq_len)
    tables = _rope_tables(pos)
    n_tab_blocks = n_tab // tm
    row = pl.BlockSpec((tm, D_MODEL), lambda i: (i, 0))
    tab = pl.BlockSpec((tm, HEAD_W), lambda i: (i % n_tab_blocks, 0))
    return pl.pallas_call(
        _mix_in_kernel,
        grid=(m // tm,),
        in_specs=[row, _resident((1, D_MODEL)), _resident((D_MODEL, IN_COLS)), tab, tab, tab, tab, tab],
        out_specs=[pl.BlockSpec((tm, IN_COLS), lambda i: (i, 0)),
                   pl.BlockSpec((tm, GROUP_W), lambda i: (i, 0)),
                   pl.BlockSpec((tm, GROUP_W), lambda i: (i, 0))],
        out_shape=[jax.ShapeDtypeStruct((m, IN_COLS), BF16),
                   jax.ShapeDtypeStruct((m, GROUP_W), F32),
                   jax.ShapeDtypeStruct((m, GROUP_W), F32)],
        compiler_params=_params("parallel"),
        name="mix_in",
    )(x, g, w, *tables)


def _ret_tables(c):
    log_g = jnp.log(1.0 - jnp.power(2.0, -5.0 - jnp.arange(HEADS, dtype=F32)))
    idx = jnp.arange(c, dtype=F32)
    rel = idx[:, None] - idx[None, :]
    dmask = jnp.where(rel >= 0, jnp.exp(log_g[:, None, None] * jnp.maximum(rel, 0.0)), 0.0)
    q_dec = jnp.exp(log_g[:, None] * (idx + 1.0)[None, :])
    k_dec = jnp.exp(log_g[:, None] * (c - 1.0 - idx)[None, :])
    b_dec = jnp.exp(log_g * c)
    bcast = lambda a: jnp.broadcast_to(a[:, :, None], (HEADS, c, HEAD_W))
    return dmask, bcast(q_dec), bcast(k_dec), jnp.broadcast_to(b_dec[:, None, None], (HEADS, 8, HEAD_W))


def _ret_kernel(q_ref, k_ref, v_ref, g_ref, s0_ref, dmask_ref, qdec_ref, kdec_ref, bdec_ref, gn_ref,
                y_ref, snew_ref, s_scr, *, c, n_c):
    t = pl.program_id(1)

    @pl.when(t == 0)
    def _():
        s_scr[...] = s0_ref[...]

    for h in range(HEADS):
        cols = slice(h * HEAD_W, (h + 1) * HEAD_W)
        state = s_scr[h]
        b_dec = bdec_ref[h][0:1, :]
        for ci in range(n_c):
            rows = slice(ci * c, (ci + 1) * c)
            q = q_ref[rows, cols]
            k = k_ref[rows, cols]
            v = v_ref[rows, cols]
            scores = lax.dot_general(q, k, _NT, preferred_element_type=F32) * dmask_ref[h]
            o = (jnp.dot(scores.astype(BF16), v, preferred_element_type=F32)
                 + qdec_ref[h] * jnp.dot(q, state.astype(BF16), preferred_element_type=F32))
            k_scaled = (k.astype(F32) * kdec_ref[h]).astype(BF16)
            state = b_dec * state + lax.dot_general(k_scaled, v, _TN, preferred_element_type=F32)
            mu = jnp.mean(o, axis=-1, keepdims=True)
            d = o - mu
            var = jnp.mean(d * d, axis=-1, keepdims=True)
            normed = d * lax.rsqrt(var + EPS) * gn_ref[:, cols]
            y_ref[rows, cols] = (_silu(g_ref[rows, cols].astype(F32)) * normed).astype(BF16)
        s_scr[h] = state

    @pl.when(t == pl.num_programs(1) - 1)
    def _():
        snew_ref[...] = s_scr[...]


def _retention(p, s0, gn, seq_len):
    b = p.shape[0]
    c = min(RET_CHUNK, seq_len)
    n_c = min(2, seq_len // c)
    t_rows = c * n_c
    tables = _ret_tables(c)
    col = lambda g: pl.BlockSpec((None, t_rows, GROUP_W), lambda bi, ti, g=g: (bi, ti, g))
    state = pl.BlockSpec((None, HEADS, RET_DK, HEAD_W), lambda bi, ti: (bi, 0, 0, 0))
    return pl.pallas_call(
        functools.partial(_ret_kernel, c=c, n_c=n_c),
        grid=(b, seq_len // t_rows),
        in_specs=[col(0), col(1), col(2), col(3), state,
                  _resident((HEADS, c, c)), _resident((HEADS, c, HEAD_W)), _resident((HEADS, c, HEAD_W)),
                  _resident((HEADS, 8, HEAD_W)), _resident((1, GROUP_W))],
        out_specs=[pl.BlockSpec((None, t_rows, GROUP_W), lambda bi, ti: (bi, ti, 0)), state],
        out_shape=[jax.ShapeDtypeStruct((b, seq_len, GROUP_W), BF16),
                   jax.ShapeDtypeStruct((b, HEADS, RET_DK, HEAD_W), F32)],
        scratch_shapes=[pltpu.VMEM((HEADS, RET_DK, HEAD_W), F32)],
        compiler_params=_params("parallel", "arbitrary"),
        name="retention",
    )(p, p, p, p, s0, *tables, gn)


def _lambda(lq1_ref, lk1_ref, lq2_ref, lk2_ref, lam_init):
    return (jnp.exp(jnp.sum(lq1_ref[...] * lk1_ref[...], axis=-1, keepdims=True))
            - jnp.exp(jnp.sum(lq2_ref[...] * lk2_ref[...], axis=-1, keepdims=True)) + lam_init)


def _split_components(q):
    lane = lax.broadcasted_iota(jnp.int32, q.shape, 1)
    zero = jnp.zeros_like(q)
    return jnp.where(lane < DIFF_DK, q, zero), jnp.where(lane >= DIFF_DK, q, zero)


def _flash_update(q_comp, k, v, mask, m_ref, l_ref, acc_ref):
    s = lax.dot_general(q_comp, k, _NT, preferred_element_type=F32)
    if mask is not None:
        s = jnp.where(mask, s, MASK_VALUE)
    m_prev = m_ref[...]
    m_new = jnp.maximum(m_prev, jnp.max(s, axis=-1, keepdims=True))
    alpha = jnp.exp(m_prev - m_new)
    p = jnp.exp(s - m_new)
    l_ref[...] = alpha * l_ref[...] + jnp.sum(p, axis=-1, keepdims=True)
    acc_ref[...] = alpha * acc_ref[...] + jnp.dot(p.astype(BF16), v, preferred_element_type=F32)
    m_ref[...] = m_new


def _flash_init(m_ref, l_ref, acc_ref):
    m_ref[...] = jnp.full(m_ref.shape, MASK_VALUE, F32)
    l_ref[...] = jnp.zeros(l_ref.shape, F32)
    acc_ref[...] = jnp.zeros(acc_ref.shape, F32)


def _diff_finish(lam, lam_init, gn, l_ref, acc_ref, i1, i2):
    o = acc_ref[i1] / l_ref[i1] - lam * (acc_ref[i2] / l_ref[i2])
    return (_rms(o, gn) * (1.0 - lam_init)).astype(BF16)


def _attn_prompt_kernel(lq1_ref, lk1_ref, lq2_ref, lk2_ref, gn_ref, q_ref, k_ref, v_ref, o_ref,
                        m_scr, l_scr, acc_scr, *, tile, lam_init):
    i = pl.program_id(1)
    lam = _lambda(lq1_ref, lk1_ref, lq2_ref, lk2_ref, lam_init)
    row_chunk = lax.broadcasted_iota(jnp.int32, (tile, tile), 0) // CHUNK
    col_chunk = lax.broadcasted_iota(jnp.int32, (tile, tile), 1) // CHUNK
    diag_mask = col_chunk <= row_chunk
    for h in range(HEADS):
        cols = slice(h * HEAD_W, (h + 1) * HEAD_W)
        comps = _split_components(q_ref[:, cols])
        for c in range(2):
            _flash_init(m_scr.at[c], l_scr.at[c], acc_scr.at[c])

        def step(j, mask):
            rows = pl.ds(pl.multiple_of(j * tile, tile), tile)
            k = k_ref[rows, cols]
            v = v_ref[rows, cols]
            for c in range(2):
                _flash_update(comps[c], k, v, mask, m_scr.at[c], l_scr.at[c], acc_scr.at[c])

        def body(j, carry):
            step(j, None)
            return carry

        lax.fori_loop(0, i, body, 0)
        step(i, diag_mask)
        o_ref[:, cols] = _diff_finish(lam, lam_init, gn_ref[...], l_scr, acc_scr, 0, 1)


def _attn_prompt(p, lq1, lk1, lq2, lk2, gn, lam_init):
    b, seq_len, _ = p.shape
    tile = ATTN_TILE
    small = lambda n: pl.BlockSpec((1, n), lambda bi, qi: (0, 0))
    kv = lambda g: pl.BlockSpec((None, seq_len, GROUP_W), lambda bi, qi, g=g: (bi, 0, g))
    return pl.pallas_call(
        functools.partial(_attn_prompt_kernel, tile=tile, lam_init=lam_init),
        grid=(b, seq_len // tile),
        in_specs=[small(DIFF_DK)] * 4 + [small(HEAD_W),
                  pl.BlockSpec((None, tile, GROUP_W), lambda bi, qi: (bi, qi, 4)), kv(5), kv(6)],
        out_specs=pl.BlockSpec((None, tile, GROUP_W), lambda bi, qi: (bi, qi, 0)),
        out_shape=jax.ShapeDtypeStruct((b, seq_len, GROUP_W), BF16),
        scratch_shapes=[pltpu.VMEM((2, tile, 1), F32), pltpu.VMEM((2, tile, 1), F32),
                        pltpu.VMEM((2, tile, HEAD_W), F32)],
        compiler_params=_params("parallel", "arbitrary"),
        name="attn_prompt",
    )(lq1, lk1, lq2, lk2, gn, p, p, p)


def _attn_sample_kernel(lq1_ref, lk1_ref, lq2_ref, lk2_ref, gn_ref, q_ref, kp_ref, vp_ref, kn_ref, vn_ref,
                        o_ref, m_scr, l_scr, acc_scr, *, n_past, lam_init):
    j = pl.program_id(1)

    @pl.when(j == 0)
    def _():
        _flash_init(m_scr, l_scr, acc_scr)

    def update(k_ref, v_ref):
        for h in range(HEADS):
            cols = slice(h * HEAD_W, (h + 1) * HEAD_W)
            comps = _split_components(q_ref[:, cols])
            k = k_ref[:, cols].astype(BF16)
            v = v_ref[:, cols].astype(BF16)
            for c in range(2):
                idx = 2 * h + c
                _flash_update(comps[c], k, v, None, m_scr.at[idx], l_scr.at[idx], acc_scr.at[idx])

    @pl.when(j < n_past)
    def _():
        update(kp_ref, vp_ref)

    @pl.when(j == n_past)
    def _():
        update(kn_ref, vn_ref)
        lam = _lambda(lq1_ref, lk1_ref, lq2_ref, lk2_ref, lam_init)
        for h in range(HEADS):
            o_ref[:, h * HEAD_W:(h + 1) * HEAD_W] = _diff_finish(
                lam, lam_init, gn_ref[...], l_scr, acc_scr, 2 * h, 2 * h + 1)


def _attn_sample(p, k_past, v_past, lq1, lk1, lq2, lk2, gn, lam_init):
    b, seq_len, _ = p.shape
    past_len = k_past.shape[1]
    tk = min(PAST_TILE, past_len)
    n_past = past_len // tk
    small = lambda n: pl.BlockSpec((1, n), lambda bi, ji: (0, 0))
    new = lambda g: pl.BlockSpec((None, seq_len, GROUP_W), lambda bi, ji, g=g: (bi, 0, g))
    past = pl.BlockSpec((None, tk, GROUP_W), lambda bi, ji: (bi, jnp.minimum(ji, n_past - 1), 0))
    return pl.pallas_call(
        functools.partial(_attn_sample_kernel, n_past=n_past, lam_init=lam_init),
        grid=(b, n_past + 1),
        in_specs=[small(DIFF_DK)] * 4 + [small(HEAD_W), new(4), past, past, new(5), new(6)],
        out_specs=pl.BlockSpec((None, seq_len, GROUP_W), lambda bi, ji: (bi, 0, 0)),
        out_shape=jax.ShapeDtypeStruct((b, seq_len, GROUP_W), BF16),
        scratch_shapes=[pltpu.VMEM((2 * HEADS, seq_len, 1), F32), pltpu.VMEM((2 * HEADS, seq_len, 1), F32),
                        pltpu.VMEM((2 * HEADS, seq_len, HEAD_W), F32)],
        compiler_params=_params("parallel", "arbitrary"),
        name="attn_sample",
    )(lq1, lk1, lq2, lk2, gn, p, k_past, v_past, p, p)


def _layer(x, ret_s0, k_past, v_past, w, lam_init):
    (f1_pre, f1_wg, f1_wu, f1_wd, f1_post, mix_pre, w_in, ret_g, lq1, lk1, lq2, lk2, diff_g, w_out,
     mix_post, f2_pre, f2_wg, f2_wu, f2_wd, f2_post) = w
    b, seq_len, _ = x.shape
    past_len = 0 if k_past is None else k_past.shape[1]
    x = x.reshape(b * seq_len, D_MODEL)
    x1 = _ffn(x, f1_pre, f1_wg, f1_wu, f1_wd, f1_post)
    p, k_new, v_new = _mix_in(x1, mix_pre, w_in, seq_len, past_len)
    p = p.reshape(b, seq_len, IN_COLS)
    ret_y, s_new = _retention(p, ret_s0, ret_g, seq_len)
    if k_past is None:
        diff_y = _attn_prompt(p, lq1, lk1, lq2, lk2, diff_g, lam_init)
    else:
        diff_y = _attn_sample(p, k_past.reshape(b, past_len, GROUP_W), v_past.reshape(b, past_len, GROUP_W),
                              lq1, lk1, lq2, lk2, diff_g, lam_init)
    x3 = _out_ffn(x1, ret_y.reshape(b * seq_len, GROUP_W), diff_y.reshape(b * seq_len, GROUP_W),
                  w_out, mix_post, f2_pre, f2_wg, f2_wu, f2_wd, f2_post)
    return (x3.reshape(b, seq_len, D_MODEL), s_new,
            k_new.reshape(b, seq_len, HEADS, 2, DIFF_DK), v_new.reshape(b, seq_len, HEADS, HEAD_W))


def kernel(x_prompt, x_sample, state_ret, cache_diff_k, cache_diff_v, ffn1_pre_g, ffn1_w_gate, ffn1_w_up, ffn1_w_down, ffn1_post_g, mix_pre_g, w_in, ret_norm_g, diff_lq1, diff_lk1, diff_lq2, diff_lk2, diff_norm_g, w_out, mix_post_g, ffn2_pre_g, ffn2_w_gate, ffn2_w_up, ffn2_w_down, ffn2_post_g):
    depth = w_in.shape[0]
    xp, xs = x_prompt, x_sample
    outs = [[] for _ in range(6)]
    for li in range(depth):
        lam_init = 0.8 - 0.6 * math.exp(-0.3 * li)
        row = lambda a: a[li].reshape(1, -1)
        mat = lambda a: a[li].astype(BF16)
        w = (row(ffn1_pre_g), mat(ffn1_w_gate), mat(ffn1_w_up), mat(ffn1_w_down), row(ffn1_post_g),
             row(mix_pre_g), mat(w_in), row(ret_norm_g), row(diff_lq1), row(diff_lk1), row(diff_lq2),
             row(diff_lk2), row(diff_norm_g), mat(w_out), row(mix_post_g),
             row(ffn2_pre_g), mat(ffn2_w_gate), mat(ffn2_w_up), mat(ffn2_w_down), row(ffn2_post_g))
        s0 = jnp.zeros((xp.shape[0], HEADS, RET_DK, HEAD_W), F32)
        xp, s_p, k_p, v_p = _layer(xp, s0, None, None, w, lam_init)
        xs, s_s, k_s, v_s = _layer(xs, state_ret[li], cache_diff_k[li], cache_diff_v[li], w, lam_init)
        for acc, val in zip(outs, (s_p, k_p, v_p, s_s, k_s, v_s)):
            acc.append(val)
    return (xp, xs) + tuple(jnp.stack(o) for o in outs)
```

```python
import functools
import math

import jax
import jax.numpy as jnp
from jax import lax
from jax.experimental import pallas as pl
from jax.experimental.pallas import tpu as pltpu

D_MODEL = 1024
D_FF = 2816
CHUNK = 64
HEADS = 4
HEAD_W = 128
GROUP_W = HEADS * HEAD_W
N_GROUPS = 7
IN_COLS = N_GROUPS * GROUP_W
RET_DK = 128
DIFF_DK = 64
ROT_DIM = DIFF_DK // 4
ROPE_THETA = 500000.0
RET_THETA = 10000.0
EPS = 1e-6
MASK_VALUE = -1e30
LOG2E = math.log2(math.e)

ROW_TILE = 512
RET_CHUNK = 256
ATTN_TILE = 512
PAST_TILE = 1024
VMEM_LIMIT = 56 * 1024 * 1024

F32 = jnp.float32
BF16 = jnp.bfloat16
_NT = (((1,), (1,)), ((), ()))
_TN = (((0,), (0,)), ((), ()))


def _rms(x, g):
    return x * lax.rsqrt(jnp.mean(x * x, axis=-1, keepdims=True) + EPS) * g


def _silu(x):
    return x * (1.0 / (1.0 + jnp.exp(-x)))


def _resident(shape):
    nd = len(shape)
    return pl.BlockSpec(shape, lambda *_: (0,) * nd, pipeline_mode=pl.Buffered(1))


def _params(*sem):
    return pltpu.CompilerParams(dimension_semantics=sem, vmem_limit_bytes=VMEM_LIMIT)


def _ffn_math(x, pre_g, wg_ref, wu_ref, wd_ref, post_g):
    h = _rms(x, pre_g).astype(BF16)
    gate = jnp.dot(h, wg_ref[...], preferred_element_type=F32)
    up = jnp.dot(h, wu_ref[...], preferred_element_type=F32)
    act = (_silu(gate) * up).astype(BF16)
    y = jnp.dot(act, wd_ref[...], preferred_element_type=F32)
    return x + 0.5 * _rms(y, post_g)


def _ffn_kernel(x_ref, pre_ref, wg_ref, wu_ref, wd_ref, post_ref, o_ref):
    o_ref[...] = _ffn_math(x_ref[...], pre_ref[...], wg_ref, wu_ref, wd_ref, post_ref[...])


def _ffn(x, pre_g, wg, wu, wd, post_g):
    m = x.shape[0]
    tm = min(ROW_TILE, m)
    row = pl.BlockSpec((tm, D_MODEL), lambda i: (i, 0))
    return pl.pallas_call(
        _ffn_kernel,
        grid=(m // tm,),
        in_specs=[row, _resident((1, D_MODEL)), _resident((D_MODEL, D_FF)), _resident((D_MODEL, D_FF)),
                  _resident((D_FF, D_MODEL)), _resident((1, D_MODEL))],
        out_specs=row,
        out_shape=jax.ShapeDtypeStruct((m, D_MODEL), F32),
        compiler_params=_params("parallel"),
        name="ffn",
    )(x, pre_g, wg, wu, wd, post_g)


def _out_ffn_kernel(x_ref, ry_ref, dy_ref, wo_ref, mixpost_ref, pre_ref, wg_ref, wu_ref, wd_ref, post_ref,
                    o_ref):
    y = (jnp.dot(ry_ref[...], wo_ref[:GROUP_W, :], preferred_element_type=F32)
         + jnp.dot(dy_ref[...], wo_ref[GROUP_W:, :], preferred_element_type=F32))
    x2 = x_ref[...] + _rms(y, mixpost_ref[...])
    o_ref[...] = _ffn_math(x2, pre_ref[...], wg_ref, wu_ref, wd_ref, post_ref[...])


def _out_ffn(x, ret_y, diff_y, wo, mixpost_g, pre_g, wg, wu, wd, post_g):
    m = x.shape[0]
    tm = min(ROW_TILE, m)
    row = pl.BlockSpec((tm, D_MODEL), lambda i: (i, 0))
    half = pl.BlockSpec((tm, GROUP_W), lambda i: (i, 0))
    return pl.pallas_call(
        _out_ffn_kernel,
        grid=(m // tm,),
        in_specs=[row, half, half, _resident((D_MODEL, D_MODEL)), _resident((1, D_MODEL)),
                  _resident((1, D_MODEL)), _resident((D_MODEL, D_FF)), _resident((D_MODEL, D_FF)),
                  _resident((D_FF, D_MODEL)), _resident((1, D_MODEL))],
        out_specs=row,
        out_shape=jax.ShapeDtypeStruct((m, D_MODEL), F32),
        compiler_params=_params("parallel"),
        name="out_ffn",
    )(x, ret_y, diff_y, wo, mixpost_g, pre_g, wg, wu, wd, post_g)


def _rope_tables(pos):
    pos = pos.astype(F32)[:, None]
    half = RET_DK // 2
    inv = jnp.power(jnp.float32(RET_THETA), -jnp.arange(half, dtype=F32) * (2.0 / RET_DK))
    ang = pos * inv[None, :]
    cos, sin = jnp.cos(ang), jnp.sin(ang)
    cr = jnp.concatenate([cos, cos], axis=1)
    sr = jnp.concatenate([-sin, sin], axis=1)
    half = ROT_DIM // 2
    inv = jnp.power(jnp.float32(ROPE_THETA), -jnp.arange(half, dtype=F32) * (2.0 / ROT_DIM))
    ang = pos * inv[None, :]
    cos, sin = jnp.cos(ang), jnp.sin(ang)
    n = pos.shape[0]
    ones = jnp.ones((n, DIFF_DK - ROT_DIM), F32)
    zeros = jnp.zeros((n, DIFF_DK - ROT_DIM), F32)
    z8 = jnp.zeros((n, half), F32)
    c_comp = jnp.concatenate([cos, cos, ones], axis=1)
    lo_comp = jnp.concatenate([-sin, z8, zeros], axis=1)
    hi_comp = jnp.concatenate([z8, sin, zeros], axis=1)
    cd = jnp.concatenate([c_comp, c_comp], axis=1)
    s_lo = jnp.concatenate([lo_comp, lo_comp], axis=1)
    s_hi = jnp.concatenate([hi_comp, hi_comp], axis=1)
    return cr, sr, cd, s_lo, s_hi


def _mix_in_kernel(x_ref, g_ref, w_ref, cr_ref, sr_ref, cd_ref, slo_ref, shi_ref, p_ref, kd_ref, vd_ref):
    h = _rms(x_ref[...], g_ref[...]).astype(BF16)
    for grp in range(N_GROUPS):
        pg = jnp.dot(h, w_ref[:, grp * GROUP_W:(grp + 1) * GROUP_W], preferred_element_type=F32)
        for t in range(HEADS):
            cols = slice(t * HEAD_W, (t + 1) * HEAD_W)
            xt = pg[:, cols]
            if grp in (0, 1):
                xt = xt * cr_ref[...] + pltpu.roll(xt, RET_DK // 2, 1) * sr_ref[...]
                if grp == 1:
                    xt = xt * (RET_DK ** -0.5)
            elif grp in (4, 5):
                xt = (xt * cd_ref[...] + pltpu.roll(xt, HEAD_W - ROT_DIM // 2, 1) * slo_ref[...]
                      + pltpu.roll(xt, ROT_DIM // 2, 1) * shi_ref[...])
                if grp == 5:
                    kd_ref[:, cols] = xt
                else:
                    xt = xt * (DIFF_DK ** -0.5 * LOG2E)
            elif grp == 6:
                vd_ref[:, cols] = xt
            p_ref[:, grp * GROUP_W + t * HEAD_W:grp * GROUP_W + (t + 1) * HEAD_W] = xt.astype(BF16)


def _mix_in(x, g, w, seq_len, pos0):
    m = x.shape[0]
    tm = min(ROW_TILE, m)
    n_tab = max(seq_len, tm)
    pos = pos0 + (jnp.arange(n_tab, dtype=jnp.int32) % seq_len)
    tables = _rope_tables(pos)
    n_tab_blocks = n_tab // tm
    row = pl.BlockSpec((tm, D_MODEL), lambda i: (i, 0))
    tab = pl.BlockSpec((tm, HEAD_W), lambda i: (i % n_tab_blocks, 0))
    return pl.pallas_call(
        _mix_in_kernel,
        grid=(m // tm,),
        in_specs=[row, _resident((1, D_MODEL)), _resident((D_MODEL, IN_COLS)), tab, tab, tab, tab, tab],
        out_specs=[pl.BlockSpec((tm, IN_COLS), lambda i: (i, 0)),
                   pl.BlockSpec((tm, GROUP_W), lambda i: (i, 0)),
                   pl.BlockSpec((tm, GROUP_W), lambda i: (i, 0))],
        out_shape=[jax.ShapeDtypeStruct((m, IN_COLS), BF16),
                   jax.ShapeDtypeStruct((m, GROUP_W), F32),
                   jax.ShapeDtypeStruct((m, GROUP_W), F32)],
        compiler_params=_params("parallel"),
        name="mix_in",
    )(x, g, w, *tables)


def _ret_tables(c):
    log_g = jnp.log(1.0 - jnp.power(2.0, -5.0 - jnp.arange(HEADS, dtype=F32)))
    idx = jnp.arange(c, dtype=F32)
    rel = idx[:, None] - idx[None, :]
    dmask = jnp.where(rel >= 0, jnp.exp(log_g[:, None, None] * jnp.maximum(rel, 0.0)), 0.0)
    q_dec = jnp.exp(log_g[:, None] * (idx + 1.0)[None, :])
    k_dec = jnp.exp(log_g[:, None] * (c - 1.0 - idx)[None, :])
    b_dec = jnp.exp(log_g * c)
    bcast = lambda a: jnp.broadcast_to(a[:, :, None], (HEADS, c, HEAD_W))
    return dmask, bcast(q_dec), bcast(k_dec), jnp.broadcast_to(b_dec[:, None, None], (HEADS, 8, HEAD_W))


def _ret_kernel(q_ref, k_ref, v_ref, g_ref, s0_ref, dmask_ref, qdec_ref, kdec_ref, bdec_ref, gn_ref,
                y_ref, snew_ref, s_scr, *, c, n_c):
    t = pl.program_id(1)

    @pl.when(t == 0)
    def _():
        s_scr[...] = s0_ref[...]

    for h in range(HEADS):
        cols = slice(h * HEAD_W, (h + 1) * HEAD_W)
        state = s_scr[h]
        b_dec = bdec_ref[h][0:1, :]
        for ci in range(n_c):
            rows = slice(ci * c, (ci + 1) * c)
            q = q_ref[rows, cols]
            k = k_ref[rows, cols]
            v = v_ref[rows, cols]
            scores = lax.dot_general(q, k, _NT, preferred_element_type=F32) * dmask_ref[h]
            o = (jnp.dot(scores.astype(BF16), v, preferred_element_type=F32)
                 + qdec_ref[h] * jnp.dot(q, state.astype(BF16), preferred_element_type=F32))
            k_scaled = (k.astype(F32) * kdec_ref[h]).astype(BF16)
            state = b_dec * state + lax.dot_general(k_scaled, v, _TN, preferred_element_type=F32)
            mu = jnp.mean(o, axis=-1, keepdims=True)
            d = o - mu
            var = jnp.mean(d * d, axis=-1, keepdims=True)
            normed = d * lax.rsqrt(var + EPS) * gn_ref[:, cols]
            y_ref[rows, cols] = (_silu(g_ref[rows, cols].astype(F32)) * normed).astype(BF16)
        s_scr[h] = state

    @pl.when(t == pl.num_programs(1) - 1)
    def _():
        snew_ref[...] = s_scr[...]


def _retention(p, s0, gn, seq_len):
    b = p.shape[0]
    c = min(RET_CHUNK, seq_len)
    n_c = min(2, seq_len // c)
    t_rows = c * n_c
    tables = _ret_tables(c)
    col = lambda g: pl.BlockSpec((None, t_rows, GROUP_W), lambda bi, ti, g=g: (bi, ti, g))
    state = pl.BlockSpec((None, HEADS, RET_DK, HEAD_W), lambda bi, ti: (bi, 0, 0, 0))
    return pl.pallas_call(
        functools.partial(_ret_kernel, c=c, n_c=n_c),
        grid=(b, seq_len // t_rows),
        in_specs=[col(0), col(1), col(2), col(3), state,
                  _resident((HEADS, c, c)), _resident((HEADS, c, HEAD_W)), _resident((HEADS, c, HEAD_W)),
                  _resident((HEADS, 8, HEAD_W)), _resident((1, GROUP_W))],
        out_specs=[pl.BlockSpec((None, t_rows, GROUP_W), lambda bi, ti: (bi, ti, 0)), state],
        out_shape=[jax.ShapeDtypeStruct((b, seq_len, GROUP_W), BF16),
                   jax.ShapeDtypeStruct((b, HEADS, RET_DK, HEAD_W), F32)],
        scratch_shapes=[pltpu.VMEM((HEADS, RET_DK, HEAD_W), F32)],
        compiler_params=_params("parallel", "arbitrary"),
        name="retention",
    )(p, p, p, p, s0, *tables, gn)


def _lambda(lq1_ref, lk1_ref, lq2_ref, lk2_ref, lam_init):
    return (jnp.exp(jnp.sum(lq1_ref[...] * lk1_ref[...], axis=-1, keepdims=True))
            - jnp.exp(jnp.sum(lq2_ref[...] * lk2_ref[...], axis=-1, keepdims=True)) + lam_init)


def _split_components(q):
    lane = lax.broadcasted_iota(jnp.int32, q.shape, 1)
    zero = jnp.zeros_like(q)
    return jnp.where(lane < DIFF_DK, q, zero), jnp.where(lane >= DIFF_DK, q, zero)


def _flash_update(q_comp, k, v, mask, m_ref, l_ref, acc_ref):
    s = lax.dot_general(q_comp, k, _NT, preferred_element_type=F32)
    if mask is not None:
        s = jnp.where(mask, s, MASK_VALUE)
    pad = -s.shape[1] % HEAD_W
    if pad:
        s = jnp.concatenate([s, jnp.full((s.shape[0], pad), MASK_VALUE, F32)], axis=1)
        v = jnp.concatenate([v, jnp.zeros((pad, v.shape[1]), v.dtype)], axis=0)
    blocks =[s[:, b * HEAD_W:(b + 1) * HEAD_W] for b in range(s.shape[1] // HEAD_W)]
    m_prev = m_ref[...]
    m_new = jnp.maximum(m_prev, jnp.max(functools.reduce(jnp.maximum, blocks), axis=-1, keepdims=True))
    alpha = jnp.exp2(m_prev - m_new)
    probs = [jnp.exp2(blk - m_new) for blk in blocks]
    l_ref[...] = alpha * l_ref[...] + functools.reduce(jnp.add, probs)
    p = jnp.concatenate([x.astype(BF16) for x in probs], axis=1)
    acc_ref[...] = alpha * acc_ref[...] + jnp.dot(p, v, preferred_element_type=F32)
    m_ref[...] = m_new


def _flash_init(m_ref, l_ref, acc_ref):
    m_ref[...] = jnp.full(m_ref.shape, MASK_VALUE, F32)
    l_ref[...] = jnp.zeros(l_ref.shape, F32)
    acc_ref[...] = jnp.zeros(acc_ref.shape, F32)


def _diff_finish(lam, lam_init, gn, stats1, stats2):
    (_, l1_ref, acc1_ref), (_, l2_ref, acc2_ref) = stats1, stats2
    l1 = jnp.sum(l1_ref[...], axis=-1, keepdims=True)
    l2 = jnp.sum(l2_ref[...], axis=-1, keepdims=True)
    o = acc1_ref[...] / l1 - lam * (acc2_ref[...] / l2)
    return (_rms(o, gn) * (1.0 - lam_init)).astype(BF16)


def _attn_prompt_kernel(lq1_ref, lk1_ref, lq2_ref, lk2_ref, gn_ref, q_ref, k_ref, v_ref, o_ref,
                        *scratch, tile, lam_init):
    i = pl.program_id(1)
    stats = (scratch[0:3], scratch[3:6])
    lam = _lambda(lq1_ref, lk1_ref, lq2_ref, lk2_ref, lam_init)
    row_chunk = lax.broadcasted_iota(jnp.int32, (tile, tile), 0) // CHUNK
    col_chunk = lax.broadcasted_iota(jnp.int32, (tile, tile), 1) // CHUNK
    diag_mask = col_chunk <= row_chunk
    for h in range(HEADS):
        cols = slice(h * HEAD_W, (h + 1) * HEAD_W)
        comps = _split_components(q_ref[:, cols])
        for c in range(2):
            _flash_init(*stats[c])

        def step(j, mask):
            rows = pl.ds(pl.multiple_of(j * tile, tile), tile)
            k = k_ref[rows, cols]
            v = v_ref[rows, cols]
            for c in range(2):
                _flash_update(comps[c], k, v, mask, *stats[c])

        def body(j, carry):
            step(j, None)
            return carry

        lax.fori_loop(0, i, body, 0)
        step(i, diag_mask)
        o_ref[:, cols] = _diff_finish(lam, lam_init, gn_ref[...], stats[0], stats[1])


def _attn_prompt(p, lq1, lk1, lq2, lk2, gn, lam_init):
    b, seq_len, _ = p.shape
    tile = ATTN_TILE
    small = lambda n: pl.BlockSpec((1, n), lambda bi, qi: (0, 0))
    kv = lambda g: pl.BlockSpec((None, seq_len, GROUP_W), lambda bi, qi, g=g: (bi, 0, g))
    return pl.pallas_call(
        functools.partial(_attn_prompt_kernel, tile=tile, lam_init=lam_init),
        grid=(b, seq_len // tile),
        in_specs=[small(DIFF_DK)] * 4 + [small(HEAD_W),
                  pl.BlockSpec((None, tile, GROUP_W), lambda bi, qi: (bi, qi, 4)), kv(5), kv(6)],
        out_specs=pl.BlockSpec((None, tile, GROUP_W), lambda bi, qi: (bi, qi, 0)),
        out_shape=jax.ShapeDtypeStruct((b, seq_len, GROUP_W), BF16),
        scratch_shapes=[pltpu.VMEM((tile, HEAD_W), F32)] * 6,
        compiler_params=_params("parallel", "arbitrary"),
        name="attn_prompt",
    )(lq1, lk1, lq2, lk2, gn, p, p, p)


def _attn_sample_kernel(lq1_ref, lk1_ref, lq2_ref, lk2_ref, gn_ref, q_ref, kp_ref, vp_ref, kn_ref, vn_ref,
                        o_ref, m_scr, l_scr, acc_scr, *, n_past, lam_init):
    j = pl.program_id(1)

    @pl.when(j == 0)
    def _():
        _flash_init(m_scr, l_scr, acc_scr)

    def update(k_ref, v_ref):
        for h in range(HEADS):
            cols = slice(h * HEAD_W, (h + 1) * HEAD_W)
            comps = _split_components(q_ref[:, cols])
            k = k_ref[:, cols].astype(BF16)
            v = v_ref[:, cols].astype(BF16)
            for c in range(2):
                idx = 2 * h + c
                _flash_update(comps[c], k, v, None, m_scr.at[idx], l_scr.at[idx], acc_scr.at[idx])

    @pl.when(j < n_past)
    def _():
        update(kp_ref, vp_ref)

    @pl.when(j == n_past)
    def _():
        update(kn_ref, vn_ref)
        lam = _lambda(lq1_ref, lk1_ref, lq2_ref, lk2_ref, lam_init)
        for h in range(HEADS):
            stats = [(m_scr.at[2 * h + c], l_scr.at[2 * h + c], acc_scr.at[2 * h + c]) for c in range(2)]
            o_ref[:, h * HEAD_W:(h + 1) * HEAD_W] = _diff_finish(lam, lam_init, gn_ref[...], *stats)


def _attn_sample(p, k_past, v_past, lq1, lk1, lq2, lk2, gn, lam_init):
    b, seq_len, _ = p.shape
    past_len = k_past.shape[1]
    tk = min(PAST_TILE, past_len)
    n_past = past_len // tk
    small = lambda n: pl.BlockSpec((1, n), lambda bi, ji: (0, 0))
    new = lambda g: pl.BlockSpec((None, seq_len, GROUP_W), lambda bi, ji, g=g: (bi, 0, g))
    past = pl.BlockSpec((None, tk, GROUP_W), lambda bi, ji: (bi, jnp.minimum(ji, n_past - 1), 0))
    return pl.pallas_call(
        functools.partial(_attn_sample_kernel, n_past=n_past, lam_init=lam_init),
        grid=(b, n_past + 1),
        in_specs=[small(DIFF_DK)] * 4 + [small(HEAD_W), new(4), past, past, new(5), new(6)],
        out_specs=pl.BlockSpec((None, seq_len, GROUP_W), lambda bi, ji: (bi, 0, 0)),
        out_shape=jax.ShapeDtypeStruct((b, seq_len, GROUP_W), BF16),
        scratch_shapes=[pltpu.VMEM((2 * HEADS, seq_len, HEAD_W), F32)] * 3,
        compiler_params=_params("parallel", "arbitrary"),
        name="attn_sample",
    )(lq1, lk1, lq2, lk2, gn, p, k_past, v_past, p, p)


def _layer(x, ret_s0, k_past, v_past, w, lam_init):
    (f1_pre, f1_wg, f1_wu, f1_wd, f1_post, mix_pre, w_in, ret_g, lq1, lk1, lq2, lk2, diff_g, w_out,
     mix_post, f2_pre, f2_wg, f2_wu, f2_wd, f2_post) = w
    b, seq_len, _ = x.shape
    past_len = 0 if k_past is None else k_past.shape[1]
    x = x.reshape(b * seq_len, D_MODEL)
    x1 = _ffn(x, f1_pre, f1_wg, f1_wu, f1_wd, f1_post)
    p, k_new, v_new = _mix_in(x1, mix_pre, w_in, seq_len, past_len)
    p = p.reshape(b, seq_len, IN_COLS)
    ret_y, s_new = _retention(p, ret_s0, ret_g, seq_len)
    if k_past is None:
        diff_y = _attn_prompt(p, lq1, lk1, lq2, lk2, diff_g, lam_init)
    else:
        diff_y = _attn_sample(p, k_past.reshape(b, past_len, GROUP_W), v_past.reshape(b, past_len, GROUP_W),
                              lq1, lk1, lq2, lk2, diff_g, lam_init)
    x3 = _out_ffn(x1, ret_y.reshape(b * seq_len, GROUP_W), diff_y.reshape(b * seq_len, GROUP_W),
                  w_out, mix_post, f2_pre, f2_wg, f2_wu, f2_wd, f2_post)
    return (x3.reshape(b, seq_len, D_MODEL), s_new,
            k_new.reshape(b, seq_len, HEADS, 2, DIFF_DK), v_new.reshape(b, seq_len, HEADS, HEAD_W))


def kernel(x_prompt, x_sample, state_ret, cache_diff_k, cache_diff_v, ffn1_pre_g, ffn1_w_gate, ffn1_w_up, ffn1_w_down, ffn1_post_g, mix_pre_g, w_in, ret_norm_g, diff_lq1, diff_lk1, diff_lq2, diff_lk2, diff_norm_g, w_out, mix_post_g, ffn2_pre_g, ffn2_w_gate, ffn2_w_up, ffn2_w_down, ffn2_post_g):
    depth = w_in.shape[0]
    xp, xs = x_prompt, x_sample
    outs = [[] for _ in range(6)]
    for li in range(depth):
        lam_init = 0.8 - 0.6 * math.exp(-0.3 * li)
        row = lambda a: a[li].reshape(1, -1)
        mat = lambda a: a[li].astype(BF16)
        w = (row(ffn1_pre_g), mat(ffn1_w_gate), mat(ffn1_w_up), mat(ffn1_w_down), row(ffn1_post_g),
             row(mix_pre_g), mat(w_in), row(ret_norm_g), row(diff_lq1), row(diff_lk1), row(diff_lq2),
             row(diff_lk2), row(diff_norm_g), mat(w_out), row(mix_post_g),
             row(ffn2_pre_g), mat(ffn2_w_gate), mat(ffn2_w_up), mat(ffn2_w_down), row(ffn2_post_g))
        s0 = jnp.zeros((xp.shape[0], HEADS, RET_DK, HEAD_W), F32)
        xp, s_p, k_p, v_p = _layer(xp, s0, None, None, w, lam_init)
        xs, s_s, k_s, v_s = _layer(xs, state_ret[li], cache_diff_k[li], cache_diff_v[li], w, lam_init)
        for acc, val in zip(outs, (s_p, k_p, v_p, s_s, k_s, v_s)):
            acc.append(val)
    return (xp, xs) + tuple(jnp.stack(o) for o in outs)
```

```python
import functools
import math

import jax
import jax.numpy as jnp
from jax import lax
from jax.experimental import pallas as pl
from jax.experimental.pallas import tpu as pltpu

D_MODEL = 1024
D_FF = 2816
CHUNK = 64
HEADS = 4
HEAD_W = 128
GROUP_W = HEADS * HEAD_W
N_GROUPS = 7
IN_COLS = N_GROUPS * GROUP_W
RET_DK = 128
DIFF_DK = 64
ROT_DIM = DIFF_DK // 4
ROPE_THETA = 500000.0
RET_THETA = 10000.0
EPS = 1e-6
MASK_VALUE = -1e30
LOG2E = math.log2(math.e)

ROW_TILE = 512
RET_CHUNK = 256
ATTN_TILE = 512
PAST_TILE = 1024
VMEM_LIMIT = 56 * 1024 * 1024

F32 = jnp.float32
BF16 = jnp.bfloat16
_NT = (((1,), (1,)), ((), ()))
_TN = (((0,), (0,)), ((), ()))


def _rms(x, g):
    return x * lax.rsqrt(jnp.mean(x * x, axis=-1, keepdims=True) + EPS) * g


def _silu(x):
    return x * (1.0 / (1.0 + jnp.exp(-x)))


def _resident(shape):
    nd = len(shape)
    return pl.BlockSpec(shape, lambda *_: (0,) * nd, pipeline_mode=pl.Buffered(1))


def _params(*sem):
    return pltpu.CompilerParams(dimension_semantics=sem, vmem_limit_bytes=VMEM_LIMIT)


def _ffn_math(x, pre_g, wg_ref, wu_ref, wd_ref, post_g):
    h = _rms(x, pre_g).astype(BF16)
    gate = jnp.dot(h, wg_ref[...], preferred_element_type=F32)
    up = jnp.dot(h, wu_ref[...], preferred_element_type=F32)
    act = (_silu(gate) * up).astype(BF16)
    y = jnp.dot(act, wd_ref[...], preferred_element_type=F32)
    return x + 0.5 * _rms(y, post_g)


def _ffn_kernel(x_ref, pre_ref, wg_ref, wu_ref, wd_ref, post_ref, o_ref):
    o_ref[...] = _ffn_math(x_ref[...], pre_ref[...], wg_ref, wu_ref, wd_ref, post_ref[...])


def _ffn(x, pre_g, wg, wu, wd, post_g):
    m = x.shape[0]
    tm = min(ROW_TILE, m)
    row = pl.BlockSpec((tm, D_MODEL), lambda i: (i, 0))
    return pl.pallas_call(
        _ffn_kernel,
        grid=(m // tm,),
        in_specs=[row, _resident((1, D_MODEL)), _resident((D_MODEL, D_FF)), _resident((D_MODEL, D_FF)),
                  _resident((D_FF, D_MODEL)), _resident((1, D_MODEL))],
        out_specs=row,
        out_shape=jax.ShapeDtypeStruct((m, D_MODEL), F32),
        compiler_params=_params("parallel"),
        name="ffn",
    )(x, pre_g, wg, wu, wd, post_g)


def _out_ffn_kernel(x_ref, ry_ref, dy_ref, wo_ref, mixpost_ref, pre_ref, wg_ref, wu_ref, wd_ref, post_ref,
                    o_ref):
    y = (jnp.dot(ry_ref[...], wo_ref[:GROUP_W, :], preferred_element_type=F32)
         + jnp.dot(dy_ref[...], wo_ref[GROUP_W:, :], preferred_element_type=F32))
    x2 = x_ref[...] + _rms(y, mixpost_ref[...])
    o_ref[...] = _ffn_math(x2, pre_ref[...], wg_ref, wu_ref, wd_ref, post_ref[...])


def _out_ffn(x, ret_y, diff_y, wo, mixpost_g, pre_g, wg, wu, wd, post_g):
    m = x.shape[0]
    tm = min(ROW_TILE, m)
    row = pl.BlockSpec((tm, D_MODEL), lambda i: (i, 0))
    half = pl.BlockSpec((tm, GROUP_W), lambda i: (i, 0))
    return pl.pallas_call(
        _out_ffn_kernel,
        grid=(m // tm,),
        in_specs=[row, half, half, _resident((D_MODEL, D_MODEL)), _resident((1, D_MODEL)),
                  _resident((1, D_MODEL)), _resident((D_MODEL, D_FF)), _resident((D_MODEL, D_FF)),
                  _resident((D_FF, D_MODEL)), _resident((1, D_MODEL))],
        out_specs=row,
        out_shape=jax.ShapeDtypeStruct((m, D_MODEL), F32),
        compiler_params=_params("parallel"),
        name="out_ffn",
    )(x, ret_y, diff_y, wo, mixpost_g, pre_g, wg, wu, wd, post_g)


def _rope_tables(pos):
    pos = pos.astype(F32)[:, None]
    half = RET_DK // 2
    inv = jnp.power(jnp.float32(RET_THETA), -jnp.arange(half, dtype=F32) * (2.0 / RET_DK))
    ang = pos * inv[None, :]
    cos, sin = jnp.cos(ang), jnp.sin(ang)
    cr = jnp.concatenate([cos, cos], axis=1)
    sr = jnp.concatenate([-sin, sin], axis=1)
    half = ROT_DIM // 2
    inv = jnp.power(jnp.float32(ROPE_THETA), -jnp.arange(half, dtype=F32) * (2.0 / ROT_DIM))
    ang = pos * inv[None, :]
    cos, sin = jnp.cos(ang), jnp.sin(ang)
    n = pos.shape[0]
    ones = jnp.ones((n, DIFF_DK - ROT_DIM), F32)
    zeros = jnp.zeros((n, DIFF_DK - ROT_DIM), F32)
    z8 = jnp.zeros((n, half), F32)
    c_comp = jnp.concatenate([cos, cos, ones], axis=1)
    lo_comp = jnp.concatenate([-sin, z8, zeros], axis=1)
    hi_comp = jnp.concatenate([z8, sin, zeros], axis=1)
    cd = jnp.concatenate([c_comp, c_comp], axis=1)
    s_lo = jnp.concatenate([lo_comp, lo_comp], axis=1)
    s_hi = jnp.concatenate([hi_comp, hi_comp], axis=1)
    return cr, sr, cd, s_lo, s_hi


def _mix_in_kernel(x_ref, g_ref, w_ref, cr_ref, sr_ref, cd_ref, slo_ref, shi_ref, p_ref, kd_ref, vd_ref,
                   *, transpose_k):
    h = _rms(x_ref[...], g_ref[...]).astype(BF16)
    for grp in range(N_GROUPS):
        pg = jnp.dot(h, w_ref[:, grp * GROUP_W:(grp + 1) * GROUP_W], preferred_element_type=F32)
        for t in range(HEADS):
            cols = slice(t * HEAD_W, (t + 1) * HEAD_W)
            xt = pg[:, cols]
            if grp in (0, 1):
                xt = xt * cr_ref[...] + pltpu.roll(xt, RET_DK // 2, 1) * sr_ref[...]
                if grp == 1:
                    xt = xt * (RET_DK ** -0.5)
            elif grp in (4, 5):
                xt = (xt * cd_ref[...] + pltpu.roll(xt, HEAD_W - ROT_DIM // 2, 1) * slo_ref[...]
                      + pltpu.roll(xt, ROT_DIM // 2, 1) * shi_ref[...])
                if grp == 5 and transpose_k:
                    kd_ref[cols, :] = xt.T
                elif grp == 5:
                    kd_ref[:, cols] = xt
                else:
                    xt = xt * (DIFF_DK ** -0.5 * LOG2E)
            elif grp == 6:
                vd_ref[:, t, :] = xt
            p_ref[:, grp * GROUP_W + t * HEAD_W:grp * GROUP_W + (t + 1) * HEAD_W] = xt.astype(BF16)


def _mix_in(x, g, w, seq_len, pos0):
    m = x.shape[0]
    tm = min(ROW_TILE, m)
    n_tab = max(seq_len, tm)
    pos = pos0 + (jnp.arange(n_tab, dtype=jnp.int32) % seq_len)
    tables = _rope_tables(pos)
    n_tab_blocks = n_tab // tm
    row = pl.BlockSpec((tm, D_MODEL), lambda i: (i, 0))
    tab = pl.BlockSpec((tm, HEAD_W), lambda i: (i % n_tab_blocks, 0))
    transpose_k = seq_len % tm == 0
    if transpose_k:
        n_seq_blocks = seq_len // tm
        k_spec = pl.BlockSpec((None, GROUP_W, tm), lambda i: (i // n_seq_blocks, 0, i % n_seq_blocks))
        k_shape = jax.ShapeDtypeStruct((m // seq_len, GROUP_W, seq_len), F32)
    else:
        k_spec = pl.BlockSpec((tm, GROUP_W), lambda i: (i, 0))
        k_shape = jax.ShapeDtypeStruct((m, GROUP_W), F32)
    p, k_new, v_new = pl.pallas_call(
        functools.partial(_mix_in_kernel, transpose_k=transpose_k),
        grid=(m // tm,),
        in_specs=[row, _resident((1, D_MODEL)), _resident((D_MODEL, IN_COLS)), tab, tab, tab, tab, tab],
        out_specs=[pl.BlockSpec((tm, IN_COLS), lambda i: (i, 0)), k_spec,
                   pl.BlockSpec((tm, HEADS, HEAD_W), lambda i: (i, 0, 0))],
        out_shape=[jax.ShapeDtypeStruct((m, IN_COLS), BF16), k_shape,
                   jax.ShapeDtypeStruct((m, HEADS, HEAD_W), F32)],
        compiler_params=_params("parallel"),
        name="mix_in",
    )(x, g, w, *tables)
    b = m // seq_len
    if transpose_k:
        k_new = jnp.transpose(k_new.reshape(b, HEADS, 2, DIFF_DK, seq_len), (0, 4, 1, 2, 3))
    else:
        k_new = k_new.reshape(b, seq_len, HEADS, 2, DIFF_DK)
    return p.reshape(b, seq_len, IN_COLS), k_new, v_new.reshape(b, seq_len, HEADS, HEAD_W)


def _ret_tables(c):
    log_g = jnp.log(1.0 - jnp.power(2.0, -5.0 - jnp.arange(HEADS, dtype=F32)))
    idx = jnp.arange(c, dtype=F32)
    rel = idx[:, None] - idx[None, :]
    dmask = jnp.where(rel >= 0, jnp.exp(log_g[:, None, None] * jnp.maximum(rel, 0.0)), 0.0)
    q_dec = jnp.exp(log_g[:, None] * (idx + 1.0)[None, :])
    k_dec = jnp.exp(log_g[:, None] * (c - 1.0 - idx)[None, :])
    b_dec = jnp.exp(log_g * c)
    bcast = lambda a: jnp.broadcast_to(a[:, :, None], (HEADS, c, HEAD_W))
    return dmask, bcast(q_dec), bcast(k_dec), jnp.broadcast_to(b_dec[:, None, None], (HEADS, 8, HEAD_W))


def _ret_kernel(q_ref, k_ref, v_ref, g_ref, s0_ref, dmask_ref, qdec_ref, kdec_ref, bdec_ref, gn_ref,
                y_ref, snew_ref, s_scr, *, c, n_c):
    t = pl.program_id(1)

    @pl.when(t == 0)
    def _():
        s_scr[...] = s0_ref[...]

    for h in range(HEADS):
        cols = slice(h * HEAD_W, (h + 1) * HEAD_W)
        state = s_scr[h]
        b_dec = bdec_ref[h][0:1, :]
        for ci in range(n_c):
            rows = slice(ci * c, (ci + 1) * c)
            q = q_ref[rows, cols]
            k = k_ref[rows, cols]
            v = v_ref[rows, cols]
            scores = lax.dot_general(q, k, _NT, preferred_element_type=F32) * dmask_ref[h]
            o = (jnp.dot(scores.astype(BF16), v, preferred_element_type=F32)
                 + qdec_ref[h] * jnp.dot(q, state.astype(BF16), preferred_element_type=F32))
            k_scaled = (k.astype(F32) * kdec_ref[h]).astype(BF16)
            state = b_dec * state + lax.dot_general(k_scaled, v, _TN, preferred_element_type=F32)
            mu = jnp.mean(o, axis=-1, keepdims=True)
            d = o - mu
            var = jnp.mean(d * d, axis=-1, keepdims=True)
            normed = d * lax.rsqrt(var + EPS) * gn_ref[:, cols]
            y_ref[rows, cols] = (_silu(g_ref[rows, cols].astype(F32)) * normed).astype(BF16)
        s_scr[h] = state

    @pl.when(t == pl.num_programs(1) - 1)
    def _():
        snew_ref[...] = s_scr[...]


def _retention(p, s0, gn, seq_len):
    b = p.shape[0]
    c = min(RET_CHUNK, seq_len)
    n_c = min(2, seq_len // c)
    t_rows = c * n_c
    tables = _ret_tables(c)
    col = lambda g: pl.BlockSpec((None, t_rows, GROUP_W), lambda bi, ti, g=g: (bi, ti, g))
    state = pl.BlockSpec((None, HEADS, RET_DK, HEAD_W), lambda bi, ti: (bi, 0, 0, 0))
    return pl.pallas_call(
        functools.partial(_ret_kernel, c=c, n_c=n_c),
        grid=(b, seq_len // t_rows),
        in_specs=[col(0), col(1), col(2), col(3), state,
                  _resident((HEADS, c, c)), _resident((HEADS, c, HEAD_W)), _resident((HEADS, c, HEAD_W)),
                  _resident((HEADS, 8, HEAD_W)), _resident((1, GROUP_W))],
        out_specs=[pl.BlockSpec((None, t_rows, GROUP_W), lambda bi, ti: (bi, ti, 0)), state],
        out_shape=[jax.ShapeDtypeStruct((b, seq_len, GROUP_W), BF16),
                   jax.ShapeDtypeStruct((b, HEADS, RET_DK, HEAD_W), F32)],
        scratch_shapes=[pltpu.VMEM((HEADS, RET_DK, HEAD_W), F32)],
        compiler_params=_params("parallel", "arbitrary"),
        name="retention",
    )(p, p, p, p, s0, *tables, gn)


def _lambda(lq1_ref, lk1_ref, lq2_ref, lk2_ref, lam_init):
    return (jnp.exp(jnp.sum(lq1_ref[...] * lk1_ref[...], axis=-1, keepdims=True))
            - jnp.exp(jnp.sum(lq2_ref[...] * lk2_ref[...], axis=-1, keepdims=True)) + lam_init)


def _split_components(q):
    lane = lax.broadcasted_iota(jnp.int32, q.shape, 1)
    zero = jnp.zeros_like(q)
    return jnp.where(lane < DIFF_DK, q, zero), jnp.where(lane >= DIFF_DK, q, zero)


def _flash_update(s, v, mask, m_ref, l_ref, acc_ref):
    if mask is not None:
        s = jnp.where(mask, s, MASK_VALUE)
    pad = -s.shape[1] % HEAD_W
    if pad:
        s = jnp.concatenate([s, jnp.full((s.shape[0], pad), MASK_VALUE, F32)], axis=1)
        v = jnp.concatenate([v, jnp.zeros((pad, v.shape[1]), v.dtype)], axis=0)
    blocks =[s[:, b * HEAD_W:(b + 1) * HEAD_W] for b in range(s.shape[1] // HEAD_W)]
    m_prev = m_ref[...]
    m_new = jnp.maximum(m_prev, jnp.max(functools.reduce(jnp.maximum, blocks), axis=-1, keepdims=True))
    alpha = jnp.exp2(m_prev - m_new)
    probs = [jnp.exp2(blk - m_new) for blk in blocks]
    l_ref[...] = alpha * l_ref[...] + functools.reduce(jnp.add, probs)
    p = jnp.concatenate([x.astype(BF16) for x in probs], axis=1)
    acc_ref[...] = alpha * acc_ref[...] + jnp.dot(p, v, preferred_element_type=F32)
    m_ref[...] = m_new


def _flash_init(m_ref, l_ref, acc_ref):
    m_ref[...] = jnp.full(m_ref.shape, MASK_VALUE, F32)
    l_ref[...] = jnp.zeros(l_ref.shape, F32)
    acc_ref[...] = jnp.zeros(acc_ref.shape, F32)


def _diff_finish(lam, lam_init, gn, stats1, stats2):
    (_, l1_ref, acc1_ref), (_, l2_ref, acc2_ref) = stats1, stats2
    l1 = jnp.sum(l1_ref[...], axis=-1, keepdims=True)
    l2 = jnp.sum(l2_ref[...], axis=-1, keepdims=True)
    o = acc1_ref[...] / l1 - lam * (acc2_ref[...] / l2)
    return (_rms(o, gn) * (1.0 - lam_init)).astype(BF16)


def _attn_prompt_kernel(lq1_ref, lk1_ref, lq2_ref, lk2_ref, gn_ref, q_ref, k_ref, v_ref, o_ref,
                        *scratch, tile, lam_init):
    i = pl.program_id(1)
    stats = (scratch[0:3], scratch[3:6])
    lam = _lambda(lq1_ref, lk1_ref, lq2_ref, lk2_ref, lam_init)
    row_chunk = lax.broadcasted_iota(jnp.int32, (tile, tile), 0) // CHUNK
    col_chunk = lax.broadcasted_iota(jnp.int32, (tile, tile), 1) // CHUNK
    diag_mask = col_chunk <= row_chunk
    for h in range(HEADS):
        cols = slice(h * HEAD_W, (h + 1) * HEAD_W)
        comps = _split_components(q_ref[:, cols])
        for c in range(2):
            _flash_init(*stats[c])

        def step(j, mask):
            rows = pl.ds(pl.multiple_of(j * tile, tile), tile)
            k = k_ref[rows, cols]
            v = v_ref[rows, cols]
            for c in range(2):
                s = lax.dot_general(comps[c], k, _NT, preferred_element_type=F32)
                _flash_update(s, v, mask, *stats[c])

        def body(j, carry):
            step(j, None)
            return carry

        lax.fori_loop(0, i, body, 0)
        step(i, diag_mask)
        o_ref[:, cols] = _diff_finish(lam, lam_init, gn_ref[...], stats[0], stats[1])


def _attn_prompt(p, lq1, lk1, lq2, lk2, gn, lam_init):
    b, seq_len, _ = p.shape
    tile = ATTN_TILE
    small = lambda n: pl.BlockSpec((1, n), lambda bi, qi: (0, 0))
    kv = lambda g: pl.BlockSpec((None, seq_len, GROUP_W), lambda bi, qi, g=g: (bi, 0, g))
    return pl.pallas_call(
        functools.partial(_attn_prompt_kernel, tile=tile, lam_init=lam_init),
        grid=(b, seq_len // tile),
        in_specs=[small(DIFF_DK)] * 4 + [small(HEAD_W),
                  pl.BlockSpec((None, tile, GROUP_W), lambda bi, qi: (bi, qi, 4)), kv(5), kv(6)],
        out_specs=pl.BlockSpec((None, tile, GROUP_W), lambda bi, qi: (bi, qi, 0)),
        out_shape=jax.ShapeDtypeStruct((b, seq_len, GROUP_W), BF16),
        scratch_shapes=[pltpu.VMEM((tile, HEAD_W), F32)] * 6,
        compiler_params=_params("parallel", "arbitrary"),
        name="attn_prompt",
    )(lq1, lk1, lq2, lk2, gn, p, p, p)


def _attn_sample_kernel(lq1_ref, lk1_ref, lq2_ref, lk2_ref, gn_ref, q_ref, kp_ref, vp_ref, kn_ref, vn_ref,
                        o_ref, m_scr, l_scr, acc_scr, *, n_past, lam_init):
    j = pl.program_id(1)

    @pl.when(j == 0)
    def _():
        _flash_init(m_scr, l_scr, acc_scr)

    def update(scores_of, value_of):
        for h in range(HEADS):
            v = value_of(h).astype(BF16)
            for c in range(2):
                idx = 2 * h + c
                feats = slice(h * HEAD_W + c * DIFF_DK, h * HEAD_W + (c + 1) * DIFF_DK)
                _flash_update(scores_of(q_ref[:, feats], feats), v, None,
                              m_scr.at[idx], l_scr.at[idx], acc_scr.at[idx])

    @pl.when(j < n_past)
    def _():
        update(lambda q, feats: jnp.dot(q, kp_ref[feats, :].astype(BF16), preferred_element_type=F32),
               lambda h: vp_ref[:, h, :])

    @pl.when(j == n_past)
    def _():
        update(lambda q, feats: lax.dot_general(q, kn_ref[:, feats], _NT, preferred_element_type=F32),
               lambda h: vn_ref[:, h * HEAD_W:(h + 1) * HEAD_W])
        lam = _lambda(lq1_ref, lk1_ref, lq2_ref, lk2_ref, lam_init)
        for h in range(HEADS):
            stats = [(m_scr.at[2 * h + c], l_scr.at[2 * h + c], acc_scr.at[2 * h + c]) for c in range(2)]
            o_ref[:, h * HEAD_W:(h + 1) * HEAD_W] = _diff_finish(lam, lam_init, gn_ref[...], *stats)


def _attn_sample(p, k_past, v_past, lq1, lk1, lq2, lk2, gn, lam_init):
    b, seq_len, _ = p.shape
    past_len = k_past.shape[1]
    tk = min(PAST_TILE, past_len)
    n_past = past_len // tk
    k_past = jnp.transpose(k_past, (0, 2, 3, 4, 1)).reshape(b, GROUP_W, past_len)
    small = lambda n: pl.BlockSpec((1, n), lambda bi, ji: (0, 0))
    new = lambda g: pl.BlockSpec((None, seq_len, GROUP_W), lambda bi, ji, g=g: (bi, 0, g))
    past_k = pl.BlockSpec((None, GROUP_W, tk), lambda bi, ji: (bi, 0, jnp.minimum(ji, n_past - 1)))
    past_v = pl.BlockSpec((None, tk, HEADS, HEAD_W), lambda bi, ji: (bi, jnp.minimum(ji, n_past - 1), 0, 0))
    return pl.pallas_call(
        functools.partial(_attn_sample_kernel, n_past=n_past, lam_init=lam_init),
        grid=(b, n_past + 1),
        in_specs=[small(DIFF_DK)] * 4 + [small(HEAD_W), new(4), past_k, past_v, new(5), new(6)],
        out_specs=pl.BlockSpec((None, seq_len, GROUP_W), lambda bi, ji: (bi, 0, 0)),
        out_shape=jax.ShapeDtypeStruct((b, seq_len, GROUP_W), BF16),
        scratch_shapes=[pltpu.VMEM((2 * HEADS, seq_len, HEAD_W), F32)] * 3,
        compiler_params=_params("parallel", "arbitrary"),
        name="attn_sample",
    )(lq1, lk1, lq2, lk2, gn, p, k_past, v_past, p, p)


def _layer(x, ret_s0, k_past, v_past, w, lam_init):
    (f1_pre, f1_wg, f1_wu, f1_wd, f1_post, mix_pre, w_in, ret_g, lq1, lk1, lq2, lk2, diff_g, w_out,
     mix_post, f2_pre, f2_wg, f2_wu, f2_wd, f2_post) = w
    b, seq_len, _ = x.shape
    past_len = 0 if k_past is None else k_past.shape[1]
    x = x.reshape(b * seq_len, D_MODEL)
    x1 = _ffn(x, f1_pre, f1_wg, f1_wu, f1_wd, f1_post)
    p, k_new, v_new = _mix_in(x1, mix_pre, w_in, seq_len, past_len)
    ret_y, s_new = _retention(p, ret_s0, ret_g, seq_len)
    if k_past is None:
        diff_y = _attn_prompt(p, lq1, lk1, lq2, lk2, diff_g, lam_init)
    else:
        diff_y = _attn_sample(p, k_past, v_past, lq1, lk1, lq2, lk2, diff_g, lam_init)
    x3 = _out_ffn(x1, ret_y.reshape(b * seq_len, GROUP_W), diff_y.reshape(b * seq_len, GROUP_W),
                  w_out, mix_post, f2_pre, f2_wg, f2_wu, f2_wd, f2_post)
    return x3.reshape(b, seq_len, D_MODEL), s_new, k_new, v_new


def kernel(x_prompt, x_sample, state_ret, cache_diff_k, cache_diff_v, ffn1_pre_g, ffn1_w_gate, ffn1_w_up, ffn1_w_down, ffn1_post_g, mix_pre_g, w_in, ret_norm_g, diff_lq1, diff_lk1, diff_lq2, diff_lk2, diff_norm_g, w_out, mix_post_g, ffn2_pre_g, ffn2_w_gate, ffn2_w_up, ffn2_w_down, ffn2_post_g):
    depth = w_in.shape[0]
    xp, xs = x_prompt, x_sample
    outs = [[] for _ in range(6)]
    for li in range(depth):
        lam_init = 0.8 - 0.6 * math.exp(-0.3 * li)
        row = lambda a: a[li].reshape(1, -1)
        mat = lambda a: a[li].astype(BF16)
        w = (row(ffn1_pre_g), mat(ffn1_w_gate), mat(ffn1_w_up), mat(ffn1_w_down), row(ffn1_post_g),
             row(mix_pre_g), mat(w_in), row(ret_norm_g), row(diff_lq1), row(diff_lk1), row(diff_lq2),
             row(diff_lk2), row(diff_norm_g), mat(w_out), row(mix_post_g),
             row(ffn2_pre_g), mat(ffn2_w_gate), mat(ffn2_w_up), mat(ffn2_w_down), row(ffn2_post_g))
        s0 = jnp.zeros((xp.shape[0], HEADS, RET_DK, HEAD_W), F32)
        xp, s_p, k_p, v_p = _layer(xp, s0, None, None, w, lam_init)
        xs, s_s, k_s, v_s = _layer(xs, state_ret[li], cache_diff_k[li], cache_diff_v[li], w, lam_init)
        for acc, val in zip(outs, (s_p, k_p, v_p, s_s, k_s, v_s)):
            acc.append(val)
    return (xp, xs) + tuple(jnp.stack(o) for o in outs)
```

```python
import functools
import math

import jax
import jax.numpy as jnp
from jax import lax
from jax.experimental import pallas as pl
from jax.experimental.pallas import tpu as pltpu

D_MODEL = 1024
D_FF = 2816
CHUNK = 64
HEADS = 4
HEAD_W = 128
GROUP_W = HEADS * HEAD_W
N_GROUPS = 7
IN_COLS = N_GROUPS * GROUP_W
RET_DK = 128
DIFF_DK = 64
ROT_DIM = DIFF_DK // 4
ROPE_THETA = 500000.0
RET_THETA = 10000.0
EPS = 1e-6
MASK_VALUE = -1e30
LOG2E = math.log2(math.e)

ROW_TILE = 512
RET_CHUNK = 256
ATTN_TILE = 512
ATTN_Q_SUB = 256
PAST_TILE = 1024
VMEM_LIMIT = 56 * 1024 * 1024

F32 = jnp.float32
BF16 = jnp.bfloat16
_NT = (((1,), (1,)), ((), ()))
_TN = (((0,), (0,)), ((), ()))


def _rms(x, g):
    return x * lax.rsqrt(jnp.mean(x * x, axis=-1, keepdims=True) + EPS) * g


def _silu(x):
    return x * (1.0 / (1.0 + jnp.exp(-x)))


def _resident(shape):
    nd = len(shape)
    return pl.BlockSpec(shape, lambda *_: (0,) * nd, pipeline_mode=pl.Buffered(1))


def _params(*sem):
    return pltpu.CompilerParams(dimension_semantics=sem, vmem_limit_bytes=VMEM_LIMIT)


def _ffn_math(x, pre_g, wg_ref, wu_ref, wd_ref, post_g):
    h = _rms(x, pre_g).astype(BF16)
    gate = jnp.dot(h, wg_ref[...], preferred_element_type=F32)
    up = jnp.dot(h, wu_ref[...], preferred_element_type=F32)
    act = (_silu(gate) * up).astype(BF16)
    y = jnp.dot(act, wd_ref[...], preferred_element_type=F32)
    return x + 0.5 * _rms(y, post_g)


def _ffn_kernel(x_ref, pre_ref, wg_ref, wu_ref, wd_ref, post_ref, o_ref):
    o_ref[...] = _ffn_math(x_ref[...], pre_ref[...], wg_ref, wu_ref, wd_ref, post_ref[...])


def _ffn(x, pre_g, wg, wu, wd, post_g):
    m = x.shape[0]
    tm = min(ROW_TILE, m)
    row = pl.BlockSpec((tm, D_MODEL), lambda i: (i, 0))
    return pl.pallas_call(
        _ffn_kernel,
        grid=(m // tm,),
        in_specs=[row, _resident((1, D_MODEL)), _resident((D_MODEL, D_FF)), _resident((D_MODEL, D_FF)),
                  _resident((D_FF, D_MODEL)), _resident((1, D_MODEL))],
        out_specs=row,
        out_shape=jax.ShapeDtypeStruct((m, D_MODEL), F32),
        compiler_params=_params("parallel"),
        name="ffn",
    )(x, pre_g, wg, wu, wd, post_g)


def _out_ffn_kernel(x_ref, ry_ref, dy_ref, wo_ref, mixpost_ref, pre_ref, wg_ref, wu_ref, wd_ref, post_ref,
                    o_ref):
    y = (jnp.dot(ry_ref[...], wo_ref[:GROUP_W, :], preferred_element_type=F32)
         + jnp.dot(dy_ref[...], wo_ref[GROUP_W:, :], preferred_element_type=F32))
    x2 = x_ref[...] + _rms(y, mixpost_ref[...])
    o_ref[...] = _ffn_math(x2, pre_ref[...], wg_ref, wu_ref, wd_ref, post_ref[...])


def _out_ffn(x, ret_y, diff_y, wo, mixpost_g, pre_g, wg, wu, wd, post_g):
    m = x.shape[0]
    tm = min(ROW_TILE, m)
    row = pl.BlockSpec((tm, D_MODEL), lambda i: (i, 0))
    half = pl.BlockSpec((tm, GROUP_W), lambda i: (i, 0))
    return pl.pallas_call(
        _out_ffn_kernel,
        grid=(m // tm,),
        in_specs=[row, half, half, _resident((D_MODEL, D_MODEL)), _resident((1, D_MODEL)),
                  _resident((1, D_MODEL)), _resident((D_MODEL, D_FF)), _resident((D_MODEL, D_FF)),
                  _resident((D_FF, D_MODEL)), _resident((1, D_MODEL))],
        out_specs=row,
        out_shape=jax.ShapeDtypeStruct((m, D_MODEL), F32),
        compiler_params=_params("parallel"),
        name="out_ffn",
    )(x, ret_y, diff_y, wo, mixpost_g, pre_g, wg, wu, wd, post_g)


def _rope_tables(pos):
    pos = pos.astype(F32)[:, None]
    half = RET_DK // 2
    inv = jnp.power(jnp.float32(RET_THETA), -jnp.arange(half, dtype=F32) * (2.0 / RET_DK))
    ang = pos * inv[None, :]
    cos, sin = jnp.cos(ang), jnp.sin(ang)
    cr = jnp.concatenate([cos, cos], axis=1)
    sr = jnp.concatenate([-sin, sin], axis=1)
    half = ROT_DIM // 2
    inv = jnp.power(jnp.float32(ROPE_THETA), -jnp.arange(half, dtype=F32) * (2.0 / ROT_DIM))
    ang = pos * inv[None, :]
    cos, sin = jnp.cos(ang), jnp.sin(ang)
    n = pos.shape[0]
    ones = jnp.ones((n, DIFF_DK - ROT_DIM), F32)
    zeros = jnp.zeros((n, DIFF_DK - ROT_DIM), F32)
    z8 = jnp.zeros((n, half), F32)
    c_comp = jnp.concatenate([cos, cos, ones], axis=1)
    lo_comp = jnp.concatenate([-sin, z8, zeros], axis=1)
    hi_comp = jnp.concatenate([z8, sin, zeros], axis=1)
    cd = jnp.concatenate([c_comp, c_comp], axis=1)
    s_lo = jnp.concatenate([lo_comp, lo_comp], axis=1)
    s_hi = jnp.concatenate([hi_comp, hi_comp], axis=1)
    return cr, sr, cd, s_lo, s_hi


def _mix_in_kernel(x_ref, g_ref, w_ref, cr_ref, sr_ref, cd_ref, slo_ref, shi_ref, p_ref, kd_ref, vd_ref,
                   *, transpose_k):
    h = _rms(x_ref[...], g_ref[...]).astype(BF16)
    for grp in range(N_GROUPS):
        pg = jnp.dot(h, w_ref[:, grp * GROUP_W:(grp + 1) * GROUP_W], preferred_element_type=F32)
        for t in range(HEADS):
            cols = slice(t * HEAD_W, (t + 1) * HEAD_W)
            xt = pg[:, cols]
            if grp in (0, 1):
                xt = xt * cr_ref[...] + pltpu.roll(xt, RET_DK // 2, 1) * sr_ref[...]
                if grp == 1:
                    xt = xt * (RET_DK ** -0.5)
            elif grp in (4, 5):
                xt = (xt * cd_ref[...] + pltpu.roll(xt, HEAD_W - ROT_DIM // 2, 1) * slo_ref[...]
                      + pltpu.roll(xt, ROT_DIM // 2, 1) * shi_ref[...])
                if grp == 5 and transpose_k:
                    kd_ref[cols, :] = xt.T
                elif grp == 5:
                    kd_ref[:, cols] = xt
                else:
                    xt = xt * (DIFF_DK ** -0.5 * LOG2E)
            elif grp == 6:
                vd_ref[:, t, :] = xt
            p_ref[:, grp * GROUP_W + t * HEAD_W:grp * GROUP_W + (t + 1) * HEAD_W] = xt.astype(BF16)


def _mix_in(x, g, w, seq_len, pos0):
    m = x.shape[0]
    tm = min(ROW_TILE, m)
    n_tab = max(seq_len, tm)
    pos = pos0 + (jnp.arange(n_tab, dtype=jnp.int32) % seq_len)
    tables = _rope_tables(pos)
    n_tab_blocks = n_tab // tm
    row = pl.BlockSpec((tm, D_MODEL), lambda i: (i, 0))
    tab = pl.BlockSpec((tm, HEAD_W), lambda i: (i % n_tab_blocks, 0))
    transpose_k = seq_len % tm == 0
    if transpose_k:
        n_seq_blocks = seq_len // tm
        k_spec = pl.BlockSpec((None, GROUP_W, tm), lambda i: (i // n_seq_blocks, 0, i % n_seq_blocks))
        k_shape = jax.ShapeDtypeStruct((m // seq_len, GROUP_W, seq_len), F32)
    else:
        k_spec = pl.BlockSpec((tm, GROUP_W), lambda i: (i, 0))
        k_shape = jax.ShapeDtypeStruct((m, GROUP_W), F32)
    p, k_new, v_new = pl.pallas_call(
        functools.partial(_mix_in_kernel, transpose_k=transpose_k),
        grid=(m // tm,),
        in_specs=[row, _resident((1, D_MODEL)), _resident((D_MODEL, IN_COLS)), tab, tab, tab, tab, tab],
        out_specs=[pl.BlockSpec((tm, IN_COLS), lambda i: (i, 0)), k_spec,
                   pl.BlockSpec((tm, HEADS, HEAD_W), lambda i: (i, 0, 0))],
        out_shape=[jax.ShapeDtypeStruct((m, IN_COLS), BF16), k_shape,
                   jax.ShapeDtypeStruct((m, HEADS, HEAD_W), F32)],
        compiler_params=_params("parallel"),
        name="mix_in",
    )(x, g, w, *tables)
    b = m // seq_len
    if transpose_k:
        k_new = jnp.transpose(k_new.reshape(b, HEADS, 2, DIFF_DK, seq_len), (0, 4, 1, 2, 3))
    else:
        k_new = k_new.reshape(b, seq_len, HEADS, 2, DIFF_DK)
    return p.reshape(b, seq_len, IN_COLS), k_new, v_new.reshape(b, seq_len, HEADS, HEAD_W)


def _ret_tables(c):
    log_g = jnp.log(1.0 - jnp.power(2.0, -5.0 - jnp.arange(HEADS, dtype=F32)))
    idx = jnp.arange(c, dtype=F32)
    rel = idx[:, None] - idx[None, :]
    dmask = jnp.where(rel >= 0, jnp.exp(log_g[:, None, None] * jnp.maximum(rel, 0.0)), 0.0)
    q_dec = jnp.exp(log_g[:, None] * (idx + 1.0)[None, :])
    k_dec = jnp.exp(log_g[:, None] * (c - 1.0 - idx)[None, :])
    b_dec = jnp.exp(log_g * c)
    bcast = lambda a: jnp.broadcast_to(a[:, :, None], (HEADS, c, HEAD_W))
    return dmask, bcast(q_dec), bcast(k_dec), jnp.broadcast_to(b_dec[:, None, None], (HEADS, 8, HEAD_W))


def _ret_kernel(q_ref, k_ref, v_ref, g_ref, s0_ref, dmask_ref, qdec_ref, kdec_ref, bdec_ref, gn_ref,
                y_ref, snew_ref, s_scr, *, c, n_c):
    t = pl.program_id(1)

    @pl.when(t == 0)
    def _():
        s_scr[...] = s0_ref[...]

    for h in range(HEADS):
        cols = slice(h * HEAD_W, (h + 1) * HEAD_W)
        state = s_scr[h]
        b_dec = bdec_ref[h][0:1, :]
        for ci in range(n_c):
            rows = slice(ci * c, (ci + 1) * c)
            q = q_ref[rows, cols]
            k = k_ref[rows, cols]
            v = v_ref[rows, cols]
            scores = lax.dot_general(q, k, _NT, preferred_element_type=F32) * dmask_ref[h]
            o = (jnp.dot(scores.astype(BF16), v, preferred_element_type=F32)
                 + qdec_ref[h] * jnp.dot(q, state.astype(BF16), preferred_element_type=F32))
            k_scaled = (k.astype(F32) * kdec_ref[h]).astype(BF16)
            state = b_dec * state + lax.dot_general(k_scaled, v, _TN, preferred_element_type=F32)
            mu = jnp.mean(o, axis=-1, keepdims=True)
            d = o - mu
            var = jnp.mean(d * d, axis=-1, keepdims=True)
            normed = d * lax.rsqrt(var + EPS) * gn_ref[:, cols]
            y_ref[rows, cols] = (_silu(g_ref[rows, cols].astype(F32)) * normed).astype(BF16)
        s_scr[h] = state

    @pl.when(t == pl.num_programs(1) - 1)
    def _():
        snew_ref[...] = s_scr[...]


def _retention(p, s0, gn, seq_len):
    b = p.shape[0]
    c = min(RET_CHUNK, seq_len)
    n_c = min(2, seq_len // c)
    t_rows = c * n_c
    tables = _ret_tables(c)
    col = lambda g: pl.BlockSpec((None, t_rows, GROUP_W), lambda bi, ti, g=g: (bi, ti, g))
    state = pl.BlockSpec((None, HEADS, RET_DK, HEAD_W), lambda bi, ti: (bi, 0, 0, 0))
    return pl.pallas_call(
        functools.partial(_ret_kernel, c=c, n_c=n_c),
        grid=(b, seq_len // t_rows),
        in_specs=[col(0), col(1), col(2), col(3), state,
                  _resident((HEADS, c, c)), _resident((HEADS, c, HEAD_W)), _resident((HEADS, c, HEAD_W)),
                  _resident((HEADS, 8, HEAD_W)), _resident((1, GROUP_W))],
        out_specs=[pl.BlockSpec((None, t_rows, GROUP_W), lambda bi, ti: (bi, ti, 0)), state],
        out_shape=[jax.ShapeDtypeStruct((b, seq_len, GROUP_W), BF16),
                   jax.ShapeDtypeStruct((b, HEADS, RET_DK, HEAD_W), F32)],
        scratch_shapes=[pltpu.VMEM((HEADS, RET_DK, HEAD_W), F32)],
        compiler_params=_params("parallel", "arbitrary"),
        name="retention",
    )(p, p, p, p, s0, *tables, gn)


def _lambda(lq1_ref, lk1_ref, lq2_ref, lk2_ref, lam_init):
    return (jnp.exp(jnp.sum(lq1_ref[...] * lk1_ref[...], axis=-1, keepdims=True))
            - jnp.exp(jnp.sum(lq2_ref[...] * lk2_ref[...], axis=-1, keepdims=True)) + lam_init)


def _split_components(q):
    lane = lax.broadcasted_iota(jnp.int32, q.shape, 1)
    zero = jnp.zeros_like(q)
    return jnp.where(lane < DIFF_DK, q, zero), jnp.where(lane >= DIFF_DK, q, zero)


def _flash_update(s, v, mask, m_ref, l_ref, acc_ref):
    if mask is not None:
        s = jnp.where(mask, s, MASK_VALUE)
    pad = -s.shape[1] % HEAD_W
    if pad:
        s = jnp.concatenate([s, jnp.full((s.shape[0], pad), MASK_VALUE, F32)], axis=1)
        v = jnp.concatenate([v, jnp.zeros((pad, v.shape[1]), v.dtype)], axis=0)
    blocks =[s[:, b * HEAD_W:(b + 1) * HEAD_W] for b in range(s.shape[1] // HEAD_W)]
    m_prev = m_ref[...]
    m_new = jnp.maximum(m_prev, jnp.max(functools.reduce(jnp.maximum, blocks), axis=-1, keepdims=True))
    alpha = jnp.exp2(m_prev - m_new)
    probs = [jnp.exp2(blk - m_new) for blk in blocks]
    l_ref[...] = alpha * l_ref[...] + functools.reduce(jnp.add, probs)
    p = jnp.concatenate([x.astype(BF16) for x in probs], axis=1)
    acc_ref[...] = alpha * acc_ref[...] + jnp.dot(p, v, preferred_element_type=F32)
    m_ref[...] = m_new


def _flash_init(m_ref, l_ref, acc_ref):
    m_ref[...] = jnp.full(m_ref.shape, MASK_VALUE, F32)
    l_ref[...] = jnp.zeros(l_ref.shape, F32)
    acc_ref[...] = jnp.zeros(acc_ref.shape, F32)


def _diff_finish(lam, lam_init, gn, stats1, stats2):
    (_, l1_ref, acc1_ref), (_, l2_ref, acc2_ref) = stats1, stats2
    l1 = jnp.sum(l1_ref[...], axis=-1, keepdims=True)
    l2 = jnp.sum(l2_ref[...], axis=-1, keepdims=True)
    o = acc1_ref[...] / l1 - lam * (acc2_ref[...] / l2)
    return (_rms(o, gn) * (1.0 - lam_init)).astype(BF16)


def _flash_update_t(s_t, v, mask, m_ref, l_ref, acc_ref):
    if mask is not None:
        s_t = jnp.where(mask, s_t, MASK_VALUE)
    m_prev = m_ref[...]
    m_new = jnp.maximum(m_prev, jnp.max(s_t, axis=0, keepdims=True))
    alpha = jnp.exp2(m_prev - m_new)
    p_t = jnp.exp2(s_t - m_new)
    l_ref[...] = alpha * l_ref[...] + jnp.sum(p_t, axis=0, keepdims=True)
    acc_ref[...] = alpha * acc_ref[...] + lax.dot_general(v, p_t.astype(BF16), _TN, preferred_element_type=F32)
    m_ref[...] = m_new


def _attn_prompt_kernel(lq1_ref, lk1_ref, lq2_ref, lk2_ref, gn_ref, q_ref, k_ref, v_ref, o_ref,
                        *scratch, tile, lam_init):
    i = pl.program_id(1)
    n_sub = tile // ATTN_Q_SUB
    chains = [(qs, c) for qs in range(n_sub) for c in range(2)]
    stats = {ch: scratch[3 * n:3 * n + 3] for n, ch in enumerate(chains)}
    slot_a, slot_b = scratch[3 * len(chains):]
    lam = _lambda(lq1_ref, lk1_ref, lq2_ref, lk2_ref, lam_init)
    key_chunk = lax.broadcasted_iota(jnp.int32, (tile, tile), 0) // CHUNK
    query_chunk = lax.broadcasted_iota(jnp.int32, (tile, tile), 1) // CHUNK
    diag_mask = key_chunk <= query_chunk
    for h in range(HEADS):
        cols = slice(h * HEAD_W, (h + 1) * HEAD_W)
        comps = _split_components(q_ref[:, cols])
        for ch in chains:
            _flash_init(*stats[ch])

        def key_rows(j):
            return pl.ds(pl.multiple_of(j * tile, tile), tile)

        def scores_into(slot, j):
            k = k_ref[key_rows(j), cols]
            for n, (qs, c) in enumerate(chains):
                slot[n] = lax.dot_general(k, comps[c][qs * ATTN_Q_SUB:(qs + 1) * ATTN_Q_SUB], _NT,
                                          preferred_element_type=F32)

        def accumulate(slot, j, mask):
            v = v_ref[key_rows(j), cols]
            for n, (qs, c) in enumerate(chains):
                sub_mask = None if mask is None else mask[:, qs * ATTN_Q_SUB:(qs + 1) * ATTN_Q_SUB]
                _flash_update_t(slot[n], v, sub_mask, *stats[(qs, c)])

        scores_into(slot_a, 0)

        def body(jj, carry):
            scores_into(slot_b, 2 * jj + 1)
            accumulate(slot_a, 2 * jj, None)
            scores_into(slot_a, 2 * jj + 2)
            accumulate(slot_b, 2 * jj + 1, None)
            return carry

        lax.fori_loop(0, i // 2, body, 0)

        @pl.when(i % 2 == 0)
        def _():
            accumulate(slot_a, i, diag_mask)

        @pl.when(i % 2 == 1)
        def _():
            scores_into(slot_b, i)
            accumulate(slot_a, i - 1, None)
            accumulate(slot_b, i, diag_mask)

        for qs in range(n_sub):
            (_, l1_ref, acc1_ref), (_, l2_ref, acc2_ref) = stats[(qs, 0)], stats[(qs, 1)]
            o_t = acc1_ref[...] / l1_ref[...] - lam * (acc2_ref[...] / l2_ref[...])
            ms = jnp.mean(o_t * o_t, axis=0, keepdims=True)
            o_t = o_t * lax.rsqrt(ms + EPS) * gn_ref[...] * (1.0 - lam_init)
            o_ref[qs * ATTN_Q_SUB:(qs + 1) * ATTN_Q_SUB, cols] = o_t.T.astype(BF16)


def _attn_prompt(p, lq1, lk1, lq2, lk2, gn, lam_init):
    b, seq_len, _ = p.shape
    tile = ATTN_TILE
    small = lambda n: pl.BlockSpec((1, n), lambda bi, qi: (0, 0))
    kv = lambda g: pl.BlockSpec((None, seq_len, GROUP_W), lambda bi, qi, g=g: (bi, 0, g))
    return pl.pallas_call(
        functools.partial(_attn_prompt_kernel, tile=tile, lam_init=lam_init),
        grid=(b, seq_len // tile),
        in_specs=[small(DIFF_DK)] * 4 + [pl.BlockSpec((HEAD_W, 1), lambda bi, qi: (0, 0)),
                  pl.BlockSpec((None, tile, GROUP_W), lambda bi, qi: (bi, qi, 4)), kv(5), kv(6)],
        out_specs=pl.BlockSpec((None, tile, GROUP_W), lambda bi, qi: (bi, qi, 0)),
        out_shape=jax.ShapeDtypeStruct((b, seq_len, GROUP_W), BF16),
        scratch_shapes=[pltpu.VMEM((1, ATTN_Q_SUB), F32), pltpu.VMEM((1, ATTN_Q_SUB), F32),
                        pltpu.VMEM((HEAD_W, ATTN_Q_SUB), F32)] * (2 * tile // ATTN_Q_SUB)
                       + [pltpu.VMEM((2 * tile // ATTN_Q_SUB, tile, ATTN_Q_SUB), F32)] * 2,
        compiler_params=_params("parallel", "arbitrary"),
        name="attn_prompt",
    )(lq1, lk1, lq2, lk2, gn.reshape(HEAD_W, 1), p, p, p)


def _attn_sample_kernel(lq1_ref, lk1_ref, lq2_ref, lk2_ref, gn_ref, q_ref, kp_ref, vp_ref, kn_ref, vn_ref,
                        o_ref, m_scr, l_scr, acc_scr, *, n_past, lam_init):
    j = pl.program_id(1)

    @pl.when(j == 0)
    def _():
        _flash_init(m_scr, l_scr, acc_scr)

    def update(scores_of, value_of):
        for h in range(HEADS):
            v = value_of(h).astype(BF16)
            for c in range(2):
                idx = 2 * h + c
                feats = slice(h * HEAD_W + c * DIFF_DK, h * HEAD_W + (c + 1) * DIFF_DK)
                _flash_update(scores_of(q_ref[:, feats], feats), v, None,
                              m_scr.at[idx], l_scr.at[idx], acc_scr.at[idx])

    @pl.when(j < n_past)
    def _():
        update(lambda q, feats: jnp.dot(q, kp_ref[feats, :].astype(BF16), preferred_element_type=F32),
               lambda h: vp_ref[:, h, :])

    @pl.when(j == n_past)
    def _():
        update(lambda q, feats: lax.dot_general(q, kn_ref[:, feats], _NT, preferred_element_type=F32),
               lambda h: vn_ref[:, h * HEAD_W:(h + 1) * HEAD_W])
        lam = _lambda(lq1_ref, lk1_ref, lq2_ref, lk2_ref, lam_init)
        for h in range(HEADS):
            stats = [(m_scr.at[2 * h + c], l_scr.at[2 * h + c], acc_scr.at[2 * h + c]) for c in range(2)]
            o_ref[:, h * HEAD_W:(h + 1) * HEAD_W] = _diff_finish(lam, lam_init, gn_ref[...], *stats)


def _attn_sample(p, k_past, v_past, lq1, lk1, lq2, lk2, gn, lam_init):
    b, seq_len, _ = p.shape
    past_len = k_past.shape[1]
    tk = min(PAST_TILE, past_len)
    n_past = past_len // tk
    k_past = jnp.transpose(k_past, (0, 2, 3, 4, 1)).reshape(b, GROUP_W, past_len)
    small = lambda n: pl.BlockSpec((1, n), lambda bi, ji: (0, 0))
    new = lambda g: pl.BlockSpec((None, seq_len, GROUP_W), lambda bi, ji, g=g: (bi, 0, g))
    past_k = pl.BlockSpec((None, GROUP_W, tk), lambda bi, ji: (bi, 0, jnp.minimum(ji, n_past - 1)))
    past_v = pl.BlockSpec((None, tk, HEADS, HEAD_W), lambda bi, ji: (bi, jnp.minimum(ji, n_past - 1), 0, 0))
    return pl.pallas_call(
        functools.partial(_attn_sample_kernel, n_past=n_past, lam_init=lam_init),
        grid=(b, n_past + 1),
        in_specs=[small(DIFF_DK)] * 4 + [small(HEAD_W), new(4), past_k, past_v, new(5), new(6)],
        out_specs=pl.BlockSpec((None, seq_len, GROUP_W), lambda bi, ji: (bi, 0, 0)),
        out_shape=jax.ShapeDtypeStruct((b, seq_len, GROUP_W), BF16),
        scratch_shapes=[pltpu.VMEM((2 * HEADS, seq_len, HEAD_W), F32)] * 3,
        compiler_params=_params("parallel", "arbitrary"),
        name="attn_sample",
    )(lq1, lk1, lq2, lk2, gn, p, k_past, v_past, p, p)


def _layer(x, ret_s0, k_past, v_past, w, lam_init):
    (f1_pre, f1_wg, f1_wu, f1_wd, f1_post, mix_pre, w_in, ret_g, lq1, lk1, lq2, lk2, diff_g, w_out,
     mix_post, f2_pre, f2_wg, f2_wu, f2_wd, f2_post) = w
    b, seq_len, _ = x.shape
    past_len = 0 if k_past is None else k_past.shape[1]
    x = x.reshape(b * seq_len, D_MODEL)
    x1 = _ffn(x, f1_pre, f1_wg, f1_wu, f1_wd, f1_post)
    p, k_new, v_new = _mix_in(x1, mix_pre, w_in, seq_len, past_len)
    ret_y, s_new = _retention(p, ret_s0, ret_g, seq_len)
    if k_past is None:
        diff_y = _attn_prompt(p, lq1, lk1, lq2, lk2, diff_g, lam_init)
    else:
        diff_y = _attn_sample(p, k_past, v_past, lq1, lk1, lq2, lk2, diff_g, lam_init)
    x3 = _out_ffn(x1, ret_y.reshape(b * seq_len, GROUP_W), diff_y.reshape(b * seq_len, GROUP_W),
                  w_out, mix_post, f2_pre, f2_wg, f2_wu, f2_wd, f2_post)
    return x3.reshape(b, seq_len, D_MODEL), s_new, k_new, v_new


def kernel(x_prompt, x_sample, state_ret, cache_diff_k, cache_diff_v, ffn1_pre_g, ffn1_w_gate, ffn1_w_up, ffn1_w_down, ffn1_post_g, mix_pre_g, w_in, ret_norm_g, diff_lq1, diff_lk1, diff_lq2, diff_lk2, diff_norm_g, w_out, mix_post_g, ffn2_pre_g, ffn2_w_gate, ffn2_w_up, ffn2_w_down, ffn2_post_g):
    depth = w_in.shape[0]
    xp, xs = x_prompt, x_sample
    outs = [[] for _ in range(6)]
    for li in range(depth):
        lam_init = 0.8 - 0.6 * math.exp(-0.3 * li)
        row = lambda a: a[li].reshape(1, -1)
        mat = lambda a: a[li].astype(BF16)
        w = (row(ffn1_pre_g), mat(ffn1_w_gate), mat(ffn1_w_up), mat(ffn1_w_down), row(ffn1_post_g),
             row(mix_pre_g), mat(w_in), row(ret_norm_g), row(diff_lq1), row(diff_lk1), row(diff_lq2),
             row(diff_lk2), row(diff_norm_g), mat(w_out), row(mix_post_g),
             row(ffn2_pre_g), mat(ffn2_w_gate), mat(ffn2_w_up), mat(ffn2_w_down), row(ffn2_post_g))
        s0 = jnp.zeros((xp.shape[0], HEADS, RET_DK, HEAD_W), F32)
        xp, s_p, k_p, v_p = _layer(xp, s0, None, None, w, lam_init)
        xs, s_s, k_s, v_s = _layer(xs, state_ret[li], cache_diff_k[li], cache_diff_v[li], w, lam_init)
        for acc, val in zip(outs, (s_p, k_p, v_p, s_s, k_s, v_s)):
            acc.append(val)
    return (xp, xs) + tuple(jnp.stack(o) for o in outs)
```

```python
import functools
import math

import jax
import jax.numpy as jnp
from jax import lax
from jax.experimental import pallas as pl
from jax.experimental.pallas import tpu as pltpu

D_MODEL = 1024
D_FF = 2816
CHUNK = 64
HEADS = 4
HEAD_W = 128
GROUP_W = HEADS * HEAD_W
N_GROUPS = 7
IN_COLS = N_GROUPS * GROUP_W
RET_DK = 128
DIFF_DK = 64
ROT_DIM = DIFF_DK // 4
ROPE_THETA = 500000.0
RET_THETA = 10000.0
EPS = 1e-6
MASK_VALUE = -1e30
LOG2E = math.log2(math.e)

ROW_TILE = 512
FFN_ROW_GROUPS = 2
RET_CHUNK = 256
ATTN_TILE = 512
ATTN_Q_SUB = 256
PAST_TILE = 1024
VMEM_LIMIT = 56 * 1024 * 1024

F32 = jnp.float32
BF16 = jnp.bfloat16
_NT = (((1,), (1,)), ((), ()))
_TN = (((0,), (0,)), ((), ()))


def _rms(x, g):
    return x * lax.rsqrt(jnp.mean(x * x, axis=-1, keepdims=True) + EPS) * g


def _silu(x):
    return x * (1.0 / (1.0 + jnp.exp(-x)))


def _resident(shape):
    nd = len(shape)
    return pl.BlockSpec(shape, lambda *_: (0,) * nd, pipeline_mode=pl.Buffered(1))


def _params(*sem):
    return pltpu.CompilerParams(dimension_semantics=sem, vmem_limit_bytes=VMEM_LIMIT)


def _ffn_math(x, pre_g, wg_ref, wu_ref, wd_ref, post_g):
    n = FFN_ROW_GROUPS if x.shape[0] % (8 * FFN_ROW_GROUPS) == 0 else 1
    rows = x.shape[0] // n
    xs = [x[g * rows:(g + 1) * rows] for g in range(n)]
    hs = [_rms(xg, pre_g).astype(BF16) for xg in xs]
    gates_ups = [(jnp.dot(h, wg_ref[...], preferred_element_type=F32),
                  jnp.dot(h, wu_ref[...], preferred_element_type=F32)) for h in hs]
    acts = [(_silu(gate) * up).astype(BF16) for gate, up in gates_ups]
    ys = [jnp.dot(act, wd_ref[...], preferred_element_type=F32) for act in acts]
    outs = [xg + 0.5 * _rms(y, post_g) for xg, y in zip(xs, ys)]
    return outs[0] if n == 1 else jnp.concatenate(outs, axis=0)


def _ffn_kernel(x_ref, pre_ref, wg_ref, wu_ref, wd_ref, post_ref, o_ref):
    o_ref[...] = _ffn_math(x_ref[...], pre_ref[...], wg_ref, wu_ref, wd_ref, post_ref[...])


def _ffn(x, pre_g, wg, wu, wd, post_g):
    m = x.shape[0]
    tm = min(ROW_TILE, m)
    row = pl.BlockSpec((tm, D_MODEL), lambda i: (i, 0))
    return pl.pallas_call(
        _ffn_kernel,
        grid=(m // tm,),
        in_specs=[row, _resident((1, D_MODEL)), _resident((D_MODEL, D_FF)), _resident((D_MODEL, D_FF)),
                  _resident((D_FF, D_MODEL)), _resident((1, D_MODEL))],
        out_specs=row,
        out_shape=jax.ShapeDtypeStruct((m, D_MODEL), F32),
        compiler_params=_params("parallel"),
        name="ffn",
    )(x, pre_g, wg, wu, wd, post_g)


def _out_ffn_kernel(x_ref, ry_ref, dy_ref, wo_ref, mixpost_ref, pre_ref, wg_ref, wu_ref, wd_ref, post_ref,
                    o_ref):
    y = (jnp.dot(ry_ref[...], wo_ref[:GROUP_W, :], preferred_element_type=F32)
         + jnp.dot(dy_ref[...], wo_ref[GROUP_W:, :], preferred_element_type=F32))
    x2 = x_ref[...] + _rms(y, mixpost_ref[...])
    o_ref[...] = _ffn_math(x2, pre_ref[...], wg_ref, wu_ref, wd_ref, post_ref[...])


def _out_ffn(x, ret_y, diff_y, wo, mixpost_g, pre_g, wg, wu, wd, post_g):
    m = x.shape[0]
    tm = min(ROW_TILE, m)
    row = pl.BlockSpec((tm, D_MODEL), lambda i: (i, 0))
    half = pl.BlockSpec((tm, GROUP_W), lambda i: (i, 0))
    return pl.pallas_call(
        _out_ffn_kernel,
        grid=(m // tm,),
        in_specs=[row, half, half, _resident((D_MODEL, D_MODEL)), _resident((1, D_MODEL)),
                  _resident((1, D_MODEL)), _resident((D_MODEL, D_FF)), _resident((D_MODEL, D_FF)),
                  _resident((D_FF, D_MODEL)), _resident((1, D_MODEL))],
        out_specs=row,
        out_shape=jax.ShapeDtypeStruct((m, D_MODEL), F32),
        compiler_params=_params("parallel"),
        name="out_ffn",
    )(x, ret_y, diff_y, wo, mixpost_g, pre_g, wg, wu, wd, post_g)


def _rope_tables(pos):
    pos = pos.astype(F32)[:, None]
    half = RET_DK // 2
    inv = jnp.power(jnp.float32(RET_THETA), -jnp.arange(half, dtype=F32) * (2.0 / RET_DK))
    ang = pos * inv[None, :]
    cos, sin = jnp.cos(ang), jnp.sin(ang)
    cr = jnp.concatenate([cos, cos], axis=1)
    sr = jnp.concatenate([-sin, sin], axis=1)
    half = ROT_DIM // 2
    inv = jnp.power(jnp.float32(ROPE_THETA), -jnp.arange(half, dtype=F32) * (2.0 / ROT_DIM))
    ang = pos * inv[None, :]
    cos, sin = jnp.cos(ang), jnp.sin(ang)
    n = pos.shape[0]
    ones = jnp.ones((n, DIFF_DK - ROT_DIM), F32)
    zeros = jnp.zeros((n, DIFF_DK - ROT_DIM), F32)
    z8 = jnp.zeros((n, half), F32)
    c_comp = jnp.concatenate([cos, cos, ones], axis=1)
    lo_comp = jnp.concatenate([-sin, z8, zeros], axis=1)
    hi_comp = jnp.concatenate([z8, sin, zeros], axis=1)
    cd = jnp.concatenate([c_comp, c_comp], axis=1)
    s_lo = jnp.concatenate([lo_comp, lo_comp], axis=1)
    s_hi = jnp.concatenate([hi_comp, hi_comp], axis=1)
    return cr, sr, cd, s_lo, s_hi


def _mix_in_kernel(x_ref, g_ref, w_ref, cr_ref, sr_ref, cd_ref, slo_ref, shi_ref, p_ref, kd_ref, vd_ref,
                   *, transpose_k):
    h = _rms(x_ref[...], g_ref[...]).astype(BF16)
    for grp in (6, 5, 4, 1, 0, 2, 3):
        pg = jnp.dot(h, w_ref[:, grp * GROUP_W:(grp + 1) * GROUP_W], preferred_element_type=F32)
        if grp == 6:
            vd_ref[...] = pg.reshape(pg.shape[0], HEADS, HEAD_W)
        for t in range(HEADS):
            cols = slice(t * HEAD_W, (t + 1) * HEAD_W)
            xt = pg[:, cols]
            if grp in (0, 1):
                xt = xt * cr_ref[...] + pltpu.roll(xt, RET_DK // 2, 1) * sr_ref[...]
                if grp == 1:
                    xt = xt * (RET_DK ** -0.5)
            elif grp in (4, 5):
                xt = (xt * cd_ref[...] + pltpu.roll(xt, HEAD_W - ROT_DIM // 2, 1) * slo_ref[...]
                      + pltpu.roll(xt, ROT_DIM // 2, 1) * shi_ref[...])
                if grp == 5 and transpose_k:
                    kd_ref[cols, :] = xt.T
                elif grp == 5:
                    kd_ref[:, cols] = xt
                else:
                    xt = xt * (DIFF_DK ** -0.5 * LOG2E)
            p_ref[:, grp * GROUP_W + t * HEAD_W:grp * GROUP_W + (t + 1) * HEAD_W] = xt.astype(BF16)


def _mix_in(x, g, w, seq_len, pos0):
    m = x.shape[0]
    tm = min(ROW_TILE, m)
    n_tab = max(seq_len, tm)
    pos = pos0 + (jnp.arange(n_tab, dtype=jnp.int32) % seq_len)
    tables = _rope_tables(pos)
    n_tab_blocks = n_tab // tm
    row = pl.BlockSpec((tm, D_MODEL), lambda i: (i, 0))
    tab = pl.BlockSpec((tm, HEAD_W), lambda i: (i % n_tab_blocks, 0))
    transpose_k = seq_len % tm == 0
    if transpose_k:
        n_seq_blocks = seq_len // tm
        k_spec = pl.BlockSpec((None, GROUP_W, tm), lambda i: (i // n_seq_blocks, 0, i % n_seq_blocks))
        k_shape = jax.ShapeDtypeStruct((m // seq_len, GROUP_W, seq_len), F32)
    else:
        k_spec = pl.BlockSpec((tm, GROUP_W), lambda i: (i, 0))
        k_shape = jax.ShapeDtypeStruct((m, GROUP_W), F32)
    p, k_new, v_new = pl.pallas_call(
        functools.partial(_mix_in_kernel, transpose_k=transpose_k),
        grid=(m // tm,),
        in_specs=[row, _resident((1, D_MODEL)), _resident((D_MODEL, IN_COLS)), tab, tab, tab, tab, tab],
        out_specs=[pl.BlockSpec((tm, IN_COLS), lambda i: (i, 0)), k_spec,
                   pl.BlockSpec((tm, HEADS, HEAD_W), lambda i: (i, 0, 0))],
        out_shape=[jax.ShapeDtypeStruct((m, IN_COLS), BF16), k_shape,
                   jax.ShapeDtypeStruct((m, HEADS, HEAD_W), F32)],
        compiler_params=_params("parallel"),
        name="mix_in",
    )(x, g, w, *tables)
    b = m // seq_len
    if transpose_k:
        k_new = jnp.transpose(k_new.reshape(b, HEADS, 2, DIFF_DK, seq_len), (0, 4, 1, 2, 3))
    else:
        k_new = k_new.reshape(b, seq_len, HEADS, 2, DIFF_DK)
    return p.reshape(b, seq_len, IN_COLS), k_new, v_new.reshape(b, seq_len, HEADS, HEAD_W)


def _ret_tables(c):
    log_g = jnp.log(1.0 - jnp.power(2.0, -5.0 - jnp.arange(HEADS, dtype=F32)))
    idx = jnp.arange(c, dtype=F32)
    rel = idx[:, None] - idx[None, :]
    dmask = jnp.where(rel >= 0, jnp.exp(log_g[:, None, None] * jnp.maximum(rel, 0.0)), 0.0)
    q_dec = jnp.exp(log_g[:, None] * (idx + 1.0)[None, :])
    k_dec = jnp.exp(log_g[:, None] * (c - 1.0 - idx)[None, :])
    b_dec = jnp.exp(log_g * c)
    bcast = lambda a: jnp.broadcast_to(a[:, :, None], (HEADS, c, HEAD_W))
    return dmask, bcast(q_dec), bcast(k_dec), jnp.broadcast_to(b_dec[:, None, None], (HEADS, 8, HEAD_W))


def _ret_kernel(q_ref, k_ref, v_ref, g_ref, s0_ref, dmask_ref, qdec_ref, kdec_ref, bdec_ref, gn_ref,
                y_ref, snew_ref, s_scr, *, c, n_c):
    t = pl.program_id(1)

    @pl.when(t == 0)
    def _():
        s_scr[...] = s0_ref[...]

    for h in range(HEADS):
        cols = slice(h * HEAD_W, (h + 1) * HEAD_W)
        state = s_scr[h]
        b_dec = bdec_ref[h][0:1, :]
        for ci in range(n_c):
            rows = slice(ci * c, (ci + 1) * c)
            q = q_ref[rows, cols]
            k = k_ref[rows, cols]
            v = v_ref[rows, cols]
            scores = lax.dot_general(q, k, _NT, preferred_element_type=F32) * dmask_ref[h]
            o = (jnp.dot(scores.astype(BF16), v, preferred_element_type=F32)
                 + qdec_ref[h] * jnp.dot(q, state.astype(BF16), preferred_element_type=F32))
            k_scaled = (k.astype(F32) * kdec_ref[h]).astype(BF16)
            state = b_dec * state + lax.dot_general(k_scaled, v, _TN, preferred_element_type=F32)
            mu = jnp.mean(o, axis=-1, keepdims=True)
            d = o - mu
            var = jnp.mean(d * d, axis=-1, keepdims=True)
            normed = d * lax.rsqrt(var + EPS) * gn_ref[:, cols]
            y_ref[rows, cols] = (_silu(g_ref[rows, cols].astype(F32)) * normed).astype(BF16)
        s_scr[h] = state

    @pl.when(t == pl.num_programs(1) - 1)
    def _():
        snew_ref[...] = s_scr[...]


def _retention(p, s0, gn, seq_len):
    b = p.shape[0]
    c = min(RET_CHUNK, seq_len)
    n_c = min(2, seq_len // c)
    t_rows = c * n_c
    tables = _ret_tables(c)
    col = lambda g: pl.BlockSpec((None, t_rows, GROUP_W), lambda bi, ti, g=g: (bi, ti, g))
    state = pl.BlockSpec((None, HEADS, RET_DK, HEAD_W), lambda bi, ti: (bi, 0, 0, 0))
    return pl.pallas_call(
        functools.partial(_ret_kernel, c=c, n_c=n_c),
        grid=(b, seq_len // t_rows),
        in_specs=[col(0), col(1), col(2), col(3), state,
                  _resident((HEADS, c, c)), _resident((HEADS, c, HEAD_W)), _resident((HEADS, c, HEAD_W)),
                  _resident((HEADS, 8, HEAD_W)), _resident((1, GROUP_W))],
        out_specs=[pl.BlockSpec((None, t_rows, GROUP_W), lambda bi, ti: (bi, ti, 0)), state],
        out_shape=[jax.ShapeDtypeStruct((b, seq_len, GROUP_W), BF16),
                   jax.ShapeDtypeStruct((b, HEADS, RET_DK, HEAD_W), F32)],
        scratch_shapes=[pltpu.VMEM((HEADS, RET_DK, HEAD_W), F32)],
        compiler_params=_params("parallel", "arbitrary"),
        name="retention",
    )(p, p, p, p, s0, *tables, gn)


def _lambda(lq1_ref, lk1_ref, lq2_ref, lk2_ref, lam_init):
    return (jnp.exp(jnp.sum(lq1_ref[...] * lk1_ref[...], axis=-1, keepdims=True))
            - jnp.exp(jnp.sum(lq2_ref[...] * lk2_ref[...], axis=-1, keepdims=True)) + lam_init)


def _split_components(q):
    lane = lax.broadcasted_iota(jnp.int32, q.shape, 1)
    zero = jnp.zeros_like(q)
    return jnp.where(lane < DIFF_DK, q, zero), jnp.where(lane >= DIFF_DK, q, zero)


def _flash_update(s, v, mask, m_ref, l_ref, acc_ref):
    if mask is not None:
        s = jnp.where(mask, s, MASK_VALUE)
    pad = -s.shape[1] % HEAD_W
    if pad:
        s = jnp.concatenate([s, jnp.full((s.shape[0], pad), MASK_VALUE, F32)], axis=1)
        v = jnp.concatenate([v, jnp.zeros((pad, v.shape[1]), v.dtype)], axis=0)
    blocks =[s[:, b * HEAD_W:(b + 1) * HEAD_W] for b in range(s.shape[1] // HEAD_W)]
    m_prev = m_ref[...]
    m_new = jnp.maximum(m_prev, jnp.max(functools.reduce(jnp.maximum, blocks), axis=-1, keepdims=True))
    alpha = jnp.exp2(m_prev - m_new)
    probs = [jnp.exp2(blk - m_new) for blk in blocks]
    l_ref[...] = alpha * l_ref[...] + functools.reduce(jnp.add, probs)
    p = jnp.concatenate([x.astype(BF16) for x in probs], axis=1)
    acc_ref[...] = alpha * acc_ref[...] + jnp.dot(p, v, preferred_element_type=F32)
    m_ref[...] = m_new


def _flash_init(m_ref, l_ref, acc_ref):
    m_ref[...] = jnp.full(m_ref.shape, MASK_VALUE, F32)
    l_ref[...] = jnp.zeros(l_ref.shape, F32)
    acc_ref[...] = jnp.zeros(acc_ref.shape, F32)


def _flash_update_t(s_t, v, mask, m_ref, l_ref, acc_ref):
    if mask is not None:
        s_t = jnp.where(mask, s_t, MASK_VALUE)
    m_prev = m_ref[...]
    m_new = jnp.maximum(m_prev, jnp.max(s_t, axis=0, keepdims=True))
    alpha = jnp.exp2(m_prev - m_new)
    p_t = jnp.exp2(s_t - m_new)
    l_ref[...] = alpha * l_ref[...] + jnp.sum(p_t, axis=0, keepdims=True)
    acc_ref[...] = alpha * acc_ref[...] + lax.dot_general(v, p_t.astype(BF16), _TN, preferred_element_type=F32)
    m_ref[...] = m_new


def _attn_prompt_kernel(lq1_ref, lk1_ref, lq2_ref, lk2_ref, gn_ref, q_ref, k_ref, v_ref, o_ref,
                        *scratch, tile, lam_init):
    i = pl.program_id(1)
    n_sub = tile // ATTN_Q_SUB
    chains = [(qs, c) for qs in range(n_sub) for c in range(2)]
    stats = {ch: scratch[3 * n:3 * n + 3] for n, ch in enumerate(chains)}
    slot_a, slot_b = scratch[3 * len(chains):]
    lam = _lambda(lq1_ref, lk1_ref, lq2_ref, lk2_ref, lam_init)
    key_chunk = lax.broadcasted_iota(jnp.int32, (tile, tile), 0) // CHUNK
    query_chunk = lax.broadcasted_iota(jnp.int32, (tile, tile), 1) // CHUNK
    diag_mask = key_chunk <= query_chunk
    for h in range(HEADS):
        cols = slice(h * HEAD_W, (h + 1) * HEAD_W)
        comps = _split_components(q_ref[:, cols])
        for ch in chains:
            _flash_init(*stats[ch])

        def key_rows(j):
            return pl.ds(pl.multiple_of(j * tile, tile), tile)

        def scores_into(slot, j):
            k = k_ref[key_rows(j), cols]
            for n, (qs, c) in enumerate(chains):
                slot[n] = lax.dot_general(k, comps[c][qs * ATTN_Q_SUB:(qs + 1) * ATTN_Q_SUB], _NT,
                                          preferred_element_type=F32)

        def accumulate(slot, j, mask):
            v = v_ref[key_rows(j), cols]
            for n, (qs, c) in enumerate(chains):
                sub_mask = None if mask is None else mask[:, qs * ATTN_Q_SUB:(qs + 1) * ATTN_Q_SUB]
                _flash_update_t(slot[n], v, sub_mask, *stats[(qs, c)])

        scores_into(slot_a, 0)

        def body(jj, carry):
            scores_into(slot_b, 2 * jj + 1)
            accumulate(slot_a, 2 * jj, None)
            scores_into(slot_a, 2 * jj + 2)
            accumulate(slot_b, 2 * jj + 1, None)
            return carry

        lax.fori_loop(0, i // 2, body, 0)

        @pl.when(i % 2 == 0)
        def _():
            accumulate(slot_a, i, diag_mask)

        @pl.when(i % 2 == 1)
        def _():
            scores_into(slot_b, i)
            accumulate(slot_a, i - 1, None)
            accumulate(slot_b, i, diag_mask)

        for qs in range(n_sub):
            (_, l1_ref, acc1_ref), (_, l2_ref, acc2_ref) = stats[(qs, 0)], stats[(qs, 1)]
            o_t = acc1_ref[...] / l1_ref[...] - lam * (acc2_ref[...] / l2_ref[...])
            ms = jnp.mean(o_t * o_t, axis=0, keepdims=True)
            o_t = o_t * lax.rsqrt(ms + EPS) * gn_ref[...] * (1.0 - lam_init)
            o_ref[qs * ATTN_Q_SUB:(qs + 1) * ATTN_Q_SUB, cols] = o_t.T.astype(BF16)


def _attn_prompt(p, lq1, lk1, lq2, lk2, gn, lam_init):
    b, seq_len, _ = p.shape
    tile = ATTN_TILE
    small = lambda n: pl.BlockSpec((1, n), lambda bi, qi: (0, 0))
    kv = lambda g: pl.BlockSpec((None, seq_len, GROUP_W), lambda bi, qi, g=g: (bi, 0, g))
    return pl.pallas_call(
        functools.partial(_attn_prompt_kernel, tile=tile, lam_init=lam_init),
        grid=(b, seq_len // tile),
        in_specs=[small(DIFF_DK)] * 4 + [pl.BlockSpec((HEAD_W, 1), lambda bi, qi: (0, 0)),
                  pl.BlockSpec((None, tile, GROUP_W), lambda bi, qi: (bi, qi, 4)), kv(5), kv(6)],
        out_specs=pl.BlockSpec((None, tile, GROUP_W), lambda bi, qi: (bi, qi, 0)),
        out_shape=jax.ShapeDtypeStruct((b, seq_len, GROUP_W), BF16),
        scratch_shapes=[pltpu.VMEM((1, ATTN_Q_SUB), F32), pltpu.VMEM((1, ATTN_Q_SUB), F32),
                        pltpu.VMEM((HEAD_W, ATTN_Q_SUB), F32)] * (2 * tile // ATTN_Q_SUB)
                       + [pltpu.VMEM((2 * tile // ATTN_Q_SUB, tile, ATTN_Q_SUB), F32)] * 2,
        compiler_params=_params("parallel", "arbitrary"),
        name="attn_prompt",
    )(lq1, lk1, lq2, lk2, gn.reshape(HEAD_W, 1), p, p, p)


def _attn_sample_kernel(lq1_ref, lk1_ref, lq2_ref, lk2_ref, gn_ref, q_ref, kp_ref, vp_ref, kn_ref, vn_ref,
                        o_ref, m_scr, l_scr, acc_scr, *, n_past, lam_init):
    j = pl.program_id(1)

    @pl.when(j == 0)
    def _():
        _flash_init(m_scr, l_scr, acc_scr)

    n_q = q_ref.shape[0]

    def update(scores_of, values):
        for h in range(HEADS):
            cols = slice(h * HEAD_W, (h + 1) * HEAD_W)
            q_both = jnp.concatenate(_split_components(q_ref[:, cols]), axis=0)
            _flash_update(scores_of(q_both, cols), values[:, cols].astype(BF16), None,
                          m_scr.at[h], l_scr.at[h], acc_scr.at[h])

    @pl.when(j < n_past)
    def _():
        update(lambda q, feats: jnp.dot(q, kp_ref[feats, :].astype(BF16), preferred_element_type=F32),
               vp_ref[...].reshape(vp_ref.shape[0], GROUP_W))

    @pl.when(j == n_past)
    def _():
        update(lambda q, feats: lax.dot_general(q, kn_ref[:, feats], _NT, preferred_element_type=F32),
               vn_ref[...])
        lam = _lambda(lq1_ref, lk1_ref, lq2_ref, lk2_ref, lam_init)
        for h in range(HEADS):
            l = jnp.sum(l_scr[h], axis=-1, keepdims=True)
            o_both = acc_scr[h] / l
            o = o_both[:n_q] - lam * o_both[n_q:]
            o_ref[:, h * HEAD_W:(h + 1) * HEAD_W] = (_rms(o, gn_ref[...]) * (1.0 - lam_init)).astype(BF16)


def _attn_sample(p, k_past, v_past, lq1, lk1, lq2, lk2, gn, lam_init):
    b, seq_len, _ = p.shape
    past_len = k_past.shape[1]
    tk = min(PAST_TILE, past_len)
    n_past = past_len // tk
    k_past = jnp.transpose(k_past, (0, 2, 3, 4, 1)).reshape(b, GROUP_W, past_len)
    small = lambda n: pl.BlockSpec((1, n), lambda bi, ji: (0, 0))
    new = lambda g: pl.BlockSpec((None, seq_len, GROUP_W), lambda bi, ji, g=g: (bi, 0, g))
    past_k = pl.BlockSpec((None, GROUP_W, tk), lambda bi, ji: (bi, 0, jnp.minimum(ji, n_past - 1)))
    past_v = pl.BlockSpec((None, tk, HEADS, HEAD_W), lambda bi, ji: (bi, jnp.minimum(ji, n_past - 1), 0, 0))
    return pl.pallas_call(
        functools.partial(_attn_sample_kernel, n_past=n_past, lam_init=lam_init),
        grid=(b, n_past + 1),
        in_specs=[small(DIFF_DK)] * 4 + [small(HEAD_W), new(4), past_k, past_v, new(5), new(6)],
        out_specs=pl.BlockSpec((None, seq_len, GROUP_W), lambda bi, ji: (bi, 0, 0)),
        out_shape=jax.ShapeDtypeStruct((b, seq_len, GROUP_W), BF16),
        scratch_shapes=[pltpu.VMEM((HEADS, 2 * seq_len, HEAD_W), F32)] * 3,
        compiler_params=_params("parallel", "arbitrary"),
        name="attn_sample",
    )(lq1, lk1, lq2, lk2, gn, p, k_past, v_past, p, p)


def _layer(x, ret_s0, k_past, v_past, w, lam_init):
    (f1_pre, f1_wg, f1_wu, f1_wd, f1_post, mix_pre, w_in, ret_g, lq1, lk1, lq2, lk2, diff_g, w_out,
     mix_post, f2_pre, f2_wg, f2_wu, f2_wd, f2_post) = w
    b, seq_len, _ = x.shape
    past_len = 0 if k_past is None else k_past.shape[1]
    x = x.reshape(b * seq_len, D_MODEL)
    x1 = _ffn(x, f1_pre, f1_wg, f1_wu, f1_wd, f1_post)
    p, k_new, v_new = _mix_in(x1, mix_pre, w_in, seq_len, past_len)
    ret_y, s_new = _retention(p, ret_s0, ret_g, seq_len)
    if k_past is None:
        diff_y = _attn_prompt(p, lq1, lk1, lq2, lk2, diff_g, lam_init)
    else:
        diff_y = _attn_sample(p, k_past, v_past, lq1, lk1, lq2, lk2, diff_g, lam_init)
    x3 = _out_ffn(x1, ret_y.reshape(b * seq_len, GROUP_W), diff_y.reshape(b * seq_len, GROUP_W),
                  w_out, mix_post, f2_pre, f2_wg, f2_wu, f2_wd, f2_post)
    return x3.reshape(b, seq_len, D_MODEL), s_new, k_new, v_new


def kernel(x_prompt, x_sample, state_ret, cache_diff_k, cache_diff_v, ffn1_pre_g, ffn1_w_gate, ffn1_w_up, ffn1_w_down, ffn1_post_g, mix_pre_g, w_in, ret_norm_g, diff_lq1, diff_lk1, diff_lq2, diff_lk2, diff_norm_g, w_out, mix_post_g, ffn2_pre_g, ffn2_w_gate, ffn2_w_up, ffn2_w_down, ffn2_post_g):
    depth = w_in.shape[0]
    xp, xs = x_prompt, x_sample
    outs = [[] for _ in range(6)]
    for li in range(depth):
        lam_init = 0.8 - 0.6 * math.exp(-0.3 * li)
        row = lambda a: a[li].reshape(1, -1)
        mat = lambda a: a[li].astype(BF16)
        w = (row(ffn1_pre_g), mat(ffn1_w_gate), mat(ffn1_w_up), mat(ffn1_w_down), row(ffn1_post_g),
             row(mix_pre_g), mat(w_in), row(ret_norm_g), row(diff_lq1), row(diff_lk1), row(diff_lq2),
             row(diff_lk2), row(diff_norm_g), mat(w_out), row(mix_post_g),
             row(ffn2_pre_g), mat(ffn2_w_gate), mat(ffn2_w_up), mat(ffn2_w_down), row(ffn2_post_g))
        s0 = jnp.zeros((xp.shape[0], HEADS, RET_DK, HEAD_W), F32)
        xp, s_p, k_p, v_p = _layer(xp, s0, None, None, w, lam_init)
        xs, s_s, k_s, v_s = _layer(xs, state_ret[li], cache_diff_k[li], cache_diff_v[li], w, lam_init)
        for acc, val in zip(outs, (s_p, k_p, v_p, s_s, k_s, v_s)):
            acc.append(val)
    return (xp, xs) + tuple(jnp.stack(o) for o in outs)
```

```python
import functools
import math

import jax
import jax.numpy as jnp
from jax import lax
from jax.experimental import pallas as pl
from jax.experimental.pallas import tpu as pltpu

D_MODEL = 1024
D_FF = 2816
CHUNK = 64
HEADS = 4
HEAD_W = 128
GROUP_W = HEADS * HEAD_W
N_GROUPS = 7
IN_COLS = N_GROUPS * GROUP_W
RET_DK = 128
DIFF_DK = 64
ROT_DIM = DIFF_DK // 4
ROPE_THETA = 500000.0
RET_THETA = 10000.0
EPS = 1e-6
MASK_VALUE = -1e30
LOG2E = math.log2(math.e)

ROW_TILE = 512
FFN_ROW_GROUPS = 2
RET_CHUNK = 256
ATTN_TILE = 512
ATTN_Q_SUB = 256
PAST_TILE = 1024
VMEM_LIMIT = 56 * 1024 * 1024

F32 = jnp.float32
BF16 = jnp.bfloat16
_NT = (((1,), (1,)), ((), ()))
_TN = (((0,), (0,)), ((), ()))


def _rms(x, g):
    return x * lax.rsqrt(jnp.mean(x * x, axis=-1, keepdims=True) + EPS) * g


def _silu(x):
    return x * (1.0 / (1.0 + jnp.exp(-x)))


def _resident(shape):
    nd = len(shape)
    return pl.BlockSpec(shape, lambda *_: (0,) * nd, pipeline_mode=pl.Buffered(1))


def _params(*sem):
    return pltpu.CompilerParams(dimension_semantics=sem, vmem_limit_bytes=VMEM_LIMIT)


def _ffn_math(x, pre_g, wg_ref, wu_ref, wd_ref, post_g):
    n = FFN_ROW_GROUPS if x.shape[0] % (8 * FFN_ROW_GROUPS) == 0 else 1
    rows = x.shape[0] // n
    xs = [x[g * rows:(g + 1) * rows] for g in range(n)]
    hs = [_rms(xg, pre_g).astype(BF16) for xg in xs]
    gates_ups = [(jnp.dot(h, wg_ref[...], preferred_element_type=F32),
                  jnp.dot(h, wu_ref[...], preferred_element_type=F32)) for h in hs]
    acts = [(_silu(gate) * up).astype(BF16) for gate, up in gates_ups]
    ys = [jnp.dot(act, wd_ref[...], preferred_element_type=F32) for act in acts]
    outs = [xg + 0.5 * _rms(y, post_g) for xg, y in zip(xs, ys)]
    return outs[0] if n == 1 else jnp.concatenate(outs, axis=0)


def _ffn_kernel(x_ref, pre_ref, wg_ref, wu_ref, wd_ref, post_ref, o_ref):
    o_ref[...] = _ffn_math(x_ref[...], pre_ref[...], wg_ref, wu_ref, wd_ref, post_ref[...])


def _ffn(x, pre_g, wg, wu, wd, post_g):
    m = x.shape[0]
    tm = min(ROW_TILE, m)
    row = pl.BlockSpec((tm, D_MODEL), lambda i: (i, 0))
    return pl.pallas_call(
        _ffn_kernel,
        grid=(m // tm,),
        in_specs=[row, _resident((1, D_MODEL)), _resident((D_MODEL, D_FF)), _resident((D_MODEL, D_FF)),
                  _resident((D_FF, D_MODEL)), _resident((1, D_MODEL))],
        out_specs=row,
        out_shape=jax.ShapeDtypeStruct((m, D_MODEL), F32),
        compiler_params=_params("parallel"),
        name="ffn",
    )(x, pre_g, wg, wu, wd, post_g)


def _out_ffn_kernel(x_ref, ry_ref, dy_ref, wo_ref, mixpost_ref, pre_ref, wg_ref, wu_ref, wd_ref, post_ref,
                    o_ref):
    y = (jnp.dot(ry_ref[...], wo_ref[:GROUP_W, :], preferred_element_type=F32)
         + jnp.dot(dy_ref[...], wo_ref[GROUP_W:, :], preferred_element_type=F32))
    x2 = x_ref[...] + _rms(y, mixpost_ref[...])
    o_ref[...] = _ffn_math(x2, pre_ref[...], wg_ref, wu_ref, wd_ref, post_ref[...])


def _out_ffn(x, ret_y, diff_y, wo, mixpost_g, pre_g, wg, wu, wd, post_g):
    m = x.shape[0]
    tm = min(ROW_TILE, m)
    row = pl.BlockSpec((tm, D_MODEL), lambda i: (i, 0))
    half = pl.BlockSpec((tm, GROUP_W), lambda i: (i, 0))
    return pl.pallas_call(
        _out_ffn_kernel,
        grid=(m // tm,),
        in_specs=[row, half, half, _resident((D_MODEL, D_MODEL)), _resident((1, D_MODEL)),
                  _resident((1, D_MODEL)), _resident((D_MODEL, D_FF)), _resident((D_MODEL, D_FF)),
                  _resident((D_FF, D_MODEL)), _resident((1, D_MODEL))],
        out_specs=row,
        out_shape=jax.ShapeDtypeStruct((m, D_MODEL), F32),
        compiler_params=_params("parallel"),
        name="out_ffn",
    )(x, ret_y, diff_y, wo, mixpost_g, pre_g, wg, wu, wd, post_g)


def _rope_tables(pos):
    pos = pos.astype(F32)[:, None]
    half = RET_DK // 2
    inv = jnp.power(jnp.float32(RET_THETA), -jnp.arange(half, dtype=F32) * (2.0 / RET_DK))
    ang = pos * inv[None, :]
    cos, sin = jnp.cos(ang), jnp.sin(ang)
    cr = jnp.concatenate([cos, cos], axis=1)
    sr = jnp.concatenate([-sin, sin], axis=1)
    half = ROT_DIM // 2
    inv = jnp.power(jnp.float32(ROPE_THETA), -jnp.arange(half, dtype=F32) * (2.0 / ROT_DIM))
    ang = pos * inv[None, :]
    cos, sin = jnp.cos(ang), jnp.sin(ang)
    n = pos.shape[0]
    ones = jnp.ones((n, DIFF_DK - ROT_DIM), F32)
    zeros = jnp.zeros((n, DIFF_DK - ROT_DIM), F32)
    z8 = jnp.zeros((n, half), F32)
    c_comp = jnp.concatenate([cos, cos, ones], axis=1)
    lo_comp = jnp.concatenate([-sin, z8, zeros], axis=1)
    hi_comp = jnp.concatenate([z8, sin, zeros], axis=1)
    cd = jnp.concatenate([c_comp, c_comp], axis=1)
    s_lo = jnp.concatenate([lo_comp, lo_comp], axis=1)
    s_hi = jnp.concatenate([hi_comp, hi_comp], axis=1)
    return cr, sr, cd, s_lo, s_hi


def _mix_in_kernel(x_ref, g_ref, w_ref, cr_ref, sr_ref, cd_ref, slo_ref, shi_ref, p_ref, kd_ref, vd_ref,
                   *, transpose_k):
    h = _rms(x_ref[...], g_ref[...]).astype(BF16)
    for grp in (6, 5, 4, 1, 0, 2, 3):
        pg = jnp.dot(h, w_ref[:, grp * GROUP_W:(grp + 1) * GROUP_W], preferred_element_type=F32)
        if grp == 6:
            vd_ref[...] = pg.reshape(pg.shape[0], HEADS, HEAD_W)
        for t in range(HEADS):
            cols = slice(t * HEAD_W, (t + 1) * HEAD_W)
            xt = pg[:, cols]
            if grp in (0, 1):
                xt = xt * cr_ref[...] + pltpu.roll(xt, RET_DK // 2, 1) * sr_ref[...]
                if grp == 1:
                    xt = xt * (RET_DK ** -0.5)
            elif grp in (4, 5):
                xt = (xt * cd_ref[...] + pltpu.roll(xt, HEAD_W - ROT_DIM // 2, 1) * slo_ref[...]
                      + pltpu.roll(xt, ROT_DIM // 2, 1) * shi_ref[...])
                if grp == 5 and transpose_k:
                    kd_ref[cols, :] = xt.T
                elif grp == 5:
                    kd_ref[:, cols] = xt
                else:
                    xt = xt * (DIFF_DK ** -0.5 * LOG2E)
            p_ref[:, grp * GROUP_W + t * HEAD_W:grp * GROUP_W + (t + 1) * HEAD_W] = xt.astype(BF16)


def _mix_in(x, g, w, seq_len, pos0):
    m = x.shape[0]
    tm = min(ROW_TILE, m)
    n_tab = max(seq_len, tm)
    pos = pos0 + (jnp.arange(n_tab, dtype=jnp.int32) % seq_len)
    tables = _rope_tables(pos)
    n_tab_blocks = n_tab // tm
    row = pl.BlockSpec((tm, D_MODEL), lambda i: (i, 0))
    tab = pl.BlockSpec((tm, HEAD_W), lambda i: (i % n_tab_blocks, 0))
    transpose_k = seq_len % tm == 0
    if transpose_k:
        n_seq_blocks = seq_len // tm
        k_spec = pl.BlockSpec((None, GROUP_W, tm), lambda i: (i // n_seq_blocks, 0, i % n_seq_blocks))
        k_shape = jax.ShapeDtypeStruct((m // seq_len, GROUP_W, seq_len), F32)
    else:
        k_spec = pl.BlockSpec((tm, GROUP_W), lambda i: (i, 0))
        k_shape = jax.ShapeDtypeStruct((m, GROUP_W), F32)
    p, k_new, v_new = pl.pallas_call(
        functools.partial(_mix_in_kernel, transpose_k=transpose_k),
        grid=(m // tm,),
        in_specs=[row, _resident((1, D_MODEL)), _resident((D_MODEL, IN_COLS)), tab, tab, tab, tab, tab],
        out_specs=[pl.BlockSpec((tm, IN_COLS), lambda i: (i, 0)), k_spec,
                   pl.BlockSpec((tm, HEADS, HEAD_W), lambda i: (i, 0, 0))],
        out_shape=[jax.ShapeDtypeStruct((m, IN_COLS), BF16), k_shape,
                   jax.ShapeDtypeStruct((m, HEADS, HEAD_W), F32)],
        compiler_params=_params("parallel"),
        name="mix_in",
    )(x, g, w, *tables)
    b = m // seq_len
    if transpose_k:
        k_new = jnp.transpose(k_new.reshape(b, HEADS, 2, DIFF_DK, seq_len), (0, 4, 1, 2, 3))
    else:
        k_new = k_new.reshape(b, seq_len, HEADS, 2, DIFF_DK)
    return p.reshape(b, seq_len, IN_COLS), k_new, v_new.reshape(b, seq_len, HEADS, HEAD_W)


def _ret_tables(c):
    log_g = jnp.log(1.0 - jnp.power(2.0, -5.0 - jnp.arange(HEADS, dtype=F32)))
    idx = jnp.arange(c, dtype=F32)
    rel = idx[:, None] - idx[None, :]
    dmask = jnp.where(rel >= 0, jnp.exp(log_g[:, None, None] * jnp.maximum(rel, 0.0)), 0.0)
    q_dec = jnp.exp(log_g[:, None] * (idx + 1.0)[None, :])
    k_dec = jnp.exp(log_g[:, None] * (c - 1.0 - idx)[None, :])
    b_dec = jnp.exp(log_g * c)
    bcast = lambda a: jnp.broadcast_to(a[:, :, None], (HEADS, c, HEAD_W))
    return dmask, bcast(q_dec), bcast(k_dec), jnp.broadcast_to(b_dec[:, None, None], (HEADS, 8, HEAD_W))


def _ret_kernel(q_ref, k_ref, v_ref, g_ref, s0_ref, dmask_ref, qdec_ref, kdec_ref, bdec_ref, gn_ref,
                y_ref, snew_ref, s_scr, *, c, n_c):
    t = pl.program_id(1)

    @pl.when(t == 0)
    def _():
        s_scr[...] = s0_ref[...]

    for h in range(HEADS):
        cols = slice(h * HEAD_W, (h + 1) * HEAD_W)
        state = s_scr[h]
        b_dec = bdec_ref[h][0:1, :]
        for ci in range(n_c):
            rows = slice(ci * c, (ci + 1) * c)
            q = q_ref[rows, cols]
            k = k_ref[rows, cols]
            v = v_ref[rows, cols]
            scores = lax.dot_general(q, k, _NT, preferred_element_type=F32) * dmask_ref[h]
            o = (jnp.dot(scores.astype(BF16), v, preferred_element_type=F32)
                 + qdec_ref[h] * jnp.dot(q, state.astype(BF16), preferred_element_type=F32))
            k_scaled = (k.astype(F32) * kdec_ref[h]).astype(BF16)
            state = b_dec * state + lax.dot_general(k_scaled, v, _TN, preferred_element_type=F32)
            mu = jnp.mean(o, axis=-1, keepdims=True)
            d = o - mu
            var = jnp.mean(d * d, axis=-1, keepdims=True)
            normed = d * lax.rsqrt(var + EPS) * gn_ref[:, cols]
            y_ref[rows, cols] = (_silu(g_ref[rows, cols].astype(F32)) * normed).astype(BF16)
        s_scr[h] = state

    @pl.when(t == pl.num_programs(1) - 1)
    def _():
        snew_ref[...] = s_scr[...]


def _retention(p, s0, gn, seq_len):
    b = p.shape[0]
    c = min(RET_CHUNK, seq_len)
    n_c = min(2, seq_len // c)
    t_rows = c * n_c
    tables = _ret_tables(c)
    col = lambda g: pl.BlockSpec((None, t_rows, GROUP_W), lambda bi, ti, g=g: (bi, ti, g))
    state = pl.BlockSpec((None, HEADS, RET_DK, HEAD_W), lambda bi, ti: (bi, 0, 0, 0))
    return pl.pallas_call(
        functools.partial(_ret_kernel, c=c, n_c=n_c),
        grid=(b, seq_len // t_rows),
        in_specs=[col(0), col(1), col(2), col(3), state,
                  _resident((HEADS, c, c)), _resident((HEADS, c, HEAD_W)), _resident((HEADS, c, HEAD_W)),
                  _resident((HEADS, 8, HEAD_W)), _resident((1, GROUP_W))],
        out_specs=[pl.BlockSpec((None, t_rows, GROUP_W), lambda bi, ti: (bi, ti, 0)), state],
        out_shape=[jax.ShapeDtypeStruct((b, seq_len, GROUP_W), BF16),
                   jax.ShapeDtypeStruct((b, HEADS, RET_DK, HEAD_W), F32)],
        scratch_shapes=[pltpu.VMEM((HEADS, RET_DK, HEAD_W), F32)],
        compiler_params=_params("parallel", "arbitrary"),
        name="retention",
    )(p, p, p, p, s0, *tables, gn)


def _lambda(lq1_ref, lk1_ref, lq2_ref, lk2_ref, lam_init):
    return (jnp.exp(jnp.sum(lq1_ref[...] * lk1_ref[...], axis=-1, keepdims=True))
            - jnp.exp(jnp.sum(lq2_ref[...] * lk2_ref[...], axis=-1, keepdims=True)) + lam_init)


def _split_components(q):
    lane = lax.broadcasted_iota(jnp.int32, q.shape, 1)
    zero = jnp.zeros_like(q)
    return jnp.where(lane < DIFF_DK, q, zero), jnp.where(lane >= DIFF_DK, q, zero)


def _flash_update(s, v, mask, m_ref, l_ref, acc_ref):
    if mask is not None:
        s = jnp.where(mask, s, MASK_VALUE)
    pad = -s.shape[1] % HEAD_W
    if pad:
        s = jnp.concatenate([s, jnp.full((s.shape[0], pad), MASK_VALUE, F32)], axis=1)
        v = jnp.concatenate([v, jnp.zeros((pad, v.shape[1]), v.dtype)], axis=0)
    blocks =[s[:, b * HEAD_W:(b + 1) * HEAD_W] for b in range(s.shape[1] // HEAD_W)]
    m_prev = m_ref[...]
    m_new = jnp.maximum(m_prev, jnp.max(functools.reduce(jnp.maximum, blocks), axis=-1, keepdims=True))
    alpha = jnp.exp2(m_prev - m_new)
    probs = [jnp.exp2(blk - m_new) for blk in blocks]
    l_ref[...] = alpha * l_ref[...] + functools.reduce(jnp.add, probs)
    p = jnp.concatenate([x.astype(BF16) for x in probs], axis=1)
    acc_ref[...] = alpha * acc_ref[...] + jnp.dot(p, v, preferred_element_type=F32)
    m_ref[...] = m_new


def _flash_init(m_ref, l_ref, acc_ref):
    m_ref[...] = jnp.full(m_ref.shape, MASK_VALUE, F32)
    l_ref[...] = jnp.zeros(l_ref.shape, F32)
    acc_ref[...] = jnp.zeros(acc_ref.shape, F32)


def _flash_update_t(s_t, v, mask, m_ref, l_ref, acc_ref):
    if mask is not None:
        s_t = jnp.where(mask, s_t, MASK_VALUE)
    m_prev = m_ref[...]
    m_new = jnp.maximum(m_prev, jnp.max(s_t, axis=0, keepdims=True))
    alpha = jnp.exp2(m_prev - m_new)
    p_t = jnp.exp2(s_t - m_new)
    l_ref[...] = alpha * l_ref[...] + jnp.sum(p_t, axis=0, keepdims=True)
    acc_ref[...] = alpha * acc_ref[...] + lax.dot_general(v, p_t.astype(BF16), _TN, preferred_element_type=F32)
    m_ref[...] = m_new


def _attn_prompt_kernel(lq1_ref, lk1_ref, lq2_ref, lk2_ref, gn_ref, q_ref, k_ref, v_ref, o_ref,
                        *scratch, tile, lam_init):
    i = pl.program_id(1)
    n_sub = tile // ATTN_Q_SUB
    chains = [(qs, c) for qs in range(n_sub) for c in range(2)]
    stats = {ch: scratch[3 * n:3 * n + 3] for n, ch in enumerate(chains)}
    slot_a, slot_b = scratch[3 * len(chains):]
    lam = _lambda(lq1_ref, lk1_ref, lq2_ref, lk2_ref, lam_init)
    key_chunk = lax.broadcasted_iota(jnp.int32, (tile, tile), 0) // CHUNK
    query_chunk = lax.broadcasted_iota(jnp.int32, (tile, tile), 1) // CHUNK
    diag_mask = key_chunk <= query_chunk
    for h in range(HEADS):
        cols = slice(h * HEAD_W, (h + 1) * HEAD_W)
        comps = [comp.T for comp in _split_components(q_ref[:, cols])]
        for ch in chains:
            _flash_init(*stats[ch])

        def key_rows(j):
            return pl.ds(pl.multiple_of(j * tile, tile), tile)

        def scores_into(slot, j):
            k = k_ref[key_rows(j), cols]
            for n, (qs, c) in enumerate(chains):
                slot[n] = jnp.dot(k, comps[c][:, qs * ATTN_Q_SUB:(qs + 1) * ATTN_Q_SUB],
                                  preferred_element_type=F32)

        def accumulate(slot, j, mask):
            v = v_ref[key_rows(j), cols]
            for n, (qs, c) in enumerate(chains):
                sub_mask = None if mask is None else mask[:, qs * ATTN_Q_SUB:(qs + 1) * ATTN_Q_SUB]
                _flash_update_t(slot[n], v, sub_mask, *stats[(qs, c)])

        scores_into(slot_a, 0)

        def body(jj, carry):
            scores_into(slot_b, 2 * jj + 1)
            accumulate(slot_a, 2 * jj, None)
            scores_into(slot_a, 2 * jj + 2)
            accumulate(slot_b, 2 * jj + 1, None)
            return carry

        lax.fori_loop(0, i // 2, body, 0)

        @pl.when(i % 2 == 0)
        def _():
            accumulate(slot_a, i, diag_mask)

        @pl.when(i % 2 == 1)
        def _():
            scores_into(slot_b, i)
            accumulate(slot_a, i - 1, None)
            accumulate(slot_b, i, diag_mask)

        for qs in range(n_sub):
            (_, l1_ref, acc1_ref), (_, l2_ref, acc2_ref) = stats[(qs, 0)], stats[(qs, 1)]
            o_t = acc1_ref[...] / l1_ref[...] - lam * (acc2_ref[...] / l2_ref[...])
            ms = jnp.mean(o_t * o_t, axis=0, keepdims=True)
            o_t = o_t * lax.rsqrt(ms + EPS) * gn_ref[...] * (1.0 - lam_init)
            o_ref[qs * ATTN_Q_SUB:(qs + 1) * ATTN_Q_SUB, cols] = o_t.T.astype(BF16)


def _attn_prompt(p, lq1, lk1, lq2, lk2, gn, lam_init):
    b, seq_len, _ = p.shape
    tile = ATTN_TILE
    small = lambda n: pl.BlockSpec((1, n), lambda bi, qi: (0, 0))
    kv = lambda g: pl.BlockSpec((None, seq_len, GROUP_W), lambda bi, qi, g=g: (bi, 0, g))
    return pl.pallas_call(
        functools.partial(_attn_prompt_kernel, tile=tile, lam_init=lam_init),
        grid=(b, seq_len // tile),
        in_specs=[small(DIFF_DK)] * 4 + [pl.BlockSpec((HEAD_W, 1), lambda bi, qi: (0, 0)),
                  pl.BlockSpec((None, tile, GROUP_W), lambda bi, qi: (bi, qi, 4)), kv(5), kv(6)],
        out_specs=pl.BlockSpec((None, tile, GROUP_W), lambda bi, qi: (bi, qi, 0)),
        out_shape=jax.ShapeDtypeStruct((b, seq_len, GROUP_W), BF16),
        scratch_shapes=[pltpu.VMEM((1, ATTN_Q_SUB), F32), pltpu.VMEM((1, ATTN_Q_SUB), F32),
                        pltpu.VMEM((HEAD_W, ATTN_Q_SUB), F32)] * (2 * tile // ATTN_Q_SUB)
                       + [pltpu.VMEM((2 * tile // ATTN_Q_SUB, tile, ATTN_Q_SUB), F32)] * 2,
        compiler_params=_params("parallel", "arbitrary"),
        name="attn_prompt",
    )(lq1, lk1, lq2, lk2, gn.reshape(HEAD_W, 1), p, p, p)


def _attn_sample_kernel(lq1_ref, lk1_ref, lq2_ref, lk2_ref, gn_ref, q_ref, kp_ref, vp_ref, kn_ref, vn_ref,
                        o_ref, m_scr, l_scr, acc_scr, *, n_past, lam_init):
    j = pl.program_id(1)

    @pl.when(j == 0)
    def _():
        _flash_init(m_scr, l_scr, acc_scr)

    n_q = q_ref.shape[0]

    def update(scores_of, values):
        for h in range(HEADS):
            cols = slice(h * HEAD_W, (h + 1) * HEAD_W)
            q_both = jnp.concatenate(_split_components(q_ref[:, cols]), axis=0)
            _flash_update(scores_of(q_both, cols), values[:, cols].astype(BF16), None,
                          m_scr.at[h], l_scr.at[h], acc_scr.at[h])

    @pl.when(j < n_past)
    def _():
        update(lambda q, feats: jnp.dot(q, kp_ref[feats, :].astype(BF16), preferred_element_type=F32),
               vp_ref[...].reshape(vp_ref.shape[0], GROUP_W))

    @pl.when(j == n_past)
    def _():
        update(lambda q, feats: lax.dot_general(q, kn_ref[:, feats], _NT, preferred_element_type=F32),
               vn_ref[...])
        lam = _lambda(lq1_ref, lk1_ref, lq2_ref, lk2_ref, lam_init)
        for h in range(HEADS):
            l = jnp.sum(l_scr[h], axis=-1, keepdims=True)
            o_both = acc_scr[h] / l
            o = o_both[:n_q] - lam * o_both[n_q:]
            o_ref[:, h * HEAD_W:(h + 1) * HEAD_W] = (_rms(o, gn_ref[...]) * (1.0 - lam_init)).astype(BF16)


def _attn_sample(p, k_past, v_past, lq1, lk1, lq2, lk2, gn, lam_init):
    b, seq_len, _ = p.shape
    past_len = k_past.shape[1]
    tk = min(PAST_TILE, past_len)
    n_past = past_len // tk
    k_past = jnp.transpose(k_past, (0, 2, 3, 4, 1)).reshape(b, GROUP_W, past_len)
    small = lambda n: pl.BlockSpec((1, n), lambda bi, ji: (0, 0))
    new = lambda g: pl.BlockSpec((None, seq_len, GROUP_W), lambda bi, ji, g=g: (bi, 0, g))
    past_k = pl.BlockSpec((None, GROUP_W, tk), lambda bi, ji: (bi, 0, jnp.minimum(ji, n_past - 1)))
    past_v = pl.BlockSpec((None, tk, HEADS, HEAD_W), lambda bi, ji: (bi, jnp.minimum(ji, n_past - 1), 0, 0))
    return pl.pallas_call(
        functools.partial(_attn_sample_kernel, n_past=n_past, lam_init=lam_init),
        grid=(b, n_past + 1),
        in_specs=[small(DIFF_DK)] * 4 + [small(HEAD_W), new(4), past_k, past_v, new(5), new(6)],
        out_specs=pl.BlockSpec((None, seq_len, GROUP_W), lambda bi, ji: (bi, 0, 0)),
        out_shape=jax.ShapeDtypeStruct((b, seq_len, GROUP_W), BF16),
        scratch_shapes=[pltpu.VMEM((HEADS, 2 * seq_len, HEAD_W), F32)] * 3,
        compiler_params=_params("parallel", "arbitrary"),
        name="attn_sample",
    )(lq1, lk1, lq2, lk2, gn, p, k_past, v_past, p, p)


def _layer(x, ret_s0, k_past, v_past, w, lam_init):
    (f1_pre, f1_wg, f1_wu, f1_wd, f1_post, mix_pre, w_in, ret_g, lq1, lk1, lq2, lk2, diff_g, w_out,
     mix_post, f2_pre, f2_wg, f2_wu, f2_wd, f2_post) = w
    b, seq_len, _ = x.shape
    past_len = 0 if k_past is None else k_past.shape[1]
    x = x.reshape(b * seq_len, D_MODEL)
    x1 = _ffn(x, f1_pre, f1_wg, f1_wu, f1_wd, f1_post)
    p, k_new, v_new = _mix_in(x1, mix_pre, w_in, seq_len, past_len)
    ret_y, s_new = _retention(p, ret_s0, ret_g, seq_len)
    if k_past is None:
        diff_y = _attn_prompt(p, lq1, lk1, lq2, lk2, diff_g, lam_init)
    else:
        diff_y = _attn_sample(p, k_past, v_past, lq1, lk1, lq2, lk2, diff_g, lam_init)
    x3 = _out_ffn(x1, ret_y.reshape(b * seq_len, GROUP_W), diff_y.reshape(b * seq_len, GROUP_W),
                  w_out, mix_post, f2_pre, f2_wg, f2_wu, f2_wd, f2_post)
    return x3.reshape(b, seq_len, D_MODEL), s_new, k_new, v_new


def kernel(x_prompt, x_sample, state_ret, cache_diff_k, cache_diff_v, ffn1_pre_g, ffn1_w_gate, ffn1_w_up, ffn1_w_down, ffn1_post_g, mix_pre_g, w_in, ret_norm_g, diff_lq1, diff_lk1, diff_lq2, diff_lk2, diff_norm_g, w_out, mix_post_g, ffn2_pre_g, ffn2_w_gate, ffn2_w_up, ffn2_w_down, ffn2_post_g):
    depth = w_in.shape[0]
    xp, xs = x_prompt, x_sample
    outs = [[] for _ in range(6)]
    for li in range(depth):
        lam_init = 0.8 - 0.6 * math.exp(-0.3 * li)
        row = lambda a: a[li].reshape(1, -1)
        mat = lambda a: a[li].astype(BF16)
        w = (row(ffn1_pre_g), mat(ffn1_w_gate), mat(ffn1_w_up), mat(ffn1_w_down), row(ffn1_post_g),
             row(mix_pre_g), mat(w_in), row(ret_norm_g), row(diff_lq1), row(diff_lk1), row(diff_lq2),
             row(diff_lk2), row(diff_norm_g), mat(w_out), row(mix_post_g),
             row(ffn2_pre_g), mat(ffn2_w_gate), mat(ffn2_w_up), mat(ffn2_w_down), row(ffn2_post_g))
        s0 = jnp.zeros((xp.shape[0], HEADS, RET_DK, HEAD_W), F32)
        xp, s_p, k_p, v_p = _layer(xp, s0, None, None, w, lam_init)
        xs, s_s, k_s, v_s = _layer(xs, state_ret[li], cache_diff_k[li], cache_diff_v[li], w, lam_init)
        for acc, val in zip(outs, (s_p, k_p, v_p, s_s, k_s, v_s)):
            acc.append(val)
    return (xp, xs) + tuple(jnp.stack(o) for o in outs)
```

```python
import functools
import math

import jax
import jax.numpy as jnp
from jax import lax
from jax.experimental import pallas as pl
from jax.experimental.pallas import tpu as pltpu

D_MODEL = 1024
D_FF = 2816
CHUNK = 64
HEADS = 4
HEAD_W = 128
GROUP_W = HEADS * HEAD_W
N_GROUPS = 7
IN_COLS = N_GROUPS * GROUP_W
RET_DK = 128
DIFF_DK = 64
ROT_DIM = DIFF_DK // 4
ROPE_THETA = 500000.0
RET_THETA = 10000.0
EPS = 1e-6
MASK_VALUE = -1e30
LOG2E = math.log2(math.e)

ROW_TILE = 512
FFN_ROW_GROUPS = 2
RET_CHUNK = 256
ATTN_TILE = 512
SUM_ROWS = 16
ATTN_Q_SUB = 256
PAST_TILE = 1024
VMEM_LIMIT = 56 * 1024 * 1024

F32 = jnp.float32
BF16 = jnp.bfloat16
_NT = (((1,), (1,)), ((), ()))
_TN = (((0,), (0,)), ((), ()))


def _rms(x, g):
    return x * lax.rsqrt(jnp.mean(x * x, axis=-1, keepdims=True) + EPS) * g


def _silu(x):
    return x * (1.0 / (1.0 + jnp.exp(-x)))


def _resident(shape):
    nd = len(shape)
    return pl.BlockSpec(shape, lambda *_: (0,) * nd, pipeline_mode=pl.Buffered(1))


def _params(*sem):
    return pltpu.CompilerParams(dimension_semantics=sem, vmem_limit_bytes=VMEM_LIMIT)


def _ffn_math(x, pre_g, wg_ref, wu_ref, wd_ref, post_g):
    n = FFN_ROW_GROUPS if x.shape[0] % (8 * FFN_ROW_GROUPS) == 0 else 1
    rows = x.shape[0] // n
    xs = [x[g * rows:(g + 1) * rows] for g in range(n)]
    hs = [_rms(xg, pre_g).astype(BF16) for xg in xs]
    gates_ups = [(jnp.dot(h, wg_ref[...], preferred_element_type=F32),
                  jnp.dot(h, wu_ref[...], preferred_element_type=F32)) for h in hs]
    acts = [(_silu(gate) * up).astype(BF16) for gate, up in gates_ups]
    ys = [jnp.dot(act, wd_ref[...], preferred_element_type=F32) for act in acts]
    outs = [xg + 0.5 * _rms(y, post_g) for xg, y in zip(xs, ys)]
    return outs[0] if n == 1 else jnp.concatenate(outs, axis=0)


def _ffn_kernel(x_ref, pre_ref, wg_ref, wu_ref, wd_ref, post_ref, o_ref):
    o_ref[...] = _ffn_math(x_ref[...], pre_ref[...], wg_ref, wu_ref, wd_ref, post_ref[...])


def _ffn(x, pre_g, wg, wu, wd, post_g):
    m = x.shape[0]
    tm = min(ROW_TILE, m)
    row = pl.BlockSpec((tm, D_MODEL), lambda i: (i, 0))
    return pl.pallas_call(
        _ffn_kernel,
        grid=(m // tm,),
        in_specs=[row, _resident((1, D_MODEL)), _resident((D_MODEL, D_FF)), _resident((D_MODEL, D_FF)),
                  _resident((D_FF, D_MODEL)), _resident((1, D_MODEL))],
        out_specs=row,
        out_shape=jax.ShapeDtypeStruct((m, D_MODEL), F32),
        compiler_params=_params("parallel"),
        name="ffn",
    )(x, pre_g, wg, wu, wd, post_g)


def _out_ffn_kernel(x_ref, ry_ref, dy_ref, wo_ref, mixpost_ref, pre_ref, wg_ref, wu_ref, wd_ref, post_ref,
                    o_ref):
    y = (jnp.dot(ry_ref[...], wo_ref[:GROUP_W, :], preferred_element_type=F32)
         + jnp.dot(dy_ref[...], wo_ref[GROUP_W:, :], preferred_element_type=F32))
    x2 = x_ref[...] + _rms(y, mixpost_ref[...])
    o_ref[...] = _ffn_math(x2, pre_ref[...], wg_ref, wu_ref, wd_ref, post_ref[...])


def _out_ffn(x, ret_y, diff_y, wo, mixpost_g, pre_g, wg, wu, wd, post_g):
    m = x.shape[0]
    tm = min(ROW_TILE, m)
    row = pl.BlockSpec((tm, D_MODEL), lambda i: (i, 0))
    half = pl.BlockSpec((tm, GROUP_W), lambda i: (i, 0))
    return pl.pallas_call(
        _out_ffn_kernel,
        grid=(m // tm,),
        in_specs=[row, half, half, _resident((D_MODEL, D_MODEL)), _resident((1, D_MODEL)),
                  _resident((1, D_MODEL)), _resident((D_MODEL, D_FF)), _resident((D_MODEL, D_FF)),
                  _resident((D_FF, D_MODEL)), _resident((1, D_MODEL))],
        out_specs=row,
        out_shape=jax.ShapeDtypeStruct((m, D_MODEL), F32),
        compiler_params=_params("parallel"),
        name="out_ffn",
    )(x, ret_y, diff_y, wo, mixpost_g, pre_g, wg, wu, wd, post_g)


def _rope_tables(pos):
    pos = pos.astype(F32)[:, None]
    half = RET_DK // 2
    inv = jnp.power(jnp.float32(RET_THETA), -jnp.arange(half, dtype=F32) * (2.0 / RET_DK))
    ang = pos * inv[None, :]
    cos, sin = jnp.cos(ang), jnp.sin(ang)
    cr = jnp.concatenate([cos, cos], axis=1)
    sr = jnp.concatenate([-sin, sin], axis=1)
    half = ROT_DIM // 2
    inv = jnp.power(jnp.float32(ROPE_THETA), -jnp.arange(half, dtype=F32) * (2.0 / ROT_DIM))
    ang = pos * inv[None, :]
    cos, sin = jnp.cos(ang), jnp.sin(ang)
    n = pos.shape[0]
    ones = jnp.ones((n, DIFF_DK - ROT_DIM), F32)
    zeros = jnp.zeros((n, DIFF_DK - ROT_DIM), F32)
    z8 = jnp.zeros((n, half), F32)
    c_comp = jnp.concatenate([cos, cos, ones], axis=1)
    lo_comp = jnp.concatenate([-sin, z8, zeros], axis=1)
    hi_comp = jnp.concatenate([z8, sin, zeros], axis=1)
    cd = jnp.concatenate([c_comp, c_comp], axis=1)
    s_lo = jnp.concatenate([lo_comp, lo_comp], axis=1)
    s_hi = jnp.concatenate([hi_comp, hi_comp], axis=1)
    return cr, sr, cd, s_lo, s_hi


def _mix_in_kernel(x_ref, g_ref, w_ref, cr_ref, sr_ref, cd_ref, slo_ref, shi_ref, p_ref, kd_ref, vd_ref,
                   *, transpose_k):
    h = _rms(x_ref[...], g_ref[...]).astype(BF16)
    for grp in (6, 5, 4, 1, 0, 2, 3):
        pg = jnp.dot(h, w_ref[:, grp * GROUP_W:(grp + 1) * GROUP_W], preferred_element_type=F32)
        if grp == 6:
            vd_ref[...] = pg.reshape(pg.shape[0], HEADS, HEAD_W)
        for t in range(HEADS):
            cols = slice(t * HEAD_W, (t + 1) * HEAD_W)
            xt = pg[:, cols]
            if grp in (0, 1):
                xt = xt * cr_ref[...] + pltpu.roll(xt, RET_DK // 2, 1) * sr_ref[...]
                if grp == 1:
                    xt = xt * (RET_DK ** -0.5)
            elif grp in (4, 5):
                xt = (xt * cd_ref[...] + pltpu.roll(xt, HEAD_W - ROT_DIM // 2, 1) * slo_ref[...]
                      + pltpu.roll(xt, ROT_DIM // 2, 1) * shi_ref[...])
                if grp == 5 and transpose_k:
                    kd_ref[cols, :] = xt.T
                elif grp == 5:
                    kd_ref[:, cols] = xt
                else:
                    xt = xt * (DIFF_DK ** -0.5 * LOG2E)
            p_ref[:, grp * GROUP_W + t * HEAD_W:grp * GROUP_W + (t + 1) * HEAD_W] = xt.astype(BF16)


def _mix_in(x, g, w, seq_len, pos0):
    m = x.shape[0]
    tm = min(ROW_TILE, m)
    n_tab = max(seq_len, tm)
    pos = pos0 + (jnp.arange(n_tab, dtype=jnp.int32) % seq_len)
    tables = _rope_tables(pos)
    n_tab_blocks = n_tab // tm
    row = pl.BlockSpec((tm, D_MODEL), lambda i: (i, 0))
    tab = pl.BlockSpec((tm, HEAD_W), lambda i: (i % n_tab_blocks, 0))
    transpose_k = seq_len % tm == 0
    if transpose_k:
        n_seq_blocks = seq_len // tm
        k_spec = pl.BlockSpec((None, GROUP_W, tm), lambda i: (i // n_seq_blocks, 0, i % n_seq_blocks))
        k_shape = jax.ShapeDtypeStruct((m // seq_len, GROUP_W, seq_len), F32)
    else:
        k_spec = pl.BlockSpec((tm, GROUP_W), lambda i: (i, 0))
        k_shape = jax.ShapeDtypeStruct((m, GROUP_W), F32)
    p, k_new, v_new = pl.pallas_call(
        functools.partial(_mix_in_kernel, transpose_k=transpose_k),
        grid=(m // tm,),
        in_specs=[row, _resident((1, D_MODEL)), _resident((D_MODEL, IN_COLS)), tab, tab, tab, tab, tab],
        out_specs=[pl.BlockSpec((tm, IN_COLS), lambda i: (i, 0)), k_spec,
                   pl.BlockSpec((tm, HEADS, HEAD_W), lambda i: (i, 0, 0))],
        out_shape=[jax.ShapeDtypeStruct((m, IN_COLS), BF16), k_shape,
                   jax.ShapeDtypeStruct((m, HEADS, HEAD_W), F32)],
        compiler_params=_params("parallel"),
        name="mix_in",
    )(x, g, w, *tables)
    b = m // seq_len
    if transpose_k:
        k_new = jnp.transpose(k_new.reshape(b, HEADS, 2, DIFF_DK, seq_len), (0, 4, 1, 2, 3))
    else:
        k_new = k_new.reshape(b, seq_len, HEADS, 2, DIFF_DK)
    return p.reshape(b, seq_len, IN_COLS), k_new, v_new.reshape(b, seq_len, HEADS, HEAD_W)


def _ret_tables(c):
    log_g = jnp.log(1.0 - jnp.power(2.0, -5.0 - jnp.arange(HEADS, dtype=F32)))
    idx = jnp.arange(c, dtype=F32)
    rel = idx[:, None] - idx[None, :]
    dmask = jnp.where(rel >= 0, jnp.exp(log_g[:, None, None] * jnp.maximum(rel, 0.0)), 0.0)
    q_dec = jnp.exp(log_g[:, None] * (idx + 1.0)[None, :])
    k_dec = jnp.exp(log_g[:, None] * (c - 1.0 - idx)[None, :])
    b_dec = jnp.exp(log_g * c)
    bcast = lambda a: jnp.broadcast_to(a[:, :, None], (HEADS, c, HEAD_W))
    return dmask, bcast(q_dec), bcast(k_dec), jnp.broadcast_to(b_dec[:, None, None], (HEADS, 8, HEAD_W))


def _ret_kernel(q_ref, k_ref, v_ref, g_ref, s0_ref, dmask_ref, qdec_ref, kdec_ref, bdec_ref, gn_ref,
                y_ref, snew_ref, s_scr, *, c, n_c):
    t = pl.program_id(1)

    @pl.when(t == 0)
    def _():
        s_scr[...] = s0_ref[...]

    for h in range(HEADS):
        cols = slice(h * HEAD_W, (h + 1) * HEAD_W)
        state = s_scr[h]
        b_dec = bdec_ref[h][0:1, :]
        for ci in range(n_c):
            rows = slice(ci * c, (ci + 1) * c)
            q = q_ref[rows, cols]
            k = k_ref[rows, cols]
            v = v_ref[rows, cols]
            scores = lax.dot_general(q, k, _NT, preferred_element_type=F32) * dmask_ref[h]
            o = (jnp.dot(scores.astype(BF16), v, preferred_element_type=F32)
                 + qdec_ref[h] * jnp.dot(q, state.astype(BF16), preferred_element_type=F32))
            k_scaled = (k.astype(F32) * kdec_ref[h]).astype(BF16)
            state = b_dec * state + lax.dot_general(k_scaled, v, _TN, preferred_element_type=F32)
            mu = jnp.mean(o, axis=-1, keepdims=True)
            d = o - mu
            var = jnp.mean(d * d, axis=-1, keepdims=True)
            normed = d * lax.rsqrt(var + EPS) * gn_ref[:, cols]
            y_ref[rows, cols] = (_silu(g_ref[rows, cols].astype(F32)) * normed).astype(BF16)
        s_scr[h] = state

    @pl.when(t == pl.num_programs(1) - 1)
    def _():
        snew_ref[...] = s_scr[...]


def _retention(p, s0, gn, seq_len):
    b = p.shape[0]
    c = min(RET_CHUNK, seq_len)
    n_c = min(2, seq_len // c)
    t_rows = c * n_c
    tables = _ret_tables(c)
    col = lambda g: pl.BlockSpec((None, t_rows, GROUP_W), lambda bi, ti, g=g: (bi, ti, g))
    state = pl.BlockSpec((None, HEADS, RET_DK, HEAD_W), lambda bi, ti: (bi, 0, 0, 0))
    return pl.pallas_call(
        functools.partial(_ret_kernel, c=c, n_c=n_c),
        grid=(b, seq_len // t_rows),
        in_specs=[col(0), col(1), col(2), col(3), state,
                  _resident((HEADS, c, c)), _resident((HEADS, c, HEAD_W)), _resident((HEADS, c, HEAD_W)),
                  _resident((HEADS, 8, HEAD_W)), _resident((1, GROUP_W))],
        out_specs=[pl.BlockSpec((None, t_rows, GROUP_W), lambda bi, ti: (bi, ti, 0)), state],
        out_shape=[jax.ShapeDtypeStruct((b, seq_len, GROUP_W), BF16),
                   jax.ShapeDtypeStruct((b, HEADS, RET_DK, HEAD_W), F32)],
        scratch_shapes=[pltpu.VMEM((HEADS, RET_DK, HEAD_W), F32)],
        compiler_params=_params("parallel", "arbitrary"),
        name="retention",
    )(p, p, p, p, s0, *tables, gn)


def _lambda(lq1_ref, lk1_ref, lq2_ref, lk2_ref, lam_init):
    return (jnp.exp(jnp.sum(lq1_ref[...] * lk1_ref[...], axis=-1, keepdims=True))
            - jnp.exp(jnp.sum(lq2_ref[...] * lk2_ref[...], axis=-1, keepdims=True)) + lam_init)


def _split_components(q):
    lane = lax.broadcasted_iota(jnp.int32, q.shape, 1)
    zero = jnp.zeros_like(q)
    return jnp.where(lane < DIFF_DK, q, zero), jnp.where(lane >= DIFF_DK, q, zero)


def _flash_update(s, v, mask, m_ref, l_ref, acc_ref):
    if mask is not None:
        s = jnp.where(mask, s, MASK_VALUE)
    pad = -s.shape[1] % HEAD_W
    if pad:
        s = jnp.concatenate([s, jnp.full((s.shape[0], pad), MASK_VALUE, F32)], axis=1)
        v = jnp.concatenate([v, jnp.zeros((pad, v.shape[1]), v.dtype)], axis=0)
    blocks =[s[:, b * HEAD_W:(b + 1) * HEAD_W] for b in range(s.shape[1] // HEAD_W)]
    m_prev = m_ref[...]
    m_new = jnp.maximum(m_prev, jnp.max(functools.reduce(jnp.maximum, blocks), axis=-1, keepdims=True))
    alpha = jnp.exp2(m_prev - m_new)
    probs = [jnp.exp2(blk - m_new) for blk in blocks]
    l_ref[...] = alpha * l_ref[...] + functools.reduce(jnp.add, probs)
    p = jnp.concatenate([x.astype(BF16) for x in probs], axis=1)
    acc_ref[...] = alpha * acc_ref[...] + jnp.dot(p, v, preferred_element_type=F32)
    m_ref[...] = m_new


def _flash_init(m_ref, l_ref, acc_ref):
    m_ref[...] = jnp.full(m_ref.shape, MASK_VALUE, F32)
    l_ref[...] = jnp.zeros(l_ref.shape, F32)
    acc_ref[...] = jnp.zeros(acc_ref.shape, F32)


def _flash_update_t(s_t, vt_ones, mask, m_ref, acc_ref):
    if mask is not None:
        s_t = jnp.where(mask, s_t, MASK_VALUE)
    m_prev = m_ref[...]
    m_new = jnp.maximum(m_prev, jnp.max(s_t, axis=0, keepdims=True))
    alpha = jnp.exp2(m_prev - m_new)
    p_t = jnp.exp2(s_t - m_new).astype(BF16)
    acc_ref[...] = alpha * acc_ref[...] + jnp.dot(vt_ones, p_t, preferred_element_type=F32)
    m_ref[...] = m_new


def _attn_prompt_kernel(lq1_ref, lk1_ref, lq2_ref, lk2_ref, gn_ref, q_ref, k_ref, v_ref, o_ref,
                        *scratch, tile, lam_init):
    i = pl.program_id(1)
    n_sub = tile // ATTN_Q_SUB
    chains = [(qs, c) for qs in range(n_sub) for c in range(2)]
    stats = {ch: scratch[2 * n:2 * n + 2] for n, ch in enumerate(chains)}
    slot_pairs = (scratch[2 * len(chains):2 * len(chains) + 2], scratch[2 * len(chains) + 2:2 * len(chains) + 4])
    qt_scr = scratch[2 * len(chains) + 4]
    lam = _lambda(lq1_ref, lk1_ref, lq2_ref, lk2_ref, lam_init)
    key_chunk = lax.broadcasted_iota(jnp.int32, (tile, tile), 0) // CHUNK
    query_chunk = lax.broadcasted_iota(jnp.int32, (tile, tile), 1) // CHUNK
    diag_mask = key_chunk <= query_chunk

    def head_cols(h):
        return slice(h * HEAD_W, (h + 1) * HEAD_W)

    def key_rows(j):
        return pl.ds(pl.multiple_of(j * tile, tile), tile)

    def scores_into(h, slot, j):
        k = k_ref[key_rows(j), head_cols(h)]
        for n, (qs, c) in enumerate(chains):
            queries = slice(qs * ATTN_Q_SUB, (qs + 1) * ATTN_Q_SUB)
            slot[n] = jnp.dot(k, qt_scr[h % 2, c, :, queries], preferred_element_type=F32)

    def accumulate(h, slot, j, mask):
        v_t = v_ref[key_rows(j), head_cols(h)].T
        vt_ones = jnp.concatenate([v_t, jnp.ones((SUM_ROWS, tile), BF16)], axis=0)
        for n, (qs, c) in enumerate(chains):
            sub_mask = None if mask is None else mask[:, qs * ATTN_Q_SUB:(qs + 1) * ATTN_Q_SUB]
            _flash_update_t(slot[n], vt_ones, sub_mask, *stats[(qs, c)])

    def start_head(h):
        if h < HEADS:
            for c, comp in enumerate(_split_components(q_ref[:, head_cols(h)])):
                qt_scr[h % 2, c] = comp.T
            scores_into(h, slot_pairs[h % 2][0], 0)

    def finish_head(h):
        for qs in range(n_sub):
            (_, acc1_ref), (_, acc2_ref) = stats[(qs, 0)], stats[(qs, 1)]
            o_t = (acc1_ref[:HEAD_W] / acc1_ref[HEAD_W:HEAD_W + 1]
                   - lam * (acc2_ref[:HEAD_W] / acc2_ref[HEAD_W:HEAD_W + 1]))
            ms = jnp.mean(o_t * o_t, axis=0, keepdims=True)
            o_t = o_t * lax.rsqrt(ms + EPS) * gn_ref[...] * (1.0 - lam_init)
            o_ref[qs * ATTN_Q_SUB:(qs + 1) * ATTN_Q_SUB, head_cols(h)] = o_t.T.astype(BF16)

    start_head(0)
    for h in range(HEADS):
        slot_a, slot_b = slot_pairs[h % 2]
        for m_ref, acc_ref in stats.values():
            m_ref[...] = jnp.full(m_ref.shape, MASK_VALUE, F32)
            acc_ref[...] = jnp.zeros(acc_ref.shape, F32)

        def body(jj, carry, h=h, slot_a=slot_a, slot_b=slot_b):
            scores_into(h, slot_b, 2 * jj + 1)
            accumulate(h, slot_a, 2 * jj, None)
            scores_into(h, slot_a, 2 * jj + 2)
            accumulate(h, slot_b, 2 * jj + 1, None)
            return carry

        lax.fori_loop(0, i // 2, body, 0)

        @pl.when(i % 2 == 0)
        def _(h=h, slot_a=slot_a):
            start_head(h + 1)
            accumulate(h, slot_a, i, diag_mask)

        @pl.when(i % 2 == 1)
        def _(h=h, slot_a=slot_a, slot_b=slot_b):
            scores_into(h, slot_b, i)
            accumulate(h, slot_a, i - 1, None)
            start_head(h + 1)
            accumulate(h, slot_b, i, diag_mask)

        finish_head(h)


def _attn_prompt(p, lq1, lk1, lq2, lk2, gn, lam_init):
    b, seq_len, _ = p.shape
    tile = ATTN_TILE
    small = lambda n: pl.BlockSpec((1, n), lambda bi, qi: (0, 0))
    kv = lambda g: pl.BlockSpec((None, seq_len, GROUP_W), lambda bi, qi, g=g: (bi, 0, g))
    return pl.pallas_call(
        functools.partial(_attn_prompt_kernel, tile=tile, lam_init=lam_init),
        grid=(b, seq_len // tile),
        in_specs=[small(DIFF_DK)] * 4 + [pl.BlockSpec((HEAD_W, 1), lambda bi, qi: (0, 0)),
                  pl.BlockSpec((None, tile, GROUP_W), lambda bi, qi: (bi, qi, 4)), kv(5), kv(6)],
        out_specs=pl.BlockSpec((None, tile, GROUP_W), lambda bi, qi: (bi, qi, 0)),
        out_shape=jax.ShapeDtypeStruct((b, seq_len, GROUP_W), BF16),
        scratch_shapes=[pltpu.VMEM((1, ATTN_Q_SUB), F32),
                        pltpu.VMEM((HEAD_W + SUM_ROWS, ATTN_Q_SUB), F32)] * (2 * tile // ATTN_Q_SUB)
                       + [pltpu.VMEM((2 * tile // ATTN_Q_SUB, tile, ATTN_Q_SUB), F32)] * 4
                       + [pltpu.VMEM((2, 2, HEAD_W, tile), BF16)],
        compiler_params=_params("parallel", "arbitrary"),
        name="attn_prompt",
    )(lq1, lk1, lq2, lk2, gn.reshape(HEAD_W, 1), p, p, p)


def _attn_sample_kernel(lq1_ref, lk1_ref, lq2_ref, lk2_ref, gn_ref, q_ref, kp_ref, vp_ref, kn_ref, vn_ref,
                        o_ref, m_scr, l_scr, acc_scr, *, n_past, lam_init):
    j = pl.program_id(1)

    @pl.when(j == 0)
    def _():
        _flash_init(m_scr, l_scr, acc_scr)

    n_q = q_ref.shape[0]

    def update(scores_of, values):
        for h in range(HEADS):
            cols = slice(h * HEAD_W, (h + 1) * HEAD_W)
            q_both = jnp.concatenate(_split_components(q_ref[:, cols]), axis=0)
            _flash_update(scores_of(q_both, cols), values[:, cols].astype(BF16), None,
                          m_scr.at[h], l_scr.at[h], acc_scr.at[h])

    @pl.when(j < n_past)
    def _():
        update(lambda q, feats: jnp.dot(q, kp_ref[feats, :].astype(BF16), preferred_element_type=F32),
               vp_ref[...].reshape(vp_ref.shape[0], GROUP_W))

    @pl.when(j == n_past)
    def _():
        update(lambda q, feats: lax.dot_general(q, kn_ref[:, feats], _NT, preferred_element_type=F32),
               vn_ref[...])
        lam = _lambda(lq1_ref, lk1_ref, lq2_ref, lk2_ref, lam_init)
        for h in range(HEADS):
            l = jnp.sum(l_scr[h], axis=-1, keepdims=True)
            o_both = acc_scr[h] / l
            o = o_both[:n_q] - lam * o_both[n_q:]
            o_ref[:, h * HEAD_W:(h + 1) * HEAD_W] = (_rms(o, gn_ref[...]) * (1.0 - lam_init)).astype(BF16)


def _attn_sample(p, k_past, v_past, lq1, lk1, lq2, lk2, gn, lam_init):
    b, seq_len, _ = p.shape
    past_len = k_past.shape[1]
    tk = min(PAST_TILE, past_len)
    n_past = past_len // tk
    k_past = jnp.transpose(k_past, (0, 2, 3, 4, 1)).reshape(b, GROUP_W, past_len)
    small = lambda n: pl.BlockSpec((1, n), lambda bi, ji: (0, 0))
    new = lambda g: pl.BlockSpec((None, seq_len, GROUP_W), lambda bi, ji, g=g: (bi, 0, g))
    past_k = pl.BlockSpec((None, GROUP_W, tk), lambda bi, ji: (bi, 0, jnp.minimum(ji, n_past - 1)))
    past_v = pl.BlockSpec((None, tk, HEADS, HEAD_W), lambda bi, ji: (bi, jnp.minimum(ji, n_past - 1), 0, 0))
    return pl.pallas_call(
        functools.partial(_attn_sample_kernel, n_past=n_past, lam_init=lam_init),
        grid=(b, n_past + 1),
        in_specs=[small(DIFF_DK)] * 4 + [small(HEAD_W), new(4), past_k, past_v, new(5), new(6)],
        out_specs=pl.BlockSpec((None, seq_len, GROUP_W), lambda bi, ji: (bi, 0, 0)),
        out_shape=jax.ShapeDtypeStruct((b, seq_len, GROUP_W), BF16),
        scratch_shapes=[pltpu.VMEM((HEADS, 2 * seq_len, HEAD_W), F32)] * 3,
        compiler_params=_params("parallel", "arbitrary"),
        name="attn_sample",
    )(lq1, lk1, lq2, lk2, gn, p, k_past, v_past, p, p)


def _layer(x, ret_s0, k_past, v_past, w, lam_init):
    (f1_pre, f1_wg, f1_wu, f1_wd, f1_post, mix_pre, w_in, ret_g, lq1, lk1, lq2, lk2, diff_g, w_out,
     mix_post, f2_pre, f2_wg, f2_wu, f2_wd, f2_post) = w
    b, seq_len, _ = x.shape
    past_len = 0 if k_past is None else k_past.shape[1]
    x = x.reshape(b * seq_len, D_MODEL)
    x1 = _ffn(x, f1_pre, f1_wg, f1_wu, f1_wd, f1_post)
    p, k_new, v_new = _mix_in(x1, mix_pre, w_in, seq_len, past_len)
    ret_y, s_new = _retention(p, ret_s0, ret_g, seq_len)
    if k_past is None:
        diff_y = _attn_prompt(p, lq1, lk1, lq2, lk2, diff_g, lam_init)
    else:
        diff_y = _attn_sample(p, k_past, v_past, lq1, lk1, lq2, lk2, diff_g, lam_init)
    x3 = _out_ffn(x1, ret_y.reshape(b * seq_len, GROUP_W), diff_y.reshape(b * seq_len, GROUP_W),
                  w_out, mix_post, f2_pre, f2_wg, f2_wu, f2_wd, f2_post)
    return x3.reshape(b, seq_len, D_MODEL), s_new, k_new, v_new


def kernel(x_prompt, x_sample, state_ret, cache_diff_k, cache_diff_v, ffn1_pre_g, ffn1_w_gate, ffn1_w_up, ffn1_w_down, ffn1_post_g, mix_pre_g, w_in, ret_norm_g, diff_lq1, diff_lk1, diff_lq2, diff_lk2, diff_norm_g, w_out, mix_post_g, ffn2_pre_g, ffn2_w_gate, ffn2_w_up, ffn2_w_down, ffn2_post_g):
    depth = w_in.shape[0]
    xp, xs = x_prompt, x_sample
    outs = [[] for _ in range(6)]
    for li in range(depth):
        lam_init = 0.8 - 0.6 * math.exp(-0.3 * li)
        row = lambda a: a[li].reshape(1, -1)
        mat = lambda a: a[li].astype(BF16)
        w = (row(ffn1_pre_g), mat(ffn1_w_gate), mat(ffn1_w_up), mat(ffn1_w_down), row(ffn1_post_g),
             row(mix_pre_g), mat(w_in), row(ret_norm_g), row(diff_lq1), row(diff_lk1), row(diff_lq2),
             row(diff_lk2), row(diff_norm_g), mat(w_out), row(mix_post_g),
             row(ffn2_pre_g), mat(ffn2_w_gate), mat(ffn2_w_up), mat(ffn2_w_down), row(ffn2_post_g))
        s0 = jnp.zeros((xp.shape[0], HEADS, RET_DK, HEAD_W), F32)
        xp, s_p, k_p, v_p = _layer(xp, s0, None, None, w, lam_init)
        xs, s_s, k_s, v_s = _layer(xs, state_ret[li], cache_diff_k[li], cache_diff_v[li], w, lam_init)
        for acc, val in zip(outs, (s_p, k_p, v_p, s_s, k_s, v_s)):
            acc.append(val)
    return (xp, xs) + tuple(jnp.stack(o) for o in outs)
```

```python
import functools
import math

import jax
import jax.numpy as jnp
from jax import lax
from jax.experimental import pallas as pl
from jax.experimental.pallas import tpu as pltpu

D_MODEL = 1024
D_FF = 2816
CHUNK = 64
HEADS = 4
HEAD_W = 128
GROUP_W = HEADS * HEAD_W
N_GROUPS = 7
IN_COLS = N_GROUPS * GROUP_W
RET_DK = 128
DIFF_DK = 64
ROT_DIM = DIFF_DK // 4
ROPE_THETA = 500000.0
RET_THETA = 10000.0
EPS = 1e-6
MASK_VALUE = -1e30
LOG2E = math.log2(math.e)

ROW_TILE = 512
FFN_ROW_GROUPS = 2
RET_CHUNK = 256
ATTN_TILE = 512
SUM_ROWS = 16
ATTN_Q_SUB = 256
PAST_TILE = 1024
VMEM_LIMIT = 56 * 1024 * 1024

F32 = jnp.float32
BF16 = jnp.bfloat16
_NT = (((1,), (1,)), ((), ()))
_TN = (((0,), (0,)), ((), ()))


def _rms(x, g):
    return x * lax.rsqrt(jnp.mean(x * x, axis=-1, keepdims=True) + EPS) * g


def _silu(x):
    return x * (1.0 / (1.0 + jnp.exp(-x)))


def _resident(shape):
    nd = len(shape)
    return pl.BlockSpec(shape, lambda *_: (0,) * nd, pipeline_mode=pl.Buffered(1))


def _params(*sem):
    return pltpu.CompilerParams(dimension_semantics=sem, vmem_limit_bytes=VMEM_LIMIT)


def _ffn_math(x, pre_g, wg_ref, wu_ref, wd_ref, post_g):
    n = FFN_ROW_GROUPS if x.shape[0] % (8 * FFN_ROW_GROUPS) == 0 else 1
    rows = x.shape[0] // n
    xs = [x[g * rows:(g + 1) * rows] for g in range(n)]
    hs = [_rms(xg, pre_g).astype(BF16) for xg in xs]
    gates_ups = [(jnp.dot(h, wg_ref[...], preferred_element_type=F32),
                  jnp.dot(h, wu_ref[...], preferred_element_type=F32)) for h in hs]
    acts = [(_silu(gate) * up).astype(BF16) for gate, up in gates_ups]
    ys = [jnp.dot(act, wd_ref[...], preferred_element_type=F32) for act in acts]
    outs = [xg + 0.5 * _rms(y, post_g) for xg, y in zip(xs, ys)]
    return outs[0] if n == 1 else jnp.concatenate(outs, axis=0)


def _ffn_kernel(x_ref, pre_ref, wg_ref, wu_ref, wd_ref, post_ref, o_ref):
    o_ref[...] = _ffn_math(x_ref[...], pre_ref[...], wg_ref, wu_ref, wd_ref, post_ref[...])


def _ffn(x, pre_g, wg, wu, wd, post_g):
    m = x.shape[0]
    tm = min(ROW_TILE, m)
    row = pl.BlockSpec((tm, D_MODEL), lambda i: (i, 0))
    return pl.pallas_call(
        _ffn_kernel,
        grid=(m // tm,),
        in_specs=[row, _resident((1, D_MODEL)), _resident((D_MODEL, D_FF)), _resident((D_MODEL, D_FF)),
                  _resident((D_FF, D_MODEL)), _resident((1, D_MODEL))],
        out_specs=row,
        out_shape=jax.ShapeDtypeStruct((m, D_MODEL), F32),
        compiler_params=_params("parallel"),
        name="ffn",
    )(x, pre_g, wg, wu, wd, post_g)


def _out_ffn_kernel(x_ref, ry_ref, dy_ref, wo_ref, mixpost_ref, pre_ref, wg_ref, wu_ref, wd_ref, post_ref,
                    o_ref):
    y = (jnp.dot(ry_ref[...], wo_ref[:GROUP_W, :], preferred_element_type=F32)
         + jnp.dot(dy_ref[...], wo_ref[GROUP_W:, :], preferred_element_type=F32))
    x2 = x_ref[...] + _rms(y, mixpost_ref[...])
    o_ref[...] = _ffn_math(x2, pre_ref[...], wg_ref, wu_ref, wd_ref, post_ref[...])


def _out_ffn(x, ret_y, diff_y, wo, mixpost_g, pre_g, wg, wu, wd, post_g):
    m = x.shape[0]
    tm = min(ROW_TILE, m)
    row = pl.BlockSpec((tm, D_MODEL), lambda i: (i, 0))
    half = pl.BlockSpec((tm, GROUP_W), lambda i: (i, 0))
    return pl.pallas_call(
        _out_ffn_kernel,
        grid=(m // tm,),
        in_specs=[row, half, half, _resident((D_MODEL, D_MODEL)), _resident((1, D_MODEL)),
                  _resident((1, D_MODEL)), _resident((D_MODEL, D_FF)), _resident((D_MODEL, D_FF)),
                  _resident((D_FF, D_MODEL)), _resident((1, D_MODEL))],
        out_specs=row,
        out_shape=jax.ShapeDtypeStruct((m, D_MODEL), F32),
        compiler_params=_params("parallel"),
        name="out_ffn",
    )(x, ret_y, diff_y, wo, mixpost_g, pre_g, wg, wu, wd, post_g)


def _rope_tables(pos):
    pos = pos.astype(F32)[:, None]
    half = RET_DK // 2
    inv = jnp.power(jnp.float32(RET_THETA), -jnp.arange(half, dtype=F32) * (2.0 / RET_DK))
    ang = pos * inv[None, :]
    cos, sin = jnp.cos(ang), jnp.sin(ang)
    cr = jnp.concatenate([cos, cos], axis=1)
    sr = jnp.concatenate([-sin, sin], axis=1)
    half = ROT_DIM // 2
    inv = jnp.power(jnp.float32(ROPE_THETA), -jnp.arange(half, dtype=F32) * (2.0 / ROT_DIM))
    ang = pos * inv[None, :]
    cos, sin = jnp.cos(ang), jnp.sin(ang)
    n = pos.shape[0]
    ones = jnp.ones((n, DIFF_DK - ROT_DIM), F32)
    zeros = jnp.zeros((n, DIFF_DK - ROT_DIM), F32)
    z8 = jnp.zeros((n, half), F32)
    c_comp = jnp.concatenate([cos, cos, ones], axis=1)
    lo_comp = jnp.concatenate([-sin, z8, zeros], axis=1)
    hi_comp = jnp.concatenate([z8, sin, zeros], axis=1)
    cd = jnp.concatenate([c_comp, c_comp], axis=1)
    s_lo = jnp.concatenate([lo_comp, lo_comp], axis=1)
    s_hi = jnp.concatenate([hi_comp, hi_comp], axis=1)
    return cr, sr, cd, s_lo, s_hi


def _mix_in_kernel(x_ref, g_ref, w_ref, cr_ref, sr_ref, cd_ref, slo_ref, shi_ref, p_ref, kd_ref, vd_ref,
                   *, transpose_k):
    h = _rms(x_ref[...], g_ref[...]).astype(BF16)
    for grp in (6, 5, 4, 1, 0, 2, 3):
        pg = jnp.dot(h, w_ref[:, grp * GROUP_W:(grp + 1) * GROUP_W], preferred_element_type=F32)
        if grp == 6:
            vd_ref[...] = pg.reshape(pg.shape[0], HEADS, HEAD_W)
        for t in range(HEADS):
            cols = slice(t * HEAD_W, (t + 1) * HEAD_W)
            xt = pg[:, cols]
            if grp in (0, 1):
                xt = xt * cr_ref[...] + pltpu.roll(xt, RET_DK // 2, 1) * sr_ref[...]
                if grp == 1:
                    xt = xt * (RET_DK ** -0.5)
            elif grp in (4, 5):
                xt = (xt * cd_ref[...] + pltpu.roll(xt, HEAD_W - ROT_DIM // 2, 1) * slo_ref[...]
                      + pltpu.roll(xt, ROT_DIM // 2, 1) * shi_ref[...])
                if grp == 5 and transpose_k:
                    kd_ref[cols, :] = xt.T
                elif grp == 5:
                    kd_ref[:, cols] = xt
                else:
                    xt = xt * (DIFF_DK ** -0.5 * LOG2E)
            p_ref[:, grp * GROUP_W + t * HEAD_W:grp * GROUP_W + (t + 1) * HEAD_W] = xt.astype(BF16)


def _mix_in(x, g, w, seq_len, pos0):
    m = x.shape[0]
    tm = min(ROW_TILE, m)
    n_tab = max(seq_len, tm)
    pos = pos0 + (jnp.arange(n_tab, dtype=jnp.int32) % seq_len)
    tables = _rope_tables(pos)
    n_tab_blocks = n_tab // tm
    row = pl.BlockSpec((tm, D_MODEL), lambda i: (i, 0))
    tab = pl.BlockSpec((tm, HEAD_W), lambda i: (i % n_tab_blocks, 0))
    transpose_k = seq_len % tm == 0
    if transpose_k:
        n_seq_blocks = seq_len // tm
        k_spec = pl.BlockSpec((None, GROUP_W, tm), lambda i: (i // n_seq_blocks, 0, i % n_seq_blocks))
        k_shape = jax.ShapeDtypeStruct((m // seq_len, GROUP_W, seq_len), F32)
    else:
        k_spec = pl.BlockSpec((tm, GROUP_W), lambda i: (i, 0))
        k_shape = jax.ShapeDtypeStruct((m, GROUP_W), F32)
    p, k_new, v_new = pl.pallas_call(
        functools.partial(_mix_in_kernel, transpose_k=transpose_k),
        grid=(m // tm,),
        in_specs=[row, _resident((1, D_MODEL)), _resident((D_MODEL, IN_COLS)), tab, tab, tab, tab, tab],
        out_specs=[pl.BlockSpec((tm, IN_COLS), lambda i: (i, 0)), k_spec,
                   pl.BlockSpec((tm, HEADS, HEAD_W), lambda i: (i, 0, 0))],
        out_shape=[jax.ShapeDtypeStruct((m, IN_COLS), BF16), k_shape,
                   jax.ShapeDtypeStruct((m, HEADS, HEAD_W), F32)],
        compiler_params=_params("parallel"),
        name="mix_in",
    )(x, g, w, *tables)
    b = m // seq_len
    if transpose_k:
        k_new = jnp.transpose(k_new.reshape(b, HEADS, 2, DIFF_DK, seq_len), (0, 4, 1, 2, 3))
    else:
        k_new = k_new.reshape(b, seq_len, HEADS, 2, DIFF_DK)
    return p.reshape(b, seq_len, IN_COLS), k_new, v_new.reshape(b, seq_len, HEADS, HEAD_W)


def _ret_tables(c):
    log_g = jnp.log(1.0 - jnp.power(2.0, -5.0 - jnp.arange(HEADS, dtype=F32)))
    idx = jnp.arange(c, dtype=F32)
    rel = idx[:, None] - idx[None, :]
    dmask = jnp.where(rel >= 0, jnp.exp(log_g[:, None, None] * jnp.maximum(rel, 0.0)), 0.0)
    q_dec = jnp.exp(log_g[:, None] * (idx + 1.0)[None, :])
    k_dec = jnp.exp(log_g[:, None] * (c - 1.0 - idx)[None, :])
    b_dec = jnp.exp(log_g * c)
    bcast = lambda a: jnp.broadcast_to(a[:, :, None], (HEADS, c, HEAD_W))
    return dmask, bcast(q_dec), bcast(k_dec), jnp.broadcast_to(b_dec[:, None, None], (HEADS, 8, HEAD_W))


def _ret_kernel(q_ref, k_ref, v_ref, g_ref, s0_ref, dmask_ref, qdec_ref, kdec_ref, bdec_ref, gn_ref,
                y_ref, snew_ref, s_scr, *, c, n_c):
    t = pl.program_id(1)

    @pl.when(t == 0)
    def _():
        s_scr[...] = s0_ref[...]

    for h in range(HEADS):
        cols = slice(h * HEAD_W, (h + 1) * HEAD_W)
        state = s_scr[h]
        b_dec = bdec_ref[h][0:1, :]
        for ci in range(n_c):
            rows = slice(ci * c, (ci + 1) * c)
            q = q_ref[rows, cols]
            k = k_ref[rows, cols]
            v = v_ref[rows, cols]
            scores = lax.dot_general(q, k, _NT, preferred_element_type=F32) * dmask_ref[h]
            o = (jnp.dot(scores.astype(BF16), v, preferred_element_type=F32)
                 + qdec_ref[h] * jnp.dot(q, state.astype(BF16), preferred_element_type=F32))
            k_scaled = (k.astype(F32) * kdec_ref[h]).astype(BF16)
            state = b_dec * state + lax.dot_general(k_scaled, v, _TN, preferred_element_type=F32)
            mu = jnp.mean(o, axis=-1, keepdims=True)
            d = o - mu
            var = jnp.mean(d * d, axis=-1, keepdims=True)
            normed = d * lax.rsqrt(var + EPS) * gn_ref[:, cols]
            y_ref[rows, cols] = (_silu(g_ref[rows, cols].astype(F32)) * normed).astype(BF16)
        s_scr[h] = state

    @pl.when(t == pl.num_programs(1) - 1)
    def _():
        snew_ref[...] = s_scr[...]


def _retention(p, s0, gn, seq_len):
    b = p.shape[0]
    c = min(RET_CHUNK, seq_len)
    n_c = min(2, seq_len // c)
    t_rows = c * n_c
    tables = _ret_tables(c)
    col = lambda g: pl.BlockSpec((None, t_rows, GROUP_W), lambda bi, ti, g=g: (bi, ti, g))
    state = pl.BlockSpec((None, HEADS, RET_DK, HEAD_W), lambda bi, ti: (bi, 0, 0, 0))
    return pl.pallas_call(
        functools.partial(_ret_kernel, c=c, n_c=n_c),
        grid=(b, seq_len // t_rows),
        in_specs=[col(0), col(1), col(2), col(3), state,
                  _resident((HEADS, c, c)), _resident((HEADS, c, HEAD_W)), _resident((HEADS, c, HEAD_W)),
                  _resident((HEADS, 8, HEAD_W)), _resident((1, GROUP_W))],
        out_specs=[pl.BlockSpec((None, t_rows, GROUP_W), lambda bi, ti: (bi, ti, 0)), state],
        out_shape=[jax.ShapeDtypeStruct((b, seq_len, GROUP_W), BF16),
                   jax.ShapeDtypeStruct((b, HEADS, RET_DK, HEAD_W), F32)],
        scratch_shapes=[pltpu.VMEM((HEADS, RET_DK, HEAD_W), F32)],
        compiler_params=_params("parallel", "arbitrary"),
        name="retention",
    )(p, p, p, p, s0, *tables, gn)


def _lambda(lq1_ref, lk1_ref, lq2_ref, lk2_ref, lam_init):
    return (jnp.exp(jnp.sum(lq1_ref[...] * lk1_ref[...], axis=-1, keepdims=True))
            - jnp.exp(jnp.sum(lq2_ref[...] * lk2_ref[...], axis=-1, keepdims=True)) + lam_init)


def _split_components(q):
    lane = lax.broadcasted_iota(jnp.int32, q.shape, 1)
    zero = jnp.zeros_like(q)
    return jnp.where(lane < DIFF_DK, q, zero), jnp.where(lane >= DIFF_DK, q, zero)


def _flash_update(s, v, mask, m_ref, l_ref, acc_ref):
    if mask is not None:
        s = jnp.where(mask, s, MASK_VALUE)
    pad = -s.shape[1] % HEAD_W
    if pad:
        s = jnp.concatenate([s, jnp.full((s.shape[0], pad), MASK_VALUE, F32)], axis=1)
        v = jnp.concatenate([v, jnp.zeros((pad, v.shape[1]), v.dtype)], axis=0)
    blocks =[s[:, b * HEAD_W:(b + 1) * HEAD_W] for b in range(s.shape[1] // HEAD_W)]
    m_prev = m_ref[...]
    m_new = jnp.maximum(m_prev, jnp.max(functools.reduce(jnp.maximum, blocks), axis=-1, keepdims=True))
    alpha = jnp.exp2(m_prev - m_new)
    probs = [jnp.exp2(blk - m_new) for blk in blocks]
    l_ref[...] = alpha * l_ref[...] + functools.reduce(jnp.add, probs)
    p = jnp.concatenate([x.astype(BF16) for x in probs], axis=1)
    acc_ref[...] = alpha * acc_ref[...] + jnp.dot(p, v, preferred_element_type=F32)
    m_ref[...] = m_new


def _flash_init(m_ref, l_ref, acc_ref):
    m_ref[...] = jnp.full(m_ref.shape, MASK_VALUE, F32)
    l_ref[...] = jnp.zeros(l_ref.shape, F32)
    acc_ref[...] = jnp.zeros(acc_ref.shape, F32)


def _flash_update_t(s_t, s_max, vt_ones, mask, m_ref, acc_ref):
    if mask is not None:
        s_t = jnp.where(mask, s_t, MASK_VALUE)
        s_max = jnp.max(s_t, axis=0, keepdims=True)
    m_prev = m_ref[...]
    m_new = jnp.maximum(m_prev, s_max)
    alpha = jnp.exp2(m_prev - m_new)
    p_t = jnp.exp2(s_t - m_new).astype(BF16)
    acc_ref[...] = alpha * acc_ref[...] + jnp.dot(vt_ones, p_t, preferred_element_type=F32)
    m_ref[...] = m_new


def _attn_prompt_kernel(lq1_ref, lk1_ref, lq2_ref, lk2_ref, gn_ref, q_ref, k_ref, v_ref, o_ref,
                        *scratch, tile, lam_init):
    i = pl.program_id(1)
    n_sub = tile // ATTN_Q_SUB
    chains = [(qs, c) for qs in range(n_sub) for c in range(2)]
    stats = {ch: scratch[2 * n:2 * n + 2] for n, ch in enumerate(chains)}
    base = 2 * len(chains)
    slots = [(scratch[base + n], scratch[base + 5 + n]) for n in range(4)]
    slot_pairs = (slots[0:2], slots[2:4])
    qt_scr = scratch[base + 4]
    lam = _lambda(lq1_ref, lk1_ref, lq2_ref, lk2_ref, lam_init)
    key_chunk = lax.broadcasted_iota(jnp.int32, (tile, tile), 0) // CHUNK
    query_chunk = lax.broadcasted_iota(jnp.int32, (tile, tile), 1) // CHUNK
    diag_mask = key_chunk <= query_chunk

    def head_cols(h):
        return slice(h * HEAD_W, (h + 1) * HEAD_W)

    def key_rows(j):
        return pl.ds(pl.multiple_of(j * tile, tile), tile)

    def scores_into(h, slot, j):
        k = k_ref[key_rows(j), head_cols(h)]
        for n, (qs, c) in enumerate(chains):
            queries = slice(qs * ATTN_Q_SUB, (qs + 1) * ATTN_Q_SUB)
            s_t = jnp.dot(k, qt_scr[h % 2, c, :, queries], preferred_element_type=F32)
            slot[0][n] = s_t
            slot[1][n] = jnp.max(s_t, axis=0, keepdims=True)

    def accumulate(h, slot, j, mask):
        v_t = v_ref[key_rows(j), head_cols(h)].T
        vt_ones = jnp.concatenate([v_t, jnp.ones((SUM_ROWS, tile), BF16)], axis=0)
        for n, (qs, c) in enumerate(chains):
            sub_mask = None if mask is None else mask[:, qs * ATTN_Q_SUB:(qs + 1) * ATTN_Q_SUB]
            _flash_update_t(slot[0][n], slot[1][n], vt_ones, sub_mask, *stats[(qs, c)])

    def start_head(h):
        if h < HEADS:
            for c, comp in enumerate(_split_components(q_ref[:, head_cols(h)])):
                qt_scr[h % 2, c] = comp.T
            scores_into(h, slot_pairs[h % 2][0], 0)

    def finish_head(h):
        for qs in range(n_sub):
            (_, acc1_ref), (_, acc2_ref) = stats[(qs, 0)], stats[(qs, 1)]
            o_t = (acc1_ref[:HEAD_W] / acc1_ref[HEAD_W:HEAD_W + 1]
                   - lam * (acc2_ref[:HEAD_W] / acc2_ref[HEAD_W:HEAD_W + 1]))
            ms = jnp.mean(o_t * o_t, axis=0, keepdims=True)
            o_t = o_t * lax.rsqrt(ms + EPS) * gn_ref[...] * (1.0 - lam_init)
            o_ref[qs * ATTN_Q_SUB:(qs + 1) * ATTN_Q_SUB, head_cols(h)] = o_t.T.astype(BF16)

    start_head(0)
    for h in range(HEADS):
        slot_a, slot_b = slot_pairs[h % 2]
        for m_ref, acc_ref in stats.values():
            m_ref[...] = jnp.full(m_ref.shape, MASK_VALUE, F32)
            acc_ref[...] = jnp.zeros(acc_ref.shape, F32)

        def body(jj, carry, h=h, slot_a=slot_a, slot_b=slot_b):
            scores_into(h, slot_b, 2 * jj + 1)
            accumulate(h, slot_a, 2 * jj, None)
            scores_into(h, slot_a, 2 * jj + 2)
            accumulate(h, slot_b, 2 * jj + 1, None)
            return carry

        lax.fori_loop(0, i // 2, body, 0)

        @pl.when(i % 2 == 0)
        def _(h=h, slot_a=slot_a):
            start_head(h + 1)
            accumulate(h, slot_a, i, diag_mask)

        @pl.when(i % 2 == 1)
        def _(h=h, slot_a=slot_a, slot_b=slot_b):
            scores_into(h, slot_b, i)
            accumulate(h, slot_a, i - 1, None)
            start_head(h + 1)
            accumulate(h, slot_b, i, diag_mask)

        finish_head(h)


def _attn_prompt(p, lq1, lk1, lq2, lk2, gn, lam_init):
    b, seq_len, _ = p.shape
    tile = ATTN_TILE
    small = lambda n: pl.BlockSpec((1, n), lambda bi, qi: (0, 0))
    kv = lambda g: pl.BlockSpec((None, seq_len, GROUP_W), lambda bi, qi, g=g: (bi, 0, g))
    return pl.pallas_call(
        functools.partial(_attn_prompt_kernel, tile=tile, lam_init=lam_init),
        grid=(b, seq_len // tile),
        in_specs=[small(DIFF_DK)] * 4 + [pl.BlockSpec((HEAD_W, 1), lambda bi, qi: (0, 0)),
                  pl.BlockSpec((None, tile, GROUP_W), lambda bi, qi: (bi, qi, 4)), kv(5), kv(6)],
        out_specs=pl.BlockSpec((None, tile, GROUP_W), lambda bi, qi: (bi, qi, 0)),
        out_shape=jax.ShapeDtypeStruct((b, seq_len, GROUP_W), BF16),
        scratch_shapes=[pltpu.VMEM((1, ATTN_Q_SUB), F32),
                        pltpu.VMEM((HEAD_W + SUM_ROWS, ATTN_Q_SUB), F32)] * (2 * tile // ATTN_Q_SUB)
                       + [pltpu.VMEM((2 * tile // ATTN_Q_SUB, tile, ATTN_Q_SUB), F32)] * 4
                       + [pltpu.VMEM((2, 2, HEAD_W, tile), BF16)]
                       + [pltpu.VMEM((2 * tile // ATTN_Q_SUB, 1, ATTN_Q_SUB), F32)] * 4,
        compiler_params=_params("parallel", "arbitrary"),
        name="attn_prompt",
    )(lq1, lk1, lq2, lk2, gn.reshape(HEAD_W, 1), p, p, p)


def _attn_sample_kernel(lq1_ref, lk1_ref, lq2_ref, lk2_ref, gn_ref, q_ref, kp_ref, vp_ref, kn_ref, vn_ref,
                        o_ref, m_scr, l_scr, acc_scr, *, n_past, lam_init):
    j = pl.program_id(1)

    @pl.when(j == 0)
    def _():
        _flash_init(m_scr, l_scr, acc_scr)

    n_q = q_ref.shape[0]

    def update(scores_of, values):
        for h in range(HEADS):
            cols = slice(h * HEAD_W, (h + 1) * HEAD_W)
            q_both = jnp.concatenate(_split_components(q_ref[:, cols]), axis=0)
            _flash_update(scores_of(q_both, cols), values[:, cols].astype(BF16), None,
                          m_scr.at[h], l_scr.at[h], acc_scr.at[h])

    @pl.when(j < n_past)
    def _():
        update(lambda q, feats: jnp.dot(q, kp_ref[feats, :].astype(BF16), preferred_element_type=F32),
               vp_ref[...].reshape(vp_ref.shape[0], GROUP_W))

    @pl.when(j == n_past)
    def _():
        update(lambda q, feats: lax.dot_general(q, kn_ref[:, feats], _NT, preferred_element_type=F32),
               vn_ref[...])
        lam = _lambda(lq1_ref, lk1_ref, lq2_ref, lk2_ref, lam_init)
        for h in range(HEADS):
            l = jnp.sum(l_scr[h], axis=-1, keepdims=True)
            o_both = acc_scr[h] / l
            o = o_both[:n_q] - lam * o_both[n_q:]
            o_ref[:, h * HEAD_W:(h + 1) * HEAD_W] = (_rms(o, gn_ref[...]) * (1.0 - lam_init)).astype(BF16)


def _attn_sample(p, k_past, v_past, lq1, lk1, lq2, lk2, gn, lam_init):
    b, seq_len, _ = p.shape
    past_len = k_past.shape[1]
    tk = min(PAST_TILE, past_len)
    n_past = past_len // tk
    k_past = jnp.transpose(k_past, (0, 2, 3, 4, 1)).reshape(b, GROUP_W, past_len)
    small = lambda n: pl.BlockSpec((1, n), lambda bi, ji: (0, 0))
    new = lambda g: pl.BlockSpec((None, seq_len, GROUP_W), lambda bi, ji, g=g: (bi, 0, g))
    past_k = pl.BlockSpec((None, GROUP_W, tk), lambda bi, ji: (bi, 0, jnp.minimum(ji, n_past - 1)))
    past_v = pl.BlockSpec((None, tk, HEADS, HEAD_W), lambda bi, ji: (bi, jnp.minimum(ji, n_past - 1), 0, 0))
    return pl.pallas_call(
        functools.partial(_attn_sample_kernel, n_past=n_past, lam_init=lam_init),
        grid=(b, n_past + 1),
        in_specs=[small(DIFF_DK)] * 4 + [small(HEAD_W), new(4), past_k, past_v, new(5), new(6)],
        out_specs=pl.BlockSpec((None, seq_len, GROUP_W), lambda bi, ji: (bi, 0, 0)),
        out_shape=jax.ShapeDtypeStruct((b, seq_len, GROUP_W), BF16),
        scratch_shapes=[pltpu.VMEM((HEADS, 2 * seq_len, HEAD_W), F32)] * 3,
        compiler_params=_params("parallel", "arbitrary"),
        name="attn_sample",
    )(lq1, lk1, lq2, lk2, gn, p, k_past, v_past, p, p)


def _layer(x, ret_s0, k_past, v_past, w, lam_init):
    (f1_pre, f1_wg, f1_wu, f1_wd, f1_post, mix_pre, w_in, ret_g, lq1, lk1, lq2, lk2, diff_g, w_out,
     mix_post, f2_pre, f2_wg, f2_wu, f2_wd, f2_post) = w
    b, seq_len, _ = x.shape
    past_len = 0 if k_past is None else k_past.shape[1]
    x = x.reshape(b * seq_len, D_MODEL)
    x1 = _ffn(x, f1_pre, f1_wg, f1_wu, f1_wd, f1_post)
    p, k_new, v_new = _mix_in(x1, mix_pre, w_in, seq_len, past_len)
    ret_y, s_new = _retention(p, ret_s0, ret_g, seq_len)
    if k_past is None:
        diff_y = _attn_prompt(p, lq1, lk1, lq2, lk2, diff_g, lam_init)
    else:
        diff_y = _attn_sample(p, k_past, v_past, lq1, lk1, lq2, lk2, diff_g, lam_init)
    x3 = _out_ffn(x1, ret_y.reshape(b * seq_len, GROUP_W), diff_y.reshape(b * seq_len, GROUP_W),
                  w_out, mix_post, f2_pre, f2_wg, f2_wu, f2_wd, f2_post)
    return x3.reshape(b, seq_len, D_MODEL), s_new, k_new, v_new


def kernel(x_prompt, x_sample, state_ret, cache_diff_k, cache_diff_v, ffn1_pre_g, ffn1_w_gate, ffn1_w_up, ffn1_w_down, ffn1_post_g, mix_pre_g, w_in, ret_norm_g, diff_lq1, diff_lk1, diff_lq2, diff_lk2, diff_norm_g, w_out, mix_post_g, ffn2_pre_g, ffn2_w_gate, ffn2_w_up, ffn2_w_down, ffn2_post_g):
    depth = w_in.shape[0]
    xp, xs = x_prompt, x_sample
    outs = [[] for _ in range(6)]
    for li in range(depth):
        lam_init = 0.8 - 0.6 * math.exp(-0.3 * li)
        row = lambda a: a[li].reshape(1, -1)
        mat = lambda a: a[li].astype(BF16)
        w = (row(ffn1_pre_g), mat(ffn1_w_gate), mat(ffn1_w_up), mat(ffn1_w_down), row(ffn1_post_g),
             row(mix_pre_g), mat(w_in), row(ret_norm_g), row(diff_lq1), row(diff_lk1), row(diff_lq2),
             row(diff_lk2), row(diff_norm_g), mat(w_out), row(mix_post_g),
             row(ffn2_pre_g), mat(ffn2_w_gate), mat(ffn2_w_up), mat(ffn2_w_down), row(ffn2_post_g))
        s0 = jnp.zeros((xp.shape[0], HEADS, RET_DK, HEAD_W), F32)
        xp, s_p, k_p, v_p = _layer(xp, s0, None, None, w, lam_init)
        xs, s_s, k_s, v_s = _layer(xs, state_ret[li], cache_diff_k[li], cache_diff_v[li], w, lam_init)
        for acc, val in zip(outs, (s_p, k_p, v_p, s_s, k_s, v_s)):
            acc.append(val)
    return (xp, xs) + tuple(jnp.stack(o) for o in outs)
```

```python
import functools
import math

import jax
import jax.numpy as jnp
from jax import lax
from jax.experimental import pallas as pl
from jax.experimental.pallas import tpu as pltpu

D_MODEL = 1024
D_FF = 2816
CHUNK = 64
HEADS = 4
HEAD_W = 128
GROUP_W = HEADS * HEAD_W
N_GROUPS = 7
IN_COLS = N_GROUPS * GROUP_W
RET_DK = 128
DIFF_DK = 64
ROT_DIM = DIFF_DK // 4
ROPE_THETA = 500000.0
RET_THETA = 10000.0
EPS = 1e-6
MASK_VALUE = -1e30
LOG2E = math.log2(math.e)

ROW_TILE = 1024
FFN_ROW_TILE = 1024
FFN_GROUP_ROWS = 128
RET_CHUNK = 256
ATTN_TILE = 512
SUM_ROWS = 16
ATTN_Q_SUB = 256
PAST_TILE = 1024
VMEM_LIMIT = 56 * 1024 * 1024

F32 = jnp.float32
BF16 = jnp.bfloat16
_NT = (((1,), (1,)), ((), ()))
_TN = (((0,), (0,)), ((), ()))


def _rms(x, g):
    return x * lax.rsqrt(jnp.mean(x * x, axis=-1, keepdims=True) + EPS) * g


def _silu(x):
    return x * (1.0 / (1.0 + jnp.exp(-x)))


def _resident(shape):
    nd = len(shape)
    return pl.BlockSpec(shape, lambda *_: (0,) * nd, pipeline_mode=pl.Buffered(1))


def _params(*sem):
    return pltpu.CompilerParams(dimension_semantics=sem, vmem_limit_bytes=VMEM_LIMIT)


def _ffn_pipeline(n_rows, x_of, o_ref, pre_g, wg_ref, wu_ref, wd_ref, post_g):
    n = pl.cdiv(n_rows, FFN_GROUP_ROWS)
    groups = [slice(g * FFN_GROUP_ROWS, min((g + 1) * FFN_GROUP_ROWS, n_rows)) for g in range(n)]

    def gate_up(rows):
        xg = x_of(rows)
        h = _rms(xg, pre_g).astype(BF16)
        return (xg, jnp.dot(h, wg_ref[...], preferred_element_type=F32),
                jnp.dot(h, wu_ref[...], preferred_element_type=F32))

    def down(rows, xg, gate, up):
        act = (_silu(gate) * up).astype(BF16)
        o_ref[rows, :] = xg + 0.5 * _rms(jnp.dot(act, wd_ref[...], preferred_element_type=F32), post_g)

    pending = gate_up(groups[0])
    for g in range(n):
        following = gate_up(groups[g + 1]) if g + 1 < n else None
        down(groups[g], *pending)
        pending = following


def _ffn_kernel(x_ref, pre_ref, wg_ref, wu_ref, wd_ref, post_ref, o_ref):
    _ffn_pipeline(x_ref.shape[0], lambda rows: x_ref[rows, :], o_ref,
                  pre_ref[...], wg_ref, wu_ref, wd_ref, post_ref[...])


def _ffn(x, pre_g, wg, wu, wd, post_g):
    m = x.shape[0]
    tm = min(FFN_ROW_TILE, m)
    row = pl.BlockSpec((tm, D_MODEL), lambda i: (i, 0))
    return pl.pallas_call(
        _ffn_kernel,
        grid=(m // tm,),
        in_specs=[row, _resident((1, D_MODEL)), _resident((D_MODEL, D_FF)), _resident((D_MODEL, D_FF)),
                  _resident((D_FF, D_MODEL)), _resident((1, D_MODEL))],
        out_specs=row,
        out_shape=jax.ShapeDtypeStruct((m, D_MODEL), F32),
        compiler_params=_params("parallel"),
        name="ffn",
    )(x, pre_g, wg, wu, wd, post_g)


def _out_ffn_kernel(x_ref, ry_ref, dy_ref, wo_ref, mixpost_ref, pre_ref, wg_ref, wu_ref, wd_ref, post_ref,
                    o_ref):
    y = (jnp.dot(ry_ref[...], wo_ref[:GROUP_W, :], preferred_element_type=F32)
         + jnp.dot(dy_ref[...], wo_ref[GROUP_W:, :], preferred_element_type=F32))
    x2 = x_ref[...] + _rms(y, mixpost_ref[...])
    _ffn_pipeline(x_ref.shape[0], lambda rows: x2[rows, :], o_ref,
                  pre_ref[...], wg_ref, wu_ref, wd_ref, post_ref[...])


def _out_ffn(x, ret_y, diff_y, wo, mixpost_g, pre_g, wg, wu, wd, post_g):
    m = x.shape[0]
    tm = min(FFN_ROW_TILE, m)
    row = pl.BlockSpec((tm, D_MODEL), lambda i: (i, 0))
    half = pl.BlockSpec((tm, GROUP_W), lambda i: (i, 0))
    return pl.pallas_call(
        _out_ffn_kernel,
        grid=(m // tm,),
        in_specs=[row, half, half, _resident((D_MODEL, D_MODEL)), _resident((1, D_MODEL)),
                  _resident((1, D_MODEL)), _resident((D_MODEL, D_FF)), _resident((D_MODEL, D_FF)),
                  _resident((D_FF, D_MODEL)), _resident((1, D_MODEL))],
        out_specs=row,
        out_shape=jax.ShapeDtypeStruct((m, D_MODEL), F32),
        compiler_params=_params("parallel"),
        name="out_ffn",
    )(x, ret_y, diff_y, wo, mixpost_g, pre_g, wg, wu, wd, post_g)


def _rope_tables(pos):
    pos = pos.astype(F32)[:, None]
    half = RET_DK // 2
    inv = jnp.power(jnp.float32(RET_THETA), -jnp.arange(half, dtype=F32) * (2.0 / RET_DK))
    ang = pos * inv[None, :]
    cos, sin = jnp.cos(ang), jnp.sin(ang)
    cr = jnp.concatenate([cos, cos], axis=1)
    sr = jnp.concatenate([-sin, sin], axis=1)
    half = ROT_DIM // 2
    inv = jnp.power(jnp.float32(ROPE_THETA), -jnp.arange(half, dtype=F32) * (2.0 / ROT_DIM))
    ang = pos * inv[None, :]
    cos, sin = jnp.cos(ang), jnp.sin(ang)
    n = pos.shape[0]
    ones = jnp.ones((n, DIFF_DK - ROT_DIM), F32)
    zeros = jnp.zeros((n, DIFF_DK - ROT_DIM), F32)
    z8 = jnp.zeros((n, half), F32)
    c_comp = jnp.concatenate([cos, cos, ones], axis=1)
    lo_comp = jnp.concatenate([-sin, z8, zeros], axis=1)
    hi_comp = jnp.concatenate([z8, sin, zeros], axis=1)
    cd = jnp.concatenate([c_comp, c_comp], axis=1)
    s_lo = jnp.concatenate([lo_comp, lo_comp], axis=1)
    s_hi = jnp.concatenate([hi_comp, hi_comp], axis=1)
    return cr, sr, cd, s_lo, s_hi


def _mix_in_kernel(x_ref, g_ref, w_ref, cr_ref, sr_ref, cd_ref, slo_ref, shi_ref, p_ref, kd_ref, vd_ref,
                   *, transpose_k):
    h = _rms(x_ref[...], g_ref[...]).astype(BF16)
    for grp in (6, 5, 4, 1, 0, 2, 3):
        pg = jnp.dot(h, w_ref[:, grp * GROUP_W:(grp + 1) * GROUP_W], preferred_element_type=F32)
        if grp == 6:
            vd_ref[...] = pg.reshape(pg.shape[0], HEADS, HEAD_W)
        for t in range(HEADS):
            cols = slice(t * HEAD_W, (t + 1) * HEAD_W)
            xt = pg[:, cols]
            if grp in (0, 1):
                xt = xt * cr_ref[...] + pltpu.roll(xt, RET_DK // 2, 1) * sr_ref[...]
                if grp == 1:
                    xt = xt * (RET_DK ** -0.5)
            elif grp in (4, 5):
                xt = (xt * cd_ref[...] + pltpu.roll(xt, HEAD_W - ROT_DIM // 2, 1) * slo_ref[...]
                      + pltpu.roll(xt, ROT_DIM // 2, 1) * shi_ref[...])
                if grp == 5 and transpose_k:
                    kd_ref[cols, :] = xt.T
                elif grp == 5:
                    kd_ref[:, cols] = xt
                else:
                    xt = xt * (DIFF_DK ** -0.5 * LOG2E)
            p_ref[:, grp * GROUP_W + t * HEAD_W:grp * GROUP_W + (t + 1) * HEAD_W] = xt.astype(BF16)


def _mix_in(x, g, w, seq_len, pos0):
    m = x.shape[0]
    tm = min(ROW_TILE, m)
    n_tab = max(seq_len, tm)
    pos = pos0 + (jnp.arange(n_tab, dtype=jnp.int32) % seq_len)
    tables = _rope_tables(pos)
    n_tab_blocks = n_tab // tm
    row = pl.BlockSpec((tm, D_MODEL), lambda i: (i, 0))
    tab = pl.BlockSpec((tm, HEAD_W), lambda i: (i % n_tab_blocks, 0))
    transpose_k = seq_len % tm == 0
    if transpose_k:
        n_seq_blocks = seq_len // tm
        k_spec = pl.BlockSpec((None, GROUP_W, tm), lambda i: (i // n_seq_blocks, 0, i % n_seq_blocks))
        k_shape = jax.ShapeDtypeStruct((m // seq_len, GROUP_W, seq_len), F32)
    else:
        k_spec = pl.BlockSpec((tm, GROUP_W), lambda i: (i, 0))
        k_shape = jax.ShapeDtypeStruct((m, GROUP_W), F32)
    p, k_new, v_new = pl.pallas_call(
        functools.partial(_mix_in_kernel, transpose_k=transpose_k),
        grid=(m // tm,),
        in_specs=[row, _resident((1, D_MODEL)), _resident((D_MODEL, IN_COLS)), tab, tab, tab, tab, tab],
        out_specs=[pl.BlockSpec((tm, IN_COLS), lambda i: (i, 0)), k_spec,
                   pl.BlockSpec((tm, HEADS, HEAD_W), lambda i: (i, 0, 0))],
        out_shape=[jax.ShapeDtypeStruct((m, IN_COLS), BF16), k_shape,
                   jax.ShapeDtypeStruct((m, HEADS, HEAD_W), F32)],
        compiler_params=_params("parallel"),
        name="mix_in",
    )(x, g, w, *tables)
    b = m // seq_len
    if transpose_k:
        k_new = jnp.transpose(k_new.reshape(b, HEADS, 2, DIFF_DK, seq_len), (0, 4, 1, 2, 3))
    else:
        k_new = k_new.reshape(b, seq_len, HEADS, 2, DIFF_DK)
    return p.reshape(b, seq_len, IN_COLS), k_new, v_new.reshape(b, seq_len, HEADS, HEAD_W)


def _ret_tables(c):
    log_g = jnp.log(1.0 - jnp.power(2.0, -5.0 - jnp.arange(HEADS, dtype=F32)))
    idx = jnp.arange(c, dtype=F32)
    rel = idx[:, None] - idx[None, :]
    dmask = jnp.where(rel >= 0, jnp.exp(log_g[:, None, None] * jnp.maximum(rel, 0.0)), 0.0)
    q_dec = jnp.exp(log_g[:, None] * (idx + 1.0)[None, :])
    k_dec = jnp.exp(log_g[:, None] * (c - 1.0 - idx)[None, :])
    b_dec = jnp.exp(log_g * c)
    bcast = lambda a: jnp.broadcast_to(a[:, :, None], (HEADS, c, HEAD_W))
    return dmask, bcast(q_dec), bcast(k_dec), jnp.broadcast_to(b_dec[:, None, None], (HEADS, 8, HEAD_W))


def _ret_kernel(q_ref, k_ref, v_ref, g_ref, s0_ref, dmask_ref, qdec_ref, kdec_ref, bdec_ref, gn_ref,
                y_ref, snew_ref, s_scr, *, c, n_c):
    t = pl.program_id(1)

    @pl.when(t == 0)
    def _():
        s_scr[...] = s0_ref[...]

    for h in range(HEADS):
        cols = slice(h * HEAD_W, (h + 1) * HEAD_W)
        state = s_scr[h]
        b_dec = bdec_ref[h][0:1, :]
        for ci in range(n_c):
            rows = slice(ci * c, (ci + 1) * c)
            q = q_ref[rows, cols]
            k = k_ref[rows, cols]
            v = v_ref[rows, cols]
            scores = lax.dot_general(q, k, _NT, preferred_element_type=F32) * dmask_ref[h]
            o = (jnp.dot(scores.astype(BF16), v, preferred_element_type=F32)
                 + qdec_ref[h] * jnp.dot(q, state.astype(BF16), preferred_element_type=F32))
            k_scaled = (k.astype(F32) * kdec_ref[h]).astype(BF16)
            state = b_dec * state + lax.dot_general(k_scaled, v, _TN, preferred_element_type=F32)
            mu = jnp.mean(o, axis=-1, keepdims=True)
            d = o - mu
            var = jnp.mean(d * d, axis=-1, keepdims=True)
            normed = d * lax.rsqrt(var + EPS) * gn_ref[:, cols]
            y_ref[rows, cols] = (_silu(g_ref[rows, cols].astype(F32)) * normed).astype(BF16)
        s_scr[h] = state

    @pl.when(t == pl.num_programs(1) - 1)
    def _():
        snew_ref[...] = s_scr[...]


def _retention(p, s0, gn, seq_len):
    b = p.shape[0]
    c = min(RET_CHUNK, seq_len)
    n_c = min(2, seq_len // c)
    t_rows = c * n_c
    tables = _ret_tables(c)
    col = lambda g: pl.BlockSpec((None, t_rows, GROUP_W), lambda bi, ti, g=g: (bi, ti, g))
    state = pl.BlockSpec((None, HEADS, RET_DK, HEAD_W), lambda bi, ti: (bi, 0, 0, 0))
    return pl.pallas_call(
        functools.partial(_ret_kernel, c=c, n_c=n_c),
        grid=(b, seq_len // t_rows),
        in_specs=[col(0), col(1), col(2), col(3), state,
                  _resident((HEADS, c, c)), _resident((HEADS, c, HEAD_W)), _resident((HEADS, c, HEAD_W)),
                  _resident((HEADS, 8, HEAD_W)), _resident((1, GROUP_W))],
        out_specs=[pl.BlockSpec((None, t_rows, GROUP_W), lambda bi, ti: (bi, ti, 0)), state],
        out_shape=[jax.ShapeDtypeStruct((b, seq_len, GROUP_W), BF16),
                   jax.ShapeDtypeStruct((b, HEADS, RET_DK, HEAD_W), F32)],
        scratch_shapes=[pltpu.VMEM((HEADS, RET_DK, HEAD_W), F32)],
        compiler_params=_params("parallel", "arbitrary"),
        name="retention",
    )(p, p, p, p, s0, *tables, gn)


def _lambda(lq1_ref, lk1_ref, lq2_ref, lk2_ref, lam_init):
    return (jnp.exp(jnp.sum(lq1_ref[...] * lk1_ref[...], axis=-1, keepdims=True))
            - jnp.exp(jnp.sum(lq2_ref[...] * lk2_ref[...], axis=-1, keepdims=True)) + lam_init)


def _split_components(q):
    lane = lax.broadcasted_iota(jnp.int32, q.shape, 1)
    zero = jnp.zeros_like(q)
    return jnp.where(lane < DIFF_DK, q, zero), jnp.where(lane >= DIFF_DK, q, zero)


def _flash_update(s, v, mask, m_ref, l_ref, acc_ref):
    if mask is not None:
        s = jnp.where(mask, s, MASK_VALUE)
    pad = -s.shape[1] % HEAD_W
    if pad:
        s = jnp.concatenate([s, jnp.full((s.shape[0], pad), MASK_VALUE, F32)], axis=1)
        v = jnp.concatenate([v, jnp.zeros((pad, v.shape[1]), v.dtype)], axis=0)
    blocks =[s[:, b * HEAD_W:(b + 1) * HEAD_W] for b in range(s.shape[1] // HEAD_W)]
    m_prev = m_ref[...]
    m_new = jnp.maximum(m_prev, jnp.max(functools.reduce(jnp.maximum, blocks), axis=-1, keepdims=True))
    alpha = jnp.exp2(m_prev - m_new)
    probs = [jnp.exp2(blk - m_new) for blk in blocks]
    l_ref[...] = alpha * l_ref[...] + functools.reduce(jnp.add, probs)
    p = jnp.concatenate([x.astype(BF16) for x in probs], axis=1)
    acc_ref[...] = alpha * acc_ref[...] + jnp.dot(p, v, preferred_element_type=F32)
    m_ref[...] = m_new


def _flash_init(m_ref, l_ref, acc_ref):
    m_ref[...] = jnp.full(m_ref.shape, MASK_VALUE, F32)
    l_ref[...] = jnp.zeros(l_ref.shape, F32)
    acc_ref[...] = jnp.zeros(acc_ref.shape, F32)


def _flash_update_t(s_t, s_max, vt_ones, mask, m_ref, acc_ref):
    if mask is not None:
        s_t = jnp.where(mask, s_t, MASK_VALUE)
        s_max = jnp.max(s_t, axis=0, keepdims=True)
    m_prev = m_ref[...]
    m_new = jnp.maximum(m_prev, s_max)
    alpha = jnp.exp2(m_prev - m_new)
    p_t = jnp.exp2(s_t - m_new).astype(BF16)
    acc_ref[...] = alpha * acc_ref[...] + jnp.dot(vt_ones, p_t, preferred_element_type=F32)
    m_ref[...] = m_new


def _attn_prompt_kernel(lq1_ref, lk1_ref, lq2_ref, lk2_ref, gn_ref, q_ref, k_ref, v_ref, o_ref,
                        *scratch, tile, lam_init):
    i = pl.program_id(1)
    n_sub = tile // ATTN_Q_SUB
    chains = [(qs, c) for qs in range(n_sub) for c in range(2)]
    stats = {ch: scratch[2 * n:2 * n + 2] for n, ch in enumerate(chains)}
    base = 2 * len(chains)
    slots = [(scratch[base + n], scratch[base + 5 + n]) for n in range(4)]
    slot_pairs = (slots[0:2], slots[2:4])
    qt_scr = scratch[base + 4]
    lam = _lambda(lq1_ref, lk1_ref, lq2_ref, lk2_ref, lam_init)
    key_chunk = lax.broadcasted_iota(jnp.int32, (tile, tile), 0) // CHUNK
    query_chunk = lax.broadcasted_iota(jnp.int32, (tile, tile), 1) // CHUNK
    diag_mask = key_chunk <= query_chunk

    def head_cols(h):
        return slice(h * HEAD_W, (h + 1) * HEAD_W)

    def key_rows(j):
        return pl.ds(pl.multiple_of(j * tile, tile), tile)

    def scores_into(h, slot, j):
        k = k_ref[key_rows(j), head_cols(h)]
        for n, (qs, c) in enumerate(chains):
            queries = slice(qs * ATTN_Q_SUB, (qs + 1) * ATTN_Q_SUB)
            s_t = jnp.dot(k, qt_scr[h % 2, c, :, queries], preferred_element_type=F32)
            slot[0][n] = s_t
            slot[1][n] = jnp.max(s_t, axis=0, keepdims=True)

    def accumulate(h, slot, j, mask):
        v_t = v_ref[key_rows(j), head_cols(h)].T
        vt_ones = jnp.concatenate([v_t, jnp.ones((SUM_ROWS, tile), BF16)], axis=0)
        for n, (qs, c) in enumerate(chains):
            sub_mask = None if mask is None else mask[:, qs * ATTN_Q_SUB:(qs + 1) * ATTN_Q_SUB]
            _flash_update_t(slot[0][n], slot[1][n], vt_ones, sub_mask, *stats[(qs, c)])

    def start_head(h):
        if h < HEADS:
            for c, comp in enumerate(_split_components(q_ref[:, head_cols(h)])):
                qt_scr[h % 2, c] = comp.T
            scores_into(h, slot_pairs[h % 2][0], 0)

    def finish_head(h):
        for qs in range(n_sub):
            (_, acc1_ref), (_, acc2_ref) = stats[(qs, 0)], stats[(qs, 1)]
            o_t = (acc1_ref[:HEAD_W] / acc1_ref[HEAD_W:HEAD_W + 1]
                   - lam * (acc2_ref[:HEAD_W] / acc2_ref[HEAD_W:HEAD_W + 1]))
            ms = jnp.mean(o_t * o_t, axis=0, keepdims=True)
            o_t = o_t * lax.rsqrt(ms + EPS) * gn_ref[...] * (1.0 - lam_init)
            o_ref[qs * ATTN_Q_SUB:(qs + 1) * ATTN_Q_SUB, head_cols(h)] = o_t.T.astype(BF16)

    start_head(0)
    for h in range(HEADS):
        slot_a, slot_b = slot_pairs[h % 2]
        for m_ref, acc_ref in stats.values():
            m_ref[...] = jnp.full(m_ref.shape, MASK_VALUE, F32)
            acc_ref[...] = jnp.zeros(acc_ref.shape, F32)

        def body(jj, carry, h=h, slot_a=slot_a, slot_b=slot_b):
            scores_into(h, slot_b, 2 * jj + 1)
            accumulate(h, slot_a, 2 * jj, None)
            scores_into(h, slot_a, 2 * jj + 2)
            accumulate(h, slot_b, 2 * jj + 1, None)
            return carry

        lax.fori_loop(0, i // 2, body, 0)

        @pl.when(i % 2 == 0)
        def _(h=h, slot_a=slot_a):
            start_head(h + 1)
            accumulate(h, slot_a, i, diag_mask)

        @pl.when(i % 2 == 1)
        def _(h=h, slot_a=slot_a, slot_b=slot_b):
            scores_into(h, slot_b, i)
            accumulate(h, slot_a, i - 1, None)
            start_head(h + 1)
            accumulate(h, slot_b, i, diag_mask)

        finish_head(h)


def _attn_prompt(p, lq1, lk1, lq2, lk2, gn, lam_init):
    b, seq_len, _ = p.shape
    tile = ATTN_TILE
    small = lambda n: pl.BlockSpec((1, n), lambda bi, qi: (0, 0))
    kv = lambda g: pl.BlockSpec((None, seq_len, GROUP_W), lambda bi, qi, g=g: (bi, 0, g))
    return pl.pallas_call(
        functools.partial(_attn_prompt_kernel, tile=tile, lam_init=lam_init),
        grid=(b, seq_len // tile),
        in_specs=[small(DIFF_DK)] * 4 + [pl.BlockSpec((HEAD_W, 1), lambda bi, qi: (0, 0)),
                  pl.BlockSpec((None, tile, GROUP_W), lambda bi, qi: (bi, qi, 4)), kv(5), kv(6)],
        out_specs=pl.BlockSpec((None, tile, GROUP_W), lambda bi, qi: (bi, qi, 0)),
        out_shape=jax.ShapeDtypeStruct((b, seq_len, GROUP_W), BF16),
        scratch_shapes=[pltpu.VMEM((1, ATTN_Q_SUB), F32),
                        pltpu.VMEM((HEAD_W + SUM_ROWS, ATTN_Q_SUB), F32)] * (2 * tile // ATTN_Q_SUB)
                       + [pltpu.VMEM((2 * tile // ATTN_Q_SUB, tile, ATTN_Q_SUB), F32)] * 4
                       + [pltpu.VMEM((2, 2, HEAD_W, tile), BF16)]
                       + [pltpu.VMEM((2 * tile // ATTN_Q_SUB, 1, ATTN_Q_SUB), F32)] * 4,
        compiler_params=_params("parallel", "arbitrary"),
        name="attn_prompt",
    )(lq1, lk1, lq2, lk2, gn.reshape(HEAD_W, 1), p, p, p)


def _attn_sample_kernel(lq1_ref, lk1_ref, lq2_ref, lk2_ref, gn_ref, q_ref, kp_ref, vp_ref, kn_ref, vn_ref,
                        o_ref, m_scr, l_scr, acc_scr, *, n_past, lam_init):
    j = pl.program_id(1)

    @pl.when(j == 0)
    def _():
        _flash_init(m_scr, l_scr, acc_scr)

    n_q = q_ref.shape[0]

    def update(scores_of, values):
        for h in range(HEADS):
            cols = slice(h * HEAD_W, (h + 1) * HEAD_W)
            q_both = jnp.concatenate(_split_components(q_ref[:, cols]), axis=0)
            _flash_update(scores_of(q_both, cols), values[:, cols].astype(BF16), None,
                          m_scr.at[h], l_scr.at[h], acc_scr.at[h])

    @pl.when(j < n_past)
    def _():
        update(lambda q, feats: jnp.dot(q, kp_ref[feats, :].astype(BF16), preferred_element_type=F32),
               vp_ref[...].reshape(vp_ref.shape[0], GROUP_W))

    @pl.when(j == n_past)
    def _():
        update(lambda q, feats: lax.dot_general(q, kn_ref[:, feats], _NT, preferred_element_type=F32),
               vn_ref[...])
        lam = _lambda(lq1_ref, lk1_ref, lq2_ref, lk2_ref, lam_init)
        for h in range(HEADS):
            l = jnp.sum(l_scr[h], axis=-1, keepdims=True)
            o_both = acc_scr[h] / l
            o = o_both[:n_q] - lam * o_both[n_q:]
            o_ref[:, h * HEAD_W:(h + 1) * HEAD_W] = (_rms(o, gn_ref[...]) * (1.0 - lam_init)).astype(BF16)


def _attn_sample(p, k_past, v_past, lq1, lk1, lq2, lk2, gn, lam_init):
    b, seq_len, _ = p.shape
    past_len = k_past.shape[1]
    tk = min(PAST_TILE, past_len)
    n_past = past_len // tk
    k_past = jnp.transpose(k_past, (0, 2, 3, 4, 1)).reshape(b, GROUP_W, past_len)
    small = lambda n: pl.BlockSpec((1, n), lambda bi, ji: (0, 0))
    new = lambda g: pl.BlockSpec((None, seq_len, GROUP_W), lambda bi, ji, g=g: (bi, 0, g))
    past_k = pl.BlockSpec((None, GROUP_W, tk), lambda bi, ji: (bi, 0, jnp.minimum(ji, n_past - 1)))
    past_v = pl.BlockSpec((None, tk, HEADS, HEAD_W), lambda bi, ji: (bi, jnp.minimum(ji, n_past - 1), 0, 0))
    return pl.pallas_call(
        functools.partial(_attn_sample_kernel, n_past=n_past, lam_init=lam_init),
        grid=(b, n_past + 1),
        in_specs=[small(DIFF_DK)] * 4 + [small(HEAD_W), new(4), past_k, past_v, new(5), new(6)],
        out_specs=pl.BlockSpec((None, seq_len, GROUP_W), lambda bi, ji: (bi, 0, 0)),
        out_shape=jax.ShapeDtypeStruct((b, seq_len, GROUP_W), BF16),
        scratch_shapes=[pltpu.VMEM((HEADS, 2 * seq_len, HEAD_W), F32)] * 3,
        compiler_params=_params("parallel", "arbitrary"),
        name="attn_sample",
    )(lq1, lk1, lq2, lk2, gn, p, k_past, v_past, p, p)


def _layer(x, ret_s0, k_past, v_past, w, lam_init):
    (f1_pre, f1_wg, f1_wu, f1_wd, f1_post, mix_pre, w_in, ret_g, lq1, lk1, lq2, lk2, diff_g, w_out,
     mix_post, f2_pre, f2_wg, f2_wu, f2_wd, f2_post) = w
    b, seq_len, _ = x.shape
    past_len = 0 if k_past is None else k_past.shape[1]
    x = x.reshape(b * seq_len, D_MODEL)
    x1 = _ffn(x, f1_pre, f1_wg, f1_wu, f1_wd, f1_post)
    p, k_new, v_new = _mix_in(x1, mix_pre, w_in, seq_len, past_len)
    ret_y, s_new = _retention(p, ret_s0, ret_g, seq_len)
    if k_past is None:
        diff_y = _attn_prompt(p, lq1, lk1, lq2, lk2, diff_g, lam_init)
    else:
        diff_y = _attn_sample(p, k_past, v_past, lq1, lk1, lq2, lk2, diff_g, lam_init)
    x3 = _out_ffn(x1, ret_y.reshape(b * seq_len, GROUP_W), diff_y.reshape(b * seq_len, GROUP_W),
                  w_out, mix_post, f2_pre, f2_wg, f2_wu, f2_wd, f2_post)
    return x3.reshape(b, seq_len, D_MODEL), s_new, k_new, v_new


def kernel(x_prompt, x_sample, state_ret, cache_diff_k, cache_diff_v, ffn1_pre_g, ffn1_w_gate, ffn1_w_up, ffn1_w_down, ffn1_post_g, mix_pre_g, w_in, ret_norm_g, diff_lq1, diff_lk1, diff_lq2, diff_lk2, diff_norm_g, w_out, mix_post_g, ffn2_pre_g, ffn2_w_gate, ffn2_w_up, ffn2_w_down, ffn2_post_g):
    depth = w_in.shape[0]
    xp, xs = x_prompt, x_sample
    outs = [[] for _ in range(6)]
    for li in range(depth):
        lam_init = 0.8 - 0.6 * math.exp(-0.3 * li)
        row = lambda a: a[li].reshape(1, -1)
        mat = lambda a: a[li].astype(BF16)
        w = (row(ffn1_pre_g), mat(ffn1_w_gate), mat(ffn1_w_up), mat(ffn1_w_down), row(ffn1_post_g),
             row(mix_pre_g), mat(w_in), row(ret_norm_g), row(diff_lq1), row(diff_lk1), row(diff_lq2),
             row(diff_lk2), row(diff_norm_g), mat(w_out), row(mix_post_g),
             row(ffn2_pre_g), mat(ffn2_w_gate), mat(ffn2_w_up), mat(ffn2_w_down), row(ffn2_post_g))
        s0 = jnp.zeros((xp.shape[0], HEADS, RET_DK, HEAD_W), F32)
        xp, s_p, k_p, v_p = _layer(xp, s0, None, None, w, lam_init)
        xs, s_s, k_s, v_s = _layer(xs, state_ret[li], cache_diff_k[li], cache_diff_v[li], w, lam_init)
        for acc, val in zip(outs, (s_p, k_p, v_p, s_s, k_s, v_s)):
            acc.append(val)
    return (xp, xs) + tuple(jnp.stack(o) for o in outs)
```

```python
import functools
import math

import jax
import jax.numpy as jnp
from jax import lax
from jax.experimental import pallas as pl
from jax.experimental.pallas import tpu as pltpu

D_MODEL = 1024
D_FF = 2816
CHUNK = 64
HEADS = 4
HEAD_W = 128
GROUP_W = HEADS * HEAD_W
N_GROUPS = 7
IN_COLS = N_GROUPS * GROUP_W
RET_DK = 128
DIFF_DK = 64
ROT_DIM = DIFF_DK // 4
ROPE_THETA = 500000.0
RET_THETA = 10000.0
EPS = 1e-6
MASK_VALUE = -1e30
LOG2E = math.log2(math.e)

ROW_TILE = 1024
FFN_ROW_TILE = 512
FFN_GROUP_ROWS = 256
RET_CHUNK = 256
ATTN_TILE = 512
SUM_ROWS = 16
ATTN_Q_SUB = 256
PAST_TILE = 1024
VMEM_LIMIT = 56 * 1024 * 1024

F32 = jnp.float32
BF16 = jnp.bfloat16
_NT = (((1,), (1,)), ((), ()))
_TN = (((0,), (0,)), ((), ()))


def _rms(x, g):
    return x * lax.rsqrt(jnp.mean(x * x, axis=-1, keepdims=True) + EPS) * g


def _silu(x):
    return x * (1.0 / (1.0 + jnp.exp(-x)))


def _resident(shape):
    nd = len(shape)
    return pl.BlockSpec(shape, lambda *_: (0,) * nd, pipeline_mode=pl.Buffered(1))


def _params(*sem):
    return pltpu.CompilerParams(dimension_semantics=sem, vmem_limit_bytes=VMEM_LIMIT)


def _ffn_pipeline(n_rows, x_of, o_ref, pre_g, wg_ref, wu_ref, wd_ref, post_g):
    n = pl.cdiv(n_rows, FFN_GROUP_ROWS)
    groups = [slice(g * FFN_GROUP_ROWS, min((g + 1) * FFN_GROUP_ROWS, n_rows)) for g in range(n)]

    def gate_up(rows):
        xg = x_of(rows)
        h = _rms(xg, pre_g).astype(BF16)
        return (xg, jnp.dot(h, wg_ref[...], preferred_element_type=F32),
                jnp.dot(h, wu_ref[...], preferred_element_type=F32))

    def down(rows, xg, gate, up):
        act = (_silu(gate) * up).astype(BF16)
        o_ref[rows, :] = xg + 0.5 * _rms(jnp.dot(act, wd_ref[...], preferred_element_type=F32), post_g)

    pending = gate_up(groups[0])
    for g in range(n):
        following = gate_up(groups[g + 1]) if g + 1 < n else None
        down(groups[g], *pending)
        pending = following


def _ffn_kernel(x_ref, pre_ref, wg_ref, wu_ref, wd_ref, post_ref, o_ref):
    _ffn_pipeline(x_ref.shape[0], lambda rows: x_ref[rows, :], o_ref,
                  pre_ref[...], wg_ref, wu_ref, wd_ref, post_ref[...])


def _ffn(x, pre_g, wg, wu, wd, post_g):
    m = x.shape[0]
    tm = min(FFN_ROW_TILE, m)
    row = pl.BlockSpec((tm, D_MODEL), lambda i: (i, 0))
    return pl.pallas_call(
        _ffn_kernel,
        grid=(m // tm,),
        in_specs=[row, _resident((1, D_MODEL)), _resident((D_MODEL, D_FF)), _resident((D_MODEL, D_FF)),
                  _resident((D_FF, D_MODEL)), _resident((1, D_MODEL))],
        out_specs=row,
        out_shape=jax.ShapeDtypeStruct((m, D_MODEL), F32),
        compiler_params=_params("parallel"),
        name="ffn",
    )(x, pre_g, wg, wu, wd, post_g)


def _out_ffn_kernel(x_ref, ry_ref, dy_ref, wo_ref, mixpost_ref, pre_ref, wg_ref, wu_ref, wd_ref, post_ref,
                    o_ref):
    y = (jnp.dot(ry_ref[...], wo_ref[:GROUP_W, :], preferred_element_type=F32)
         + jnp.dot(dy_ref[...], wo_ref[GROUP_W:, :], preferred_element_type=F32))
    x2 = x_ref[...] + _rms(y, mixpost_ref[...])
    _ffn_pipeline(x_ref.shape[0], lambda rows: x2[rows, :], o_ref,
                  pre_ref[...], wg_ref, wu_ref, wd_ref, post_ref[...])


def _out_ffn(x, ret_y, diff_y, wo, mixpost_g, pre_g, wg, wu, wd, post_g):
    m = x.shape[0]
    tm = min(FFN_ROW_TILE, m)
    row = pl.BlockSpec((tm, D_MODEL), lambda i: (i, 0))
    half = pl.BlockSpec((tm, GROUP_W), lambda i: (i, 0))
    return pl.pallas_call(
        _out_ffn_kernel,
        grid=(m // tm,),
        in_specs=[row, half, half, _resident((D_MODEL, D_MODEL)), _resident((1, D_MODEL)),
                  _resident((1, D_MODEL)), _resident((D_MODEL, D_FF)), _resident((D_MODEL, D_FF)),
                  _resident((D_FF, D_MODEL)), _resident((1, D_MODEL))],
        out_specs=row,
        out_shape=jax.ShapeDtypeStruct((m, D_MODEL), F32),
        compiler_params=_params("parallel"),
        name="out_ffn",
    )(x, ret_y, diff_y, wo, mixpost_g, pre_g, wg, wu, wd, post_g)


def _rope_tables(pos):
    pos = pos.astype(F32)[:, None]
    half = RET_DK // 2
    inv = jnp.power(jnp.float32(RET_THETA), -jnp.arange(half, dtype=F32) * (2.0 / RET_DK))
    ang = pos * inv[None, :]
    cos, sin = jnp.cos(ang), jnp.sin(ang)
    cr = jnp.concatenate([cos, cos], axis=1)
    sr = jnp.concatenate([-sin, sin], axis=1)
    half = ROT_DIM // 2
    inv = jnp.power(jnp.float32(ROPE_THETA), -jnp.arange(half, dtype=F32) * (2.0 / ROT_DIM))
    ang = pos * inv[None, :]
    cos, sin = jnp.cos(ang), jnp.sin(ang)
    n = pos.shape[0]
    ones = jnp.ones((n, DIFF_DK - ROT_DIM), F32)
    zeros = jnp.zeros((n, DIFF_DK - ROT_DIM), F32)
    z8 = jnp.zeros((n, half), F32)
    c_comp = jnp.concatenate([cos, cos, ones], axis=1)
    lo_comp = jnp.concatenate([-sin, z8, zeros], axis=1)
    hi_comp = jnp.concatenate([z8, sin, zeros], axis=1)
    cd = jnp.concatenate([c_comp, c_comp], axis=1)
    s_lo = jnp.concatenate([lo_comp, lo_comp], axis=1)
    s_hi = jnp.concatenate([hi_comp, hi_comp], axis=1)
    return cr, sr, cd, s_lo, s_hi


def _mix_in_kernel(x_ref, g_ref, w_ref, cr_ref, sr_ref, cd_ref, slo_ref, shi_ref, p_ref, kd_ref, vd_ref,
                   *, transpose_k):
    h = _rms(x_ref[...], g_ref[...]).astype(BF16)
    for grp in (6, 5, 4, 1, 0, 2, 3):
        pg = jnp.dot(h, w_ref[:, grp * GROUP_W:(grp + 1) * GROUP_W], preferred_element_type=F32)
        if grp == 6:
            vd_ref[...] = pg.reshape(pg.shape[0], HEADS, HEAD_W)
        for t in range(HEADS):
            cols = slice(t * HEAD_W, (t + 1) * HEAD_W)
            xt = pg[:, cols]
            if grp in (0, 1):
                xt = xt * cr_ref[...] + pltpu.roll(xt, RET_DK // 2, 1) * sr_ref[...]
                if grp == 1:
                    xt = xt * (RET_DK ** -0.5)
            elif grp in (4, 5):
                xt = (xt * cd_ref[...] + pltpu.roll(xt, HEAD_W - ROT_DIM // 2, 1) * slo_ref[...]
                      + pltpu.roll(xt, ROT_DIM // 2, 1) * shi_ref[...])
                if grp == 5 and transpose_k:
                    kd_ref[cols, :] = xt.T
                elif grp == 5:
                    kd_ref[:, cols] = xt
                else:
                    xt = xt * (DIFF_DK ** -0.5 * LOG2E)
            p_ref[:, grp * GROUP_W + t * HEAD_W:grp * GROUP_W + (t + 1) * HEAD_W] = xt.astype(BF16)


def _mix_in(x, g, w, seq_len, pos0):
    m = x.shape[0]
    tm = min(ROW_TILE, m)
    n_tab = max(seq_len, tm)
    pos = pos0 + (jnp.arange(n_tab, dtype=jnp.int32) % seq_len)
    tables = _rope_tables(pos)
    n_tab_blocks = n_tab // tm
    row = pl.BlockSpec((tm, D_MODEL), lambda i: (i, 0))
    tab = pl.BlockSpec((tm, HEAD_W), lambda i: (i % n_tab_blocks, 0))
    transpose_k = seq_len % tm == 0
    if transpose_k:
        n_seq_blocks = seq_len // tm
        k_spec = pl.BlockSpec((None, GROUP_W, tm), lambda i: (i // n_seq_blocks, 0, i % n_seq_blocks))
        k_shape = jax.ShapeDtypeStruct((m // seq_len, GROUP_W, seq_len), F32)
    else:
        k_spec = pl.BlockSpec((tm, GROUP_W), lambda i: (i, 0))
        k_shape = jax.ShapeDtypeStruct((m, GROUP_W), F32)
    p, k_new, v_new = pl.pallas_call(
        functools.partial(_mix_in_kernel, transpose_k=transpose_k),
        grid=(m // tm,),
        in_specs=[row, _resident((1, D_MODEL)), _resident((D_MODEL, IN_COLS)), tab, tab, tab, tab, tab],
        out_specs=[pl.BlockSpec((tm, IN_COLS), lambda i: (i, 0)), k_spec,
                   pl.BlockSpec((tm, HEADS, HEAD_W), lambda i: (i, 0, 0))],
        out_shape=[jax.ShapeDtypeStruct((m, IN_COLS), BF16), k_shape,
                   jax.ShapeDtypeStruct((m, HEADS, HEAD_W), F32)],
        compiler_params=_params("parallel"),
        name="mix_in",
    )(x, g, w, *tables)
    b = m // seq_len
    if transpose_k:
        k_new = jnp.transpose(k_new.reshape(b, HEADS, 2, DIFF_DK, seq_len), (0, 4, 1, 2, 3))
    else:
        k_new = k_new.reshape(b, seq_len, HEADS, 2, DIFF_DK)
    return p.reshape(b, seq_len, IN_COLS), k_new, v_new.reshape(b, seq_len, HEADS, HEAD_W)


def _ret_tables(c):
    log_g = jnp.log(1.0 - jnp.power(2.0, -5.0 - jnp.arange(HEADS, dtype=F32)))
    idx = jnp.arange(c, dtype=F32)
    rel = idx[:, None] - idx[None, :]
    dmask = jnp.where(rel >= 0, jnp.exp(log_g[:, None, None] * jnp.maximum(rel, 0.0)), 0.0)
    q_dec = jnp.exp(log_g[:, None] * (idx + 1.0)[None, :])
    k_dec = jnp.exp(log_g[:, None] * (c - 1.0 - idx)[None, :])
    b_dec = jnp.exp(log_g * c)
    bcast = lambda a: jnp.broadcast_to(a[:, :, None], (HEADS, c, HEAD_W))
    return dmask, bcast(q_dec), bcast(k_dec), jnp.broadcast_to(b_dec[:, None, None], (HEADS, 8, HEAD_W))


def _ret_kernel(q_ref, k_ref, v_ref, g_ref, s0_ref, dmask_ref, qdec_ref, kdec_ref, bdec_ref, gn_ref,
                y_ref, snew_ref, s_scr, *, c, n_c):
    t = pl.program_id(1)

    @pl.when(t == 0)
    def _():
        s_scr[...] = s0_ref[...]

    for h in range(HEADS):
        cols = slice(h * HEAD_W, (h + 1) * HEAD_W)
        state = s_scr[h]
        b_dec = bdec_ref[h][0:1, :]
        for ci in range(n_c):
            rows = slice(ci * c, (ci + 1) * c)
            q = q_ref[rows, cols]
            k = k_ref[rows, cols]
            v = v_ref[rows, cols]
            scores = lax.dot_general(q, k, _NT, preferred_element_type=F32) * dmask_ref[h]
            o = (jnp.dot(scores.astype(BF16), v, preferred_element_type=F32)
                 + qdec_ref[h] * jnp.dot(q, state.astype(BF16), preferred_element_type=F32))
            k_scaled = (k.astype(F32) * kdec_ref[h]).astype(BF16)
            state = b_dec * state + lax.dot_general(k_scaled, v, _TN, preferred_element_type=F32)
            mu = jnp.mean(o, axis=-1, keepdims=True)
            d = o - mu
            var = jnp.mean(d * d, axis=-1, keepdims=True)
            normed = d * lax.rsqrt(var + EPS) * gn_ref[:, cols]
            y_ref[rows, cols] = (_silu(g_ref[rows, cols].astype(F32)) * normed).astype(BF16)
        s_scr[h] = state

    @pl.when(t == pl.num_programs(1) - 1)
    def _():
        snew_ref[...] = s_scr[...]


def _retention(p, s0, gn, seq_len):
    b = p.shape[0]
    c = min(RET_CHUNK, seq_len)
    n_c = min(2, seq_len // c)
    t_rows = c * n_c
    tables = _ret_tables(c)
    col = lambda g: pl.BlockSpec((None, t_rows, GROUP_W), lambda bi, ti, g=g: (bi, ti, g))
    state = pl.BlockSpec((None, HEADS, RET_DK, HEAD_W), lambda bi, ti: (bi, 0, 0, 0))
    return pl.pallas_call(
        functools.partial(_ret_kernel, c=c, n_c=n_c),
        grid=(b, seq_len // t_rows),
        in_specs=[col(0), col(1), col(2), col(3), state,
                  _resident((HEADS, c, c)), _resident((HEADS, c, HEAD_W)), _resident((HEADS, c, HEAD_W)),
                  _resident((HEADS, 8, HEAD_W)), _resident((1, GROUP_W))],
        out_specs=[pl.BlockSpec((None, t_rows, GROUP_W), lambda bi, ti: (bi, ti, 0)), state],
        out_shape=[jax.ShapeDtypeStruct((b, seq_len, GROUP_W), BF16),
                   jax.ShapeDtypeStruct((b, HEADS, RET_DK, HEAD_W), F32)],
        scratch_shapes=[pltpu.VMEM((HEADS, RET_DK, HEAD_W), F32)],
        compiler_params=_params("parallel", "arbitrary"),
        name="retention",
    )(p, p, p, p, s0, *tables, gn)


def _lambda(lq1_ref, lk1_ref, lq2_ref, lk2_ref, lam_init):
    return (jnp.exp(jnp.sum(lq1_ref[...] * lk1_ref[...], axis=-1, keepdims=True))
            - jnp.exp(jnp.sum(lq2_ref[...] * lk2_ref[...], axis=-1, keepdims=True)) + lam_init)


def _split_components(q):
    lane = lax.broadcasted_iota(jnp.int32, q.shape, 1)
    zero = jnp.zeros_like(q)
    return jnp.where(lane < DIFF_DK, q, zero), jnp.where(lane >= DIFF_DK, q, zero)


def _flash_update(s, v, mask, m_ref, l_ref, acc_ref):
    if mask is not None:
        s = jnp.where(mask, s, MASK_VALUE)
    pad = -s.shape[1] % HEAD_W
    if pad:
        s = jnp.concatenate([s, jnp.full((s.shape[0], pad), MASK_VALUE, F32)], axis=1)
        v = jnp.concatenate([v, jnp.zeros((pad, v.shape[1]), v.dtype)], axis=0)
    blocks =[s[:, b * HEAD_W:(b + 1) * HEAD_W] for b in range(s.shape[1] // HEAD_W)]
    m_prev = m_ref[...]
    m_new = jnp.maximum(m_prev, jnp.max(functools.reduce(jnp.maximum, blocks), axis=-1, keepdims=True))
    alpha = jnp.exp2(m_prev - m_new)
    probs = [jnp.exp2(blk - m_new) for blk in blocks]
    l_ref[...] = alpha * l_ref[...] + functools.reduce(jnp.add, probs)
    p = jnp.concatenate([x.astype(BF16) for x in probs], axis=1)
    acc_ref[...] = alpha * acc_ref[...] + jnp.dot(p, v, preferred_element_type=F32)
    m_ref[...] = m_new


def _flash_init(m_ref, l_ref, acc_ref):
    m_ref[...] = jnp.full(m_ref.shape, MASK_VALUE, F32)
    l_ref[...] = jnp.zeros(l_ref.shape, F32)
    acc_ref[...] = jnp.zeros(acc_ref.shape, F32)


def _flash_update_t(s_t, s_max, vt_ones, mask, m_ref, acc_ref):
    if mask is not None:
        s_t = jnp.where(mask, s_t, MASK_VALUE)
        s_max = jnp.max(s_t, axis=0, keepdims=True)
    m_prev = m_ref[...]
    m_new = jnp.maximum(m_prev, s_max)
    alpha = jnp.exp2(m_prev - m_new)
    p_t = jnp.exp2(s_t - m_new).astype(BF16)
    acc_ref[...] = alpha * acc_ref[...] + jnp.dot(vt_ones, p_t, preferred_element_type=F32)
    m_ref[...] = m_new


def _attn_prompt_kernel(lq1_ref, lk1_ref, lq2_ref, lk2_ref, gn_ref, q_ref, k_ref, v_ref, o_ref,
                        *scratch, tile, lam_init):
    i = pl.program_id(1)
    n_sub = tile // ATTN_Q_SUB
    chains = [(qs, c) for qs in range(n_sub) for c in range(2)]
    stats = [{ch: scratch[2 * (par * len(chains) + n):2 * (par * len(chains) + n) + 2]
              for n, ch in enumerate(chains)} for par in range(2)]
    base = 4 * len(chains)
    slots = [(scratch[base + n], scratch[base + 5 + n]) for n in range(4)]
    slot_pairs = (slots[0:2], slots[2:4])
    qt_scr = scratch[base + 4]
    lam = _lambda(lq1_ref, lk1_ref, lq2_ref, lk2_ref, lam_init)
    key_chunk = lax.broadcasted_iota(jnp.int32, (tile, tile), 0) // CHUNK
    query_chunk = lax.broadcasted_iota(jnp.int32, (tile, tile), 1) // CHUNK
    diag_mask = key_chunk <= query_chunk

    def head_cols(h):
        return slice(h * HEAD_W, (h + 1) * HEAD_W)

    def key_rows(j):
        return pl.ds(pl.multiple_of(j * tile, tile), tile)

    def scores_into(h, slot, j):
        k = k_ref[key_rows(j), head_cols(h)]
        for n, (qs, c) in enumerate(chains):
            queries = slice(qs * ATTN_Q_SUB, (qs + 1) * ATTN_Q_SUB)
            s_t = jnp.dot(k, qt_scr[h % 2, c, :, queries], preferred_element_type=F32)
            slot[0][n] = s_t
            slot[1][n] = jnp.max(s_t, axis=0, keepdims=True)

    def accumulate(h, slot, j, mask):
        v_t = v_ref[key_rows(j), head_cols(h)].T
        vt_ones = jnp.concatenate([v_t, jnp.ones((SUM_ROWS, tile), BF16)], axis=0)
        for n, (qs, c) in enumerate(chains):
            sub_mask = None if mask is None else mask[:, qs * ATTN_Q_SUB:(qs + 1) * ATTN_Q_SUB]
            _flash_update_t(slot[0][n], slot[1][n], vt_ones, sub_mask, *stats[h % 2][(qs, c)])

    def start_head(h):
        if h < HEADS:
            for m_ref, acc_ref in stats[h % 2].values():
                m_ref[...] = jnp.full(m_ref.shape, MASK_VALUE, F32)
                acc_ref[...] = jnp.zeros(acc_ref.shape, F32)
            for c, comp in enumerate(_split_components(q_ref[:, head_cols(h)])):
                qt_scr[h % 2, c] = comp.T
            scores_into(h, slot_pairs[h % 2][0], 0)

    def finish_head(h):
        if h < 0:
            return
        for qs in range(n_sub):
            (_, acc1_ref), (_, acc2_ref) = stats[h % 2][(qs, 0)], stats[h % 2][(qs, 1)]
            o_t = (acc1_ref[:HEAD_W] / acc1_ref[HEAD_W:HEAD_W + 1]
                   - lam * (acc2_ref[:HEAD_W] / acc2_ref[HEAD_W:HEAD_W + 1]))
            ms = jnp.mean(o_t * o_t, axis=0, keepdims=True)
            o_t = o_t * lax.rsqrt(ms + EPS) * gn_ref[...] * (1.0 - lam_init)
            o_ref[qs * ATTN_Q_SUB:(qs + 1) * ATTN_Q_SUB, head_cols(h)] = o_t.T.astype(BF16)

    start_head(0)
    for h in range(HEADS):
        slot_a, slot_b = slot_pairs[h % 2]

        def body(jj, carry, h=h, slot_a=slot_a, slot_b=slot_b):
            scores_into(h, slot_b, 2 * jj + 1)
            accumulate(h, slot_a, 2 * jj, None)
            scores_into(h, slot_a, 2 * jj + 2)
            accumulate(h, slot_b, 2 * jj + 1, None)
            return carry

        lax.fori_loop(0, i // 2, body, 0)

        @pl.when(i % 2 == 0)
        def _(h=h, slot_a=slot_a):
            finish_head(h - 1)
            start_head(h + 1)
            accumulate(h, slot_a, i, diag_mask)

        @pl.when(i % 2 == 1)
        def _(h=h, slot_a=slot_a, slot_b=slot_b):
            scores_into(h, slot_b, i)
            accumulate(h, slot_a, i - 1, None)
            finish_head(h - 1)
            start_head(h + 1)
            accumulate(h, slot_b, i, diag_mask)

    finish_head(HEADS - 1)


def _attn_prompt(p, lq1, lk1, lq2, lk2, gn, lam_init):
    b, seq_len, _ = p.shape
    tile = ATTN_TILE
    small = lambda n: pl.BlockSpec((1, n), lambda bi, qi: (0, 0))
    kv = lambda g: pl.BlockSpec((None, seq_len, GROUP_W), lambda bi, qi, g=g: (bi, 0, g))
    return pl.pallas_call(
        functools.partial(_attn_prompt_kernel, tile=tile, lam_init=lam_init),
        grid=(b, seq_len // tile),
        in_specs=[small(DIFF_DK)] * 4 + [pl.BlockSpec((HEAD_W, 1), lambda bi, qi: (0, 0)),
                  pl.BlockSpec((None, tile, GROUP_W), lambda bi, qi: (bi, qi, 4)), kv(5), kv(6)],
        out_specs=pl.BlockSpec((None, tile, GROUP_W), lambda bi, qi: (bi, qi, 0)),
        out_shape=jax.ShapeDtypeStruct((b, seq_len, GROUP_W), BF16),
        scratch_shapes=[pltpu.VMEM((1, ATTN_Q_SUB), F32),
                        pltpu.VMEM((HEAD_W + SUM_ROWS, ATTN_Q_SUB), F32)] * (4 * tile // ATTN_Q_SUB)
                       + [pltpu.VMEM((2 * tile // ATTN_Q_SUB, tile, ATTN_Q_SUB), F32)] * 4
                       + [pltpu.VMEM((2, 2, HEAD_W, tile), BF16)]
                       + [pltpu.VMEM((2 * tile // ATTN_Q_SUB, 1, ATTN_Q_SUB), F32)] * 4,
        compiler_params=_params("parallel", "arbitrary"),
        name="attn_prompt",
    )(lq1, lk1, lq2, lk2, gn.reshape(HEAD_W, 1), p, p, p)


def _attn_sample_kernel(lq1_ref, lk1_ref, lq2_ref, lk2_ref, gn_ref, q_ref, kp_ref, vp_ref, kn_ref, vn_ref,
                        o_ref, m_scr, l_scr, acc_scr, *, n_past, lam_init):
    j = pl.program_id(1)

    @pl.when(j == 0)
    def _():
        _flash_init(m_scr, l_scr, acc_scr)

    n_q = q_ref.shape[0]

    def update(scores_of, values):
        for h in range(HEADS):
            cols = slice(h * HEAD_W, (h + 1) * HEAD_W)
            q_both = jnp.concatenate(_split_components(q_ref[:, cols]), axis=0)
            _flash_update(scores_of(q_both, cols), values[:, cols].astype(BF16), None,
                          m_scr.at[h], l_scr.at[h], acc_scr.at[h])

    @pl.when(j < n_past)
    def _():
        update(lambda q, feats: jnp.dot(q, kp_ref[feats, :].astype(BF16), preferred_element_type=F32),
               vp_ref[...].reshape(vp_ref.shape[0], GROUP_W))

    @pl.when(j == n_past)
    def _():
        update(lambda q, feats: lax.dot_general(q, kn_ref[:, feats], _NT, preferred_element_type=F32),
               vn_ref[...])
        lam = _lambda(lq1_ref, lk1_ref, lq2_ref, lk2_ref, lam_init)
        for h in range(HEADS):
            l = jnp.sum(l_scr[h], axis=-1, keepdims=True)
            o_both = acc_scr[h] / l
            o = o_both[:n_q] - lam * o_both[n_q:]
            o_ref[:, h * HEAD_W:(h + 1) * HEAD_W] = (_rms(o, gn_ref[...]) * (1.0 - lam_init)).astype(BF16)


def _attn_sample(p, k_past, v_past, lq1, lk1, lq2, lk2, gn, lam_init):
    b, seq_len, _ = p.shape
    past_len = k_past.shape[1]
    tk = min(PAST_TILE, past_len)
    n_past = past_len // tk
    k_past = jnp.transpose(k_past, (0, 2, 3, 4, 1)).reshape(b, GROUP_W, past_len)
    small = lambda n: pl.BlockSpec((1, n), lambda bi, ji: (0, 0))
    new = lambda g: pl.BlockSpec((None, seq_len, GROUP_W), lambda bi, ji, g=g: (bi, 0, g))
    past_k = pl.BlockSpec((None, GROUP_W, tk), lambda bi, ji: (bi, 0, jnp.minimum(ji, n_past - 1)))
    past_v = pl.BlockSpec((None, tk, HEADS, HEAD_W), lambda bi, ji: (bi, jnp.minimum(ji, n_past - 1), 0, 0))
    return pl.pallas_call(
        functools.partial(_attn_sample_kernel, n_past=n_past, lam_init=lam_init),
        grid=(b, n_past + 1),
        in_specs=[small(DIFF_DK)] * 4 + [small(HEAD_W), new(4), past_k, past_v, new(5), new(6)],
        out_specs=pl.BlockSpec((None, seq_len, GROUP_W), lambda bi, ji: (bi, 0, 0)),
        out_shape=jax.ShapeDtypeStruct((b, seq_len, GROUP_W), BF16),
        scratch_shapes=[pltpu.VMEM((HEADS, 2 * seq_len, HEAD_W), F32)] * 3,
        compiler_params=_params("parallel", "arbitrary"),
        name="attn_sample",
    )(lq1, lk1, lq2, lk2, gn, p, k_past, v_past, p, p)


def _layer(x, ret_s0, k_past, v_past, w, lam_init):
    (f1_pre, f1_wg, f1_wu, f1_wd, f1_post, mix_pre, w_in, ret_g, lq1, lk1, lq2, lk2, diff_g, w_out,
     mix_post, f2_pre, f2_wg, f2_wu, f2_wd, f2_post) = w
    b, seq_len, _ = x.shape
    past_len = 0 if k_past is None else k_past.shape[1]
    x = x.reshape(b * seq_len, D_MODEL)
    x1 = _ffn(x, f1_pre, f1_wg, f1_wu, f1_wd, f1_post)
    p, k_new, v_new = _mix_in(x1, mix_pre, w_in, seq_len, past_len)
    ret_y, s_new = _retention(p, ret_s0, ret_g, seq_len)
    if k_past is None:
        diff_y = _attn_prompt(p, lq1, lk1, lq2, lk2, diff_g, lam_init)
    else:
        diff_y = _attn_sample(p, k_past, v_past, lq1, lk1, lq2, lk2, diff_g, lam_init)
    x3 = _out_ffn(x1, ret_y.reshape(b * seq_len, GROUP_W), diff_y.reshape(b * seq_len, GROUP_W),
                  w_out, mix_post, f2_pre, f2_wg, f2_wu, f2_wd, f2_post)
    return x3.reshape(b, seq_len, D_MODEL), s_new, k_new, v_new


def kernel(x_prompt, x_sample, state_ret, cache_diff_k, cache_diff_v, ffn1_pre_g, ffn1_w_gate, ffn1_w_up, ffn1_w_down, ffn1_post_g, mix_pre_g, w_in, ret_norm_g, diff_lq1, diff_lk1, diff_lq2, diff_lk2, diff_norm_g, w_out, mix_post_g, ffn2_pre_g, ffn2_w_gate, ffn2_w_up, ffn2_w_down, ffn2_post_g):
    depth = w_in.shape[0]
    xp, xs = x_prompt, x_sample
    outs = [[] for _ in range(6)]
    for li in range(depth):
        lam_init = 0.8 - 0.6 * math.exp(-0.3 * li)
        row = lambda a: a[li].reshape(1, -1)
        mat = lambda a: a[li].astype(BF16)
        w = (row(ffn1_pre_g), mat(ffn1_w_gate), mat(ffn1_w_up), mat(ffn1_w_down), row(ffn1_post_g),
             row(mix_pre_g), mat(w_in), row(ret_norm_g), row(diff_lq1), row(diff_lk1), row(diff_lq2),
             row(diff_lk2), row(diff_norm_g), mat(w_out), row(mix_post_g),
             row(ffn2_pre_g), mat(ffn2_w_gate), mat(ffn2_w_up), mat(ffn2_w_down), row(ffn2_post_g))
        s0 = jnp.zeros((xp.shape[0], HEADS, RET_DK, HEAD_W), F32)
        xp, s_p, k_p, v_p = _layer(xp, s0, None, None, w, lam_init)
        xs, s_s, k_s, v_s = _layer(xs, state_ret[li], cache_diff_k[li], cache_diff_v[li], w, lam_init)
        for acc, val in zip(outs, (s_p, k_p, v_p, s_s, k_s, v_s)):
            acc.append(val)
    return (xp, xs) + tuple(jnp.stack(o) for o in outs)
```

```python
import functools
import math

import jax
import jax.numpy as jnp
from jax import lax
from jax.experimental import pallas as pl
from jax.experimental.pallas import tpu as pltpu

D_MODEL = 1024
D_FF = 2816
CHUNK = 64
HEADS = 4
HEAD_W = 128
GROUP_W = HEADS * HEAD_W
N_GROUPS = 7
IN_COLS = N_GROUPS * GROUP_W
RET_DK = 128
DIFF_DK = 64
ROT_DIM = DIFF_DK // 4
ROPE_THETA = 500000.0
RET_THETA = 10000.0
EPS = 1e-6
MASK_VALUE = -1e30
LOG2E = math.log2(math.e)

ROW_TILE = 1024
FFN_ROW_TILE = 512
FFN_ROW_GROUPS = 2
RET_CHUNK = 256
ATTN_TILE = 512
SUM_ROWS = 16
ATTN_Q_SUB = 256
PAST_TILE = 1024
VMEM_LIMIT = 56 * 1024 * 1024

F32 = jnp.float32
BF16 = jnp.bfloat16
_NT = (((1,), (1,)), ((), ()))
_TN = (((0,), (0,)), ((), ()))


def _rms(x, g):
    return x * lax.rsqrt(jnp.mean(x * x, axis=-1, keepdims=True) + EPS) * g


def _silu(x):
    return x * (1.0 / (1.0 + jnp.exp(-x)))


def _resident(shape):
    nd = len(shape)
    return pl.BlockSpec(shape, lambda *_: (0,) * nd, pipeline_mode=pl.Buffered(1))


def _params(*sem):
    return pltpu.CompilerParams(dimension_semantics=sem, vmem_limit_bytes=VMEM_LIMIT)


def _ffn_math(x, pre_g, wg_ref, wu_ref, wd_ref, post_g):
    n = FFN_ROW_GROUPS if x.shape[0] % (8 * FFN_ROW_GROUPS) == 0 else 1
    rows = x.shape[0] // n
    xs = [x[g * rows:(g + 1) * rows] for g in range(n)]
    hs = [_rms(xg, pre_g).astype(BF16) for xg in xs]
    gates_ups = [(jnp.dot(h, wg_ref[...], preferred_element_type=F32),
                  jnp.dot(h, wu_ref[...], preferred_element_type=F32)) for h in hs]
    acts = [(_silu(gate) * up).astype(BF16) for gate, up in gates_ups]
    ys = [jnp.dot(act, wd_ref[...], preferred_element_type=F32) for act in acts]
    outs = [xg + 0.5 * _rms(y, post_g) for xg, y in zip(xs, ys)]
    return outs[0] if n == 1 else jnp.concatenate(outs, axis=0)


def _ffn_kernel(x_ref, pre_ref, wg_ref, wu_ref, wd_ref, post_ref, o_ref):
    o_ref[...] = _ffn_math(x_ref[...], pre_ref[...], wg_ref, wu_ref, wd_ref, post_ref[...])


def _ffn(x, pre_g, wg, wu, wd, post_g):
    m = x.shape[0]
    tm = min(FFN_ROW_TILE, m)
    row = pl.BlockSpec((tm, D_MODEL), lambda i: (i, 0))
    return pl.pallas_call(
        _ffn_kernel,
        grid=(m // tm,),
        in_specs=[row, _resident((1, D_MODEL)), _resident((D_MODEL, D_FF)), _resident((D_MODEL, D_FF)),
                  _resident((D_FF, D_MODEL)), _resident((1, D_MODEL))],
        out_specs=row,
        out_shape=jax.ShapeDtypeStruct((m, D_MODEL), F32),
        compiler_params=_params("parallel"),
        name="ffn",
    )(x, pre_g, wg, wu, wd, post_g)


def _out_ffn_kernel(x_ref, ry_ref, dy_ref, wo_ref, mixpost_ref, pre_ref, wg_ref, wu_ref, wd_ref, post_ref,
                    o_ref):
    y = (jnp.dot(ry_ref[...], wo_ref[:GROUP_W, :], preferred_element_type=F32)
         + jnp.dot(dy_ref[...], wo_ref[GROUP_W:, :], preferred_element_type=F32))
    x2 = x_ref[...] + _rms(y, mixpost_ref[...])
    o_ref[...] = _ffn_math(x2, pre_ref[...], wg_ref, wu_ref, wd_ref, post_ref[...])


def _out_ffn(x, ret_y, diff_y, wo, mixpost_g, pre_g, wg, wu, wd, post_g):
    m = x.shape[0]
    tm = min(FFN_ROW_TILE, m)
    row = pl.BlockSpec((tm, D_MODEL), lambda i: (i, 0))
    half = pl.BlockSpec((tm, GROUP_W), lambda i: (i, 0))
    return pl.pallas_call(
        _out_ffn_kernel,
        grid=(m // tm,),
        in_specs=[row, half, half, _resident((D_MODEL, D_MODEL)), _resident((1, D_MODEL)),
                  _resident((1, D_MODEL)), _resident((D_MODEL, D_FF)), _resident((D_MODEL, D_FF)),
                  _resident((D_FF, D_MODEL)), _resident((1, D_MODEL))],
        out_specs=row,
        out_shape=jax.ShapeDtypeStruct((m, D_MODEL), F32),
        compiler_params=_params("parallel"),
        name="out_ffn",
    )(x, ret_y, diff_y, wo, mixpost_g, pre_g, wg, wu, wd, post_g)


def _rope_tables(pos):
    pos = pos.astype(F32)[:, None]
    half = RET_DK // 2
    inv = jnp.power(jnp.float32(RET_THETA), -jnp.arange(half, dtype=F32) * (2.0 / RET_DK))
    ang = pos * inv[None, :]
    cos, sin = jnp.cos(ang), jnp.sin(ang)
    cr = jnp.concatenate([cos, cos], axis=1)
    sr = jnp.concatenate([-sin, sin], axis=1)
    half = ROT_DIM // 2
    inv = jnp.power(jnp.float32(ROPE_THETA), -jnp.arange(half, dtype=F32) * (2.0 / ROT_DIM))
    ang = pos * inv[None, :]
    cos, sin = jnp.cos(ang), jnp.sin(ang)
    n = pos.shape[0]
    ones = jnp.ones((n, DIFF_DK - ROT_DIM), F32)
    zeros = jnp.zeros((n, DIFF_DK - ROT_DIM), F32)
    z8 = jnp.zeros((n, half), F32)
    c_comp = jnp.concatenate([cos, cos, ones], axis=1)
    lo_comp = jnp.concatenate([-sin, z8, zeros], axis=1)
    hi_comp = jnp.concatenate([z8, sin, zeros], axis=1)
    cd = jnp.concatenate([c_comp, c_comp], axis=1)
    s_lo = jnp.concatenate([lo_comp, lo_comp], axis=1)
    s_hi = jnp.concatenate([hi_comp, hi_comp], axis=1)
    return cr, sr, cd, s_lo, s_hi


def _mix_in_kernel(x_ref, g_ref, w_ref, cr_ref, sr_ref, cd_ref, slo_ref, shi_ref, p_ref, kd_ref, vd_ref,
                   *, transpose_k):
    h = _rms(x_ref[...], g_ref[...]).astype(BF16)
    for grp in (6, 5, 4, 1, 0, 2, 3):
        pg = jnp.dot(h, w_ref[:, grp * GROUP_W:(grp + 1) * GROUP_W], preferred_element_type=F32)
        if grp == 6:
            vd_ref[...] = pg.reshape(pg.shape[0], HEADS, HEAD_W)
        for t in range(HEADS):
            cols = slice(t * HEAD_W, (t + 1) * HEAD_W)
            xt = pg[:, cols]
            if grp in (0, 1):
                xt = xt * cr_ref[...] + pltpu.roll(xt, RET_DK // 2, 1) * sr_ref[...]
                if grp == 1:
                    xt = xt * (RET_DK ** -0.5)
            elif grp in (4, 5):
                xt = (xt * cd_ref[...] + pltpu.roll(xt, HEAD_W - ROT_DIM // 2, 1) * slo_ref[...]
                      + pltpu.roll(xt, ROT_DIM // 2, 1) * shi_ref[...])
                if grp == 5 and transpose_k:
                    kd_ref[cols, :] = xt.T
                elif grp == 5:
                    kd_ref[:, cols] = xt
                else:
                    xt = xt * (DIFF_DK ** -0.5 * LOG2E)
            p_ref[:, grp * GROUP_W + t * HEAD_W:grp * GROUP_W + (t + 1) * HEAD_W] = xt.astype(BF16)


def _mix_in(x, g, w, seq_len, pos0):
    m = x.shape[0]
    tm = min(ROW_TILE, m)
    n_tab = max(seq_len, tm)
    pos = pos0 + (jnp.arange(n_tab, dtype=jnp.int32) % seq_len)
    tables = _rope_tables(pos)
    n_tab_blocks = n_tab // tm
    row = pl.BlockSpec((tm, D_MODEL), lambda i: (i, 0))
    tab = pl.BlockSpec((tm, HEAD_W), lambda i: (i % n_tab_blocks, 0))
    transpose_k = seq_len % tm == 0
    if transpose_k:
        n_seq_blocks = seq_len // tm
        k_spec = pl.BlockSpec((None, GROUP_W, tm), lambda i: (i // n_seq_blocks, 0, i % n_seq_blocks))
        k_shape = jax.ShapeDtypeStruct((m // seq_len, GROUP_W, seq_len), F32)
    else:
        k_spec = pl.BlockSpec((tm, GROUP_W), lambda i: (i, 0))
        k_shape = jax.ShapeDtypeStruct((m, GROUP_W), F32)
    p, k_new, v_new = pl.pallas_call(
        functools.partial(_mix_in_kernel, transpose_k=transpose_k),
        grid=(m // tm,),
        in_specs=[row, _resident((1, D_MODEL)), _resident((D_MODEL, IN_COLS)), tab, tab, tab, tab, tab],
        out_specs=[pl.BlockSpec((tm, IN_COLS), lambda i: (i, 0)), k_spec,
                   pl.BlockSpec((tm, HEADS, HEAD_W), lambda i: (i, 0, 0))],
        out_shape=[jax.ShapeDtypeStruct((m, IN_COLS), BF16), k_shape,
                   jax.ShapeDtypeStruct((m, HEADS, HEAD_W), F32)],
        compiler_params=_params("parallel"),
        name="mix_in",
    )(x, g, w, *tables)
    b = m // seq_len
    if transpose_k:
        k_new = jnp.transpose(k_new.reshape(b, HEADS, 2, DIFF_DK, seq_len), (0, 4, 1, 2, 3))
    else:
        k_new = k_new.reshape(b, seq_len, HEADS, 2, DIFF_DK)
    return p.reshape(b, seq_len, IN_COLS), k_new, v_new.reshape(b, seq_len, HEADS, HEAD_W)


def _ret_tables(c):
    log_g = jnp.log(1.0 - jnp.power(2.0, -5.0 - jnp.arange(HEADS, dtype=F32)))
    idx = jnp.arange(c, dtype=F32)
    rel = idx[:, None] - idx[None, :]
    dmask = jnp.where(rel >= 0, jnp.exp(log_g[:, None, None] * jnp.maximum(rel, 0.0)), 0.0)
    q_dec = jnp.exp(log_g[:, None] * (idx + 1.0)[None, :])
    k_dec = jnp.exp(log_g[:, None] * (c - 1.0 - idx)[None, :])
    b_dec = jnp.exp(log_g * c)
    bcast = lambda a: jnp.broadcast_to(a[:, :, None], (HEADS, c, HEAD_W))
    return dmask, bcast(q_dec), bcast(k_dec), jnp.broadcast_to(b_dec[:, None, None], (HEADS, 8, HEAD_W))


def _ret_kernel(q_ref, k_ref, v_ref, g_ref, s0_ref, dmask_ref, qdec_ref, kdec_ref, bdec_ref, gn_ref,
                y_ref, snew_ref, s_scr, *, c, n_c):
    t = pl.program_id(1)

    @pl.when(t == 0)
    def _():
        s_scr[...] = s0_ref[...]

    for h in range(HEADS):
        cols = slice(h * HEAD_W, (h + 1) * HEAD_W)
        state = s_scr[h]
        b_dec = bdec_ref[h][0:1, :]
        for ci in range(n_c):
            rows = slice(ci * c, (ci + 1) * c)
            q = q_ref[rows, cols]
            k = k_ref[rows, cols]
            v = v_ref[rows, cols]
            scores = lax.dot_general(q, k, _NT, preferred_element_type=F32) * dmask_ref[h]
            o = (jnp.dot(scores.astype(BF16), v, preferred_element_type=F32)
                 + qdec_ref[h] * jnp.dot(q, state.astype(BF16), preferred_element_type=F32))
            k_scaled = (k.astype(F32) * kdec_ref[h]).astype(BF16)
            state = b_dec * state + lax.dot_general(k_scaled, v, _TN, preferred_element_type=F32)
            mu = jnp.mean(o, axis=-1, keepdims=True)
            d = o - mu
            var = jnp.mean(d * d, axis=-1, keepdims=True)
            normed = d * lax.rsqrt(var + EPS) * gn_ref[:, cols]
            y_ref[rows, cols] = (_silu(g_ref[rows, cols].astype(F32)) * normed).astype(BF16)
        s_scr[h] = state

    @pl.when(t == pl.num_programs(1) - 1)
    def _():
        snew_ref[...] = s_scr[...]


def _retention(p, s0, gn, seq_len):
    b = p.shape[0]
    c = min(RET_CHUNK, seq_len)
    n_c = min(2, seq_len // c)
    t_rows = c * n_c
    tables = _ret_tables(c)
    col = lambda g: pl.BlockSpec((None, t_rows, GROUP_W), lambda bi, ti, g=g: (bi, ti, g))
    state = pl.BlockSpec((None, HEADS, RET_DK, HEAD_W), lambda bi, ti: (bi, 0, 0, 0))
    return pl.pallas_call(
        functools.partial(_ret_kernel, c=c, n_c=n_c),
        grid=(b, seq_len // t_rows),
        in_specs=[col(0), col(1), col(2), col(3), state,
                  _resident((HEADS, c, c)), _resident((HEADS, c, HEAD_W)), _resident((HEADS, c, HEAD_W)),
                  _resident((HEADS, 8, HEAD_W)), _resident((1, GROUP_W))],
        out_specs=[pl.BlockSpec((None, t_rows, GROUP_W), lambda bi, ti: (bi, ti, 0)), state],
        out_shape=[jax.ShapeDtypeStruct((b, seq_len, GROUP_W), BF16),
                   jax.ShapeDtypeStruct((b, HEADS, RET_DK, HEAD_W), F32)],
        scratch_shapes=[pltpu.VMEM((HEADS, RET_DK, HEAD_W), F32)],
        compiler_params=_params("parallel", "arbitrary"),
        name="retention",
    )(p, p, p, p, s0, *tables, gn)


def _lambda(lq1_ref, lk1_ref, lq2_ref, lk2_ref, lam_init):
    return (jnp.exp(jnp.sum(lq1_ref[...] * lk1_ref[...], axis=-1, keepdims=True))
            - jnp.exp(jnp.sum(lq2_ref[...] * lk2_ref[...], axis=-1, keepdims=True)) + lam_init)


def _split_components(q):
    lane = lax.broadcasted_iota(jnp.int32, q.shape, 1)
    zero = jnp.zeros_like(q)
    return jnp.where(lane < DIFF_DK, q, zero), jnp.where(lane >= DIFF_DK, q, zero)


def _flash_update(s, v, mask, m_ref, l_ref, acc_ref):
    if mask is not None:
        s = jnp.where(mask, s, MASK_VALUE)
    pad = -s.shape[1] % HEAD_W
    if pad:
        s = jnp.concatenate([s, jnp.full((s.shape[0], pad), MASK_VALUE, F32)], axis=1)
        v = jnp.concatenate([v, jnp.zeros((pad, v.shape[1]), v.dtype)], axis=0)
    blocks =[s[:, b * HEAD_W:(b + 1) * HEAD_W] for b in range(s.shape[1] // HEAD_W)]
    m_prev = m_ref[...]
    m_new = jnp.maximum(m_prev, jnp.max(functools.reduce(jnp.maximum, blocks), axis=-1, keepdims=True))
    alpha = jnp.exp2(m_prev - m_new)
    probs = [jnp.exp2(blk - m_new) for blk in blocks]
    l_ref[...] = alpha * l_ref[...] + functools.reduce(jnp.add, probs)
    p = jnp.concatenate([x.astype(BF16) for x in probs], axis=1)
    acc_ref[...] = alpha * acc_ref[...] + jnp.dot(p, v, preferred_element_type=F32)
    m_ref[...] = m_new


def _flash_init(m_ref, l_ref, acc_ref):
    m_ref[...] = jnp.full(m_ref.shape, MASK_VALUE, F32)
    l_ref[...] = jnp.zeros(l_ref.shape, F32)
    acc_ref[...] = jnp.zeros(acc_ref.shape, F32)


def _flash_update_t(s_t, s_max, vt_ones, mask, m_ref, acc_ref):
    if mask is not None:
        s_t = jnp.where(mask, s_t, MASK_VALUE)
        s_max = jnp.max(s_t, axis=0, keepdims=True)
    m_prev = m_ref[...]
    m_new = jnp.maximum(m_prev, s_max)
    alpha = jnp.exp2(m_prev - m_new)
    p_t = jnp.exp2(s_t - m_new).astype(BF16)
    acc_ref[...] = alpha * acc_ref[...] + jnp.dot(vt_ones, p_t, preferred_element_type=F32)
    m_ref[...] = m_new


def _attn_prompt_kernel(lq1_ref, lk1_ref, lq2_ref, lk2_ref, gn_ref, q_ref, k_ref, v_ref, o_ref,
                        *scratch, tile, lam_init):
    i = pl.program_id(1)
    n_sub = tile // ATTN_Q_SUB
    chains = [(qs, c) for qs in range(n_sub) for c in range(2)]
    stats = [{ch: scratch[2 * (par * len(chains) + n):2 * (par * len(chains) + n) + 2]
              for n, ch in enumerate(chains)} for par in range(2)]
    base = 4 * len(chains)
    slots = [(scratch[base + n], scratch[base + 5 + n]) for n in range(4)]
    slot_pairs = (slots[0:2], slots[2:4])
    qt_scr = scratch[base + 4]
    lam = _lambda(lq1_ref, lk1_ref, lq2_ref, lk2_ref, lam_init)
    key_chunk = lax.broadcasted_iota(jnp.int32, (tile, tile), 0) // CHUNK
    query_chunk = lax.broadcasted_iota(jnp.int32, (tile, tile), 1) // CHUNK
    diag_mask = key_chunk <= query_chunk

    def head_cols(h):
        return slice(h * HEAD_W, (h + 1) * HEAD_W)

    def key_rows(j):
        return pl.ds(pl.multiple_of(j * tile, tile), tile)

    def scores_into(h, slot, j):
        k = k_ref[key_rows(j), head_cols(h)]
        for n, (qs, c) in enumerate(chains):
            queries = slice(qs * ATTN_Q_SUB, (qs + 1) * ATTN_Q_SUB)
            s_t = jnp.dot(k, qt_scr[h % 2, c, :, queries], preferred_element_type=F32)
            slot[0][n] = s_t
            slot[1][n] = jnp.max(s_t, axis=0, keepdims=True)

    def accumulate(h, slot, j, mask):
        v_t = v_ref[key_rows(j), head_cols(h)].T
        vt_ones = jnp.concatenate([v_t, jnp.ones((SUM_ROWS, tile), BF16)], axis=0)
        for n, (qs, c) in enumerate(chains):
            if mask is None:
                _flash_update_t(slot[0][n], slot[1][n], vt_ones, None, *stats[h % 2][(qs, c)])
            else:
                keys = (qs + 1) * ATTN_Q_SUB
                sub_mask = mask[:keys, qs * ATTN_Q_SUB:(qs + 1) * ATTN_Q_SUB]
                _flash_update_t(slot[0][n, :keys, :], None, vt_ones[:, :keys], sub_mask, *stats[h % 2][(qs, c)])

    def start_head(h):
        if h < HEADS:
            for m_ref, acc_ref in stats[h % 2].values():
                m_ref[...] = jnp.full(m_ref.shape, MASK_VALUE, F32)
                acc_ref[...] = jnp.zeros(acc_ref.shape, F32)
            for c, comp in enumerate(_split_components(q_ref[:, head_cols(h)])):
                qt_scr[h % 2, c] = comp.T
            scores_into(h, slot_pairs[h % 2][0], 0)

    def finish_head(h):
        if h < 0:
            return
        for qs in range(n_sub):
            (_, acc1_ref), (_, acc2_ref) = stats[h % 2][(qs, 0)], stats[h % 2][(qs, 1)]
            o_t = (acc1_ref[:HEAD_W] / acc1_ref[HEAD_W:HEAD_W + 1]
                   - lam * (acc2_ref[:HEAD_W] / acc2_ref[HEAD_W:HEAD_W + 1]))
            ms = jnp.mean(o_t * o_t, axis=0, keepdims=True)
            o_t = o_t * lax.rsqrt(ms + EPS) * gn_ref[...] * (1.0 - lam_init)
            o_ref[qs * ATTN_Q_SUB:(qs + 1) * ATTN_Q_SUB, head_cols(h)] = o_t.T.astype(BF16)

    start_head(0)
    for h in range(HEADS):
        slot_a, slot_b = slot_pairs[h % 2]

        def body(jj, carry, h=h, slot_a=slot_a, slot_b=slot_b):
            scores_into(h, slot_b, 2 * jj + 1)
            accumulate(h, slot_a, 2 * jj, None)
            scores_into(h, slot_a, 2 * jj + 2)
            accumulate(h, slot_b, 2 * jj + 1, None)
            return carry

        lax.fori_loop(0, i // 2, body, 0)

        @pl.when(i % 2 == 0)
        def _(h=h, slot_a=slot_a):
            finish_head(h - 1)
            start_head(h + 1)
            accumulate(h, slot_a, i, diag_mask)

        @pl.when(i % 2 == 1)
        def _(h=h, slot_a=slot_a, slot_b=slot_b):
            scores_into(h, slot_b, i)
            accumulate(h, slot_a, i - 1, None)
            finish_head(h - 1)
            start_head(h + 1)
            accumulate(h, slot_b, i, diag_mask)

    finish_head(HEADS - 1)


def _attn_prompt(p, lq1, lk1, lq2, lk2, gn, lam_init):
    b, seq_len, _ = p.shape
    tile = ATTN_TILE
    small = lambda n: pl.BlockSpec((1, n), lambda bi, qi: (0, 0))
    kv = lambda g: pl.BlockSpec((None, seq_len, GROUP_W), lambda bi, qi, g=g: (bi, 0, g))
    return pl.pallas_call(
        functools.partial(_attn_prompt_kernel, tile=tile, lam_init=lam_init),
        grid=(b, seq_len // tile),
        in_specs=[small(DIFF_DK)] * 4 + [pl.BlockSpec((HEAD_W, 1), lambda bi, qi: (0, 0)),
                  pl.BlockSpec((None, tile, GROUP_W), lambda bi, qi: (bi, qi, 4)), kv(5), kv(6)],
        out_specs=pl.BlockSpec((None, tile, GROUP_W), lambda bi, qi: (bi, qi, 0)),
        out_shape=jax.ShapeDtypeStruct((b, seq_len, GROUP_W), BF16),
        scratch_shapes=[pltpu.VMEM((1, ATTN_Q_SUB), F32),
                        pltpu.VMEM((HEAD_W + SUM_ROWS, ATTN_Q_SUB), F32)] * (4 * tile // ATTN_Q_SUB)
                       + [pltpu.VMEM((2 * tile // ATTN_Q_SUB, tile, ATTN_Q_SUB), F32)] * 4
                       + [pltpu.VMEM((2, 2, HEAD_W, tile), BF16)]
                       + [pltpu.VMEM((2 * tile // ATTN_Q_SUB, 1, ATTN_Q_SUB), F32)] * 4,
        compiler_params=_params("parallel", "arbitrary"),
        name="attn_prompt",
    )(lq1, lk1, lq2, lk2, gn.reshape(HEAD_W, 1), p, p, p)


def _attn_sample_kernel(lq1_ref, lk1_ref, lq2_ref, lk2_ref, gn_ref, q_ref, kp_ref, vp_ref, kn_ref, vn_ref,
                        o_ref, m_scr, l_scr, acc_scr, *, n_past, lam_init):
    j = pl.program_id(1)

    @pl.when(j == 0)
    def _():
        _flash_init(m_scr, l_scr, acc_scr)

    n_q = q_ref.shape[0]

    def update(scores_of, values):
        for h in range(HEADS):
            cols = slice(h * HEAD_W, (h + 1) * HEAD_W)
            q_both = jnp.concatenate(_split_components(q_ref[:, cols]), axis=0)
            _flash_update(scores_of(q_both, cols), values[:, cols].astype(BF16), None,
                          m_scr.at[h], l_scr.at[h], acc_scr.at[h])

    @pl.when(j < n_past)
    def _():
        update(lambda q, feats: jnp.dot(q, kp_ref[feats, :].astype(BF16), preferred_element_type=F32),
               vp_ref[...].reshape(vp_ref.shape[0], GROUP_W))

    @pl.when(j == n_past)
    def _():
        update(lambda q, feats: lax.dot_general(q, kn_ref[:, feats], _NT, preferred_element_type=F32),
               vn_ref[...])
        lam = _lambda(lq1_ref, lk1_ref, lq2_ref, lk2_ref, lam_init)
        for h in range(HEADS):
            l = jnp.sum(l_scr[h], axis=-1, keepdims=True)
            o_both = acc_scr[h] / l
            o = o_both[:n_q] - lam * o_both[n_q:]
            o_ref[:, h * HEAD_W:(h + 1) * HEAD_W] = (_rms(o, gn_ref[...]) * (1.0 - lam_init)).astype(BF16)


def _attn_sample(p, k_past, v_past, lq1, lk1, lq2, lk2, gn, lam_init):
    b, seq_len, _ = p.shape
    past_len = k_past.shape[1]
    tk = min(PAST_TILE, past_len)
    n_past = past_len // tk
    k_past = jnp.transpose(k_past, (0, 2, 3, 4, 1)).reshape(b, GROUP_W, past_len)
    small = lambda n: pl.BlockSpec((1, n), lambda bi, ji: (0, 0))
    new = lambda g: pl.BlockSpec((None, seq_len, GROUP_W), lambda bi, ji, g=g: (bi, 0, g))
    past_k = pl.BlockSpec((None, GROUP_W, tk), lambda bi, ji: (bi, 0, jnp.minimum(ji, n_past - 1)))
    past_v = pl.BlockSpec((None, tk, HEADS, HEAD_W), lambda bi, ji: (bi, jnp.minimum(ji, n_past - 1), 0, 0))
    return pl.pallas_call(
        functools.partial(_attn_sample_kernel, n_past=n_past, lam_init=lam_init),
        grid=(b, n_past + 1),
        in_specs=[small(DIFF_DK)] * 4 + [small(HEAD_W), new(4), past_k, past_v, new(5), new(6)],
        out_specs=pl.BlockSpec((None, seq_len, GROUP_W), lambda bi, ji: (bi, 0, 0)),
        out_shape=jax.ShapeDtypeStruct((b, seq_len, GROUP_W), BF16),
        scratch_shapes=[pltpu.VMEM((HEADS, 2 * seq_len, HEAD_W), F32)] * 3,
        compiler_params=_params("parallel", "arbitrary"),
        name="attn_sample",
    )(lq1, lk1, lq2, lk2, gn, p, k_past, v_past, p, p)


def _layer(x, ret_s0, k_past, v_past, w, lam_init):
    (f1_pre, f1_wg, f1_wu, f1_wd, f1_post, mix_pre, w_in, ret_g, lq1, lk1, lq2, lk2, diff_g, w_out,
     mix_post, f2_pre, f2_wg, f2_wu, f2_wd, f2_post) = w
    b, seq_len, _ = x.shape
    past_len = 0 if k_past is None else k_past.shape[1]
    x = x.reshape(b * seq_len, D_MODEL)
    x1 = _ffn(x, f1_pre, f1_wg, f1_wu, f1_wd, f1_post)
    p, k_new, v_new = _mix_in(x1, mix_pre, w_in, seq_len, past_len)
    ret_y, s_new = _retention(p, ret_s0, ret_g, seq_len)
    if k_past is None:
        diff_y = _attn_prompt(p, lq1, lk1, lq2, lk2, diff_g, lam_init)
    else:
        diff_y = _attn_sample(p, k_past, v_past, lq1, lk1, lq2, lk2, diff_g, lam_init)
    x3 = _out_ffn(x1, ret_y.reshape(b * seq_len, GROUP_W), diff_y.reshape(b * seq_len, GROUP_W),
                  w_out, mix_post, f2_pre, f2_wg, f2_wu, f2_wd, f2_post)
    return x3.reshape(b, seq_len, D_MODEL), s_new, k_new, v_new


def kernel(x_prompt, x_sample, state_ret, cache_diff_k, cache_diff_v, ffn1_pre_g, ffn1_w_gate, ffn1_w_up, ffn1_w_down, ffn1_post_g, mix_pre_g, w_in, ret_norm_g, diff_lq1, diff_lk1, diff_lq2, diff_lk2, diff_norm_g, w_out, mix_post_g, ffn2_pre_g, ffn2_w_gate, ffn2_w_up, ffn2_w_down, ffn2_post_g):
    depth = w_in.shape[0]
    xp, xs = x_prompt, x_sample
    outs = [[] for _ in range(6)]
    for li in range(depth):
        lam_init = 0.8 - 0.6 * math.exp(-0.3 * li)
        row = lambda a: a[li].reshape(1, -1)
        mat = lambda a: a[li].astype(BF16)
        w = (row(ffn1_pre_g), mat(ffn1_w_gate), mat(ffn1_w_up), mat(ffn1_w_down), row(ffn1_post_g),
             row(mix_pre_g), mat(w_in), row(ret_norm_g), row(diff_lq1), row(diff_lk1), row(diff_lq2),
             row(diff_lk2), row(diff_norm_g), mat(w_out), row(mix_post_g),
             row(ffn2_pre_g), mat(ffn2_w_gate), mat(ffn2_w_up), mat(ffn2_w_down), row(ffn2_post_g))
        s0 = jnp.zeros((xp.shape[0], HEADS, RET_DK, HEAD_W), F32)
        xp, s_p, k_p, v_p = _layer(xp, s0, None, None, w, lam_init)
        xs, s_s, k_s, v_s = _layer(xs, state_ret[li], cache_diff_k[li], cache_diff_v[li], w, lam_init)
        for acc, val in zip(outs, (s_p, k_p, v_p, s_s, k_s, v_s)):
            acc.append(val)
    return (xp, xs) + tuple(jnp.stack(o) for o in outs)
```

```python
import functools
import math

import jax
import jax.numpy as jnp
from jax import lax
from jax.experimental import pallas as pl
from jax.experimental.pallas import tpu as pltpu

D_MODEL = 1024
D_FF = 2816
CHUNK = 64
HEADS = 4
HEAD_W = 128
GROUP_W = HEADS * HEAD_W
N_GROUPS = 7
IN_COLS = N_GROUPS * GROUP_W
RET_DK = 128
DIFF_DK = 64
ROT_DIM = DIFF_DK // 4
ROPE_THETA = 500000.0
RET_THETA = 10000.0
EPS = 1e-6
MASK_VALUE = -1e30
LOG2E = math.log2(math.e)

ROW_TILE = 1024
FFN_ROW_TILE = 512
FFN_ROW_GROUPS = 2
RET_CHUNK = 256
ATTN_TILE = 512
ATTN_Q_SUB = 256
ATTN_HEAD_GROUP = 2
SUM_ROWS = 16
PAST_TILE = 1024
VMEM_LIMIT = 56 * 1024 * 1024

F32 = jnp.float32
BF16 = jnp.bfloat16
_NT = (((1,), (1,)), ((), ()))
_TN = (((0,), (0,)), ((), ()))


def _rms(x, g):
    return x * lax.rsqrt(jnp.mean(x * x, axis=-1, keepdims=True) + EPS) * g


def _silu(x):
    return x * (1.0 / (1.0 + jnp.exp(-x)))


def _resident(shape):
    nd = len(shape)
    return pl.BlockSpec(shape, lambda *_: (0,) * nd, pipeline_mode=pl.Buffered(1))


def _params(*sem):
    return pltpu.CompilerParams(dimension_semantics=sem, vmem_limit_bytes=VMEM_LIMIT)


def _ffn_math(x, pre_g, wg_ref, wu_ref, wd_ref, post_g):
    n = FFN_ROW_GROUPS if x.shape[0] % (8 * FFN_ROW_GROUPS) == 0 else 1
    rows = x.shape[0] // n
    xs = [x[g * rows:(g + 1) * rows] for g in range(n)]
    hs = [_rms(xg, pre_g).astype(BF16) for xg in xs]
    gates_ups = [(jnp.dot(h, wg_ref[...], preferred_element_type=F32),
                  jnp.dot(h, wu_ref[...], preferred_element_type=F32)) for h in hs]
    acts = [(_silu(gate) * up).astype(BF16) for gate, up in gates_ups]
    ys = [jnp.dot(act, wd_ref[...], preferred_element_type=F32) for act in acts]
    outs = [xg + 0.5 * _rms(y, post_g) for xg, y in zip(xs, ys)]
    return outs[0] if n == 1 else jnp.concatenate(outs, axis=0)


def _ffn_kernel(x_ref, pre_ref, wg_ref, wu_ref, wd_ref, post_ref, o_ref):
    o_ref[...] = _ffn_math(x_ref[...], pre_ref[...], wg_ref, wu_ref, wd_ref, post_ref[...])


def _ffn(x, pre_g, wg, wu, wd, post_g):
    m = x.shape[0]
    tm = min(FFN_ROW_TILE, m)
    row = pl.BlockSpec((tm, D_MODEL), lambda i: (i, 0))
    return pl.pallas_call(
        _ffn_kernel,
        grid=(m // tm,),
        in_specs=[row, _resident((1, D_MODEL)), _resident((D_MODEL, D_FF)), _resident((D_MODEL, D_FF)),
                  _resident((D_FF, D_MODEL)), _resident((1, D_MODEL))],
        out_specs=row,
        out_shape=jax.ShapeDtypeStruct((m, D_MODEL), F32),
        compiler_params=_params("parallel"),
        name="ffn",
    )(x, pre_g, wg, wu, wd, post_g)


def _out_ffn_kernel(x_ref, ry_ref, dy_ref, wo_ref, mixpost_ref, pre_ref, wg_ref, wu_ref, wd_ref, post_ref,
                    o_ref):
    y = (jnp.dot(ry_ref[...], wo_ref[:GROUP_W, :], preferred_element_type=F32)
         + jnp.dot(dy_ref[...], wo_ref[GROUP_W:, :], preferred_element_type=F32))
    x2 = x_ref[...] + _rms(y, mixpost_ref[...])
    o_ref[...] = _ffn_math(x2, pre_ref[...], wg_ref, wu_ref, wd_ref, post_ref[...])


def _out_ffn(x, ret_y, diff_y, wo, mixpost_g, pre_g, wg, wu, wd, post_g):
    m = x.shape[0]
    tm = min(FFN_ROW_TILE, m)
    row = pl.BlockSpec((tm, D_MODEL), lambda i: (i, 0))
    half = pl.BlockSpec((tm, GROUP_W), lambda i: (i, 0))
    return pl.pallas_call(
        _out_ffn_kernel,
        grid=(m // tm,),
        in_specs=[row, half, half, _resident((D_MODEL, D_MODEL)), _resident((1, D_MODEL)),
                  _resident((1, D_MODEL)), _resident((D_MODEL, D_FF)), _resident((D_MODEL, D_FF)),
                  _resident((D_FF, D_MODEL)), _resident((1, D_MODEL))],
        out_specs=row,
        out_shape=jax.ShapeDtypeStruct((m, D_MODEL), F32),
        compiler_params=_params("parallel"),
        name="out_ffn",
    )(x, ret_y, diff_y, wo, mixpost_g, pre_g, wg, wu, wd, post_g)


def _rope_tables(pos):
    pos = pos.astype(F32)[:, None]
    half = RET_DK // 2
    inv = jnp.power(jnp.float32(RET_THETA), -jnp.arange(half, dtype=F32) * (2.0 / RET_DK))
    ang = pos * inv[None, :]
    cos, sin = jnp.cos(ang), jnp.sin(ang)
    cr = jnp.concatenate([cos, cos], axis=1)
    sr = jnp.concatenate([-sin, sin], axis=1)
    half = ROT_DIM // 2
    inv = jnp.power(jnp.float32(ROPE_THETA), -jnp.arange(half, dtype=F32) * (2.0 / ROT_DIM))
    ang = pos * inv[None, :]
    cos, sin = jnp.cos(ang), jnp.sin(ang)
    n = pos.shape[0]
    ones = jnp.ones((n, DIFF_DK - ROT_DIM), F32)
    zeros = jnp.zeros((n, DIFF_DK - ROT_DIM), F32)
    z8 = jnp.zeros((n, half), F32)
    c_comp = jnp.concatenate([cos, cos, ones], axis=1)
    lo_comp = jnp.concatenate([-sin, z8, zeros], axis=1)
    hi_comp = jnp.concatenate([z8, sin, zeros], axis=1)
    cd = jnp.concatenate([c_comp, c_comp], axis=1)
    s_lo = jnp.concatenate([lo_comp, lo_comp], axis=1)
    s_hi = jnp.concatenate([hi_comp, hi_comp], axis=1)
    return cr, sr, cd, s_lo, s_hi


def _mix_in_kernel(x_ref, g_ref, w_ref, cr_ref, sr_ref, cd_ref, slo_ref, shi_ref, p_ref, kd_ref, vd_ref,
                   *, transpose_k):
    h = _rms(x_ref[...], g_ref[...]).astype(BF16)
    for grp in (6, 5, 4, 1, 0, 2, 3):
        pg = jnp.dot(h, w_ref[:, grp * GROUP_W:(grp + 1) * GROUP_W], preferred_element_type=F32)
        if grp == 6:
            vd_ref[...] = pg.reshape(pg.shape[0], HEADS, HEAD_W)
        for t in range(HEADS):
            cols = slice(t * HEAD_W, (t + 1) * HEAD_W)
            xt = pg[:, cols]
            if grp in (0, 1):
                xt = xt * cr_ref[...] + pltpu.roll(xt, RET_DK // 2, 1) * sr_ref[...]
                if grp == 1:
                    xt = xt * (RET_DK ** -0.5)
            elif grp in (4, 5):
                xt = (xt * cd_ref[...] + pltpu.roll(xt, HEAD_W - ROT_DIM // 2, 1) * slo_ref[...]
                      + pltpu.roll(xt, ROT_DIM // 2, 1) * shi_ref[...])
                if grp == 5 and transpose_k:
                    kd_ref[cols, :] = xt.T
                elif grp == 5:
                    kd_ref[:, cols] = xt
                else:
                    xt = xt * (DIFF_DK ** -0.5 * LOG2E)
            p_ref[:, grp * GROUP_W + t * HEAD_W:grp * GROUP_W + (t + 1) * HEAD_W] = xt.astype(BF16)


def _mix_in(x, g, w, seq_len, pos0):
    m = x.shape[0]
    tm = min(ROW_TILE, m)
    n_tab = max(seq_len, tm)
    pos = pos0 + (jnp.arange(n_tab, dtype=jnp.int32) % seq_len)
    tables = _rope_tables(pos)
    n_tab_blocks = n_tab // tm
    row = pl.BlockSpec((tm, D_MODEL), lambda i: (i, 0))
    tab = pl.BlockSpec((tm, HEAD_W), lambda i: (i % n_tab_blocks, 0))
    transpose_k = seq_len % tm == 0
    if transpose_k:
        n_seq_blocks = seq_len // tm
        k_spec = pl.BlockSpec((None, GROUP_W, tm), lambda i: (i // n_seq_blocks, 0, i % n_seq_blocks))
        k_shape = jax.ShapeDtypeStruct((m // seq_len, GROUP_W, seq_len), F32)
    else:
        k_spec = pl.BlockSpec((tm, GROUP_W), lambda i: (i, 0))
        k_shape = jax.ShapeDtypeStruct((m, GROUP_W), F32)
    p, k_new, v_new = pl.pallas_call(
        functools.partial(_mix_in_kernel, transpose_k=transpose_k),
        grid=(m // tm,),
        in_specs=[row, _resident((1, D_MODEL)), _resident((D_MODEL, IN_COLS)), tab, tab, tab, tab, tab],
        out_specs=[pl.BlockSpec((tm, IN_COLS), lambda i: (i, 0)), k_spec,
                   pl.BlockSpec((tm, HEADS, HEAD_W), lambda i: (i, 0, 0))],
        out_shape=[jax.ShapeDtypeStruct((m, IN_COLS), BF16), k_shape,
                   jax.ShapeDtypeStruct((m, HEADS, HEAD_W), F32)],
        compiler_params=_params("parallel"),
        name="mix_in",
    )(x, g, w, *tables)
    b = m // seq_len
    if transpose_k:
        k_new = jnp.transpose(k_new.reshape(b, HEADS, 2, DIFF_DK, seq_len), (0, 4, 1, 2, 3))
    else:
        k_new = k_new.reshape(b, seq_len, HEADS, 2, DIFF_DK)
    return p.reshape(b, seq_len, IN_COLS), k_new, v_new.reshape(b, seq_len, HEADS, HEAD_W)


def _ret_tables(c):
    log_g = jnp.log(1.0 - jnp.power(2.0, -5.0 - jnp.arange(HEADS, dtype=F32)))
    idx = jnp.arange(c, dtype=F32)
    rel = idx[:, None] - idx[None, :]
    dmask = jnp.where(rel >= 0, jnp.exp(log_g[:, None, None] * jnp.maximum(rel, 0.0)), 0.0)
    q_dec = jnp.exp(log_g[:, None] * (idx + 1.0)[None, :])
    k_dec = jnp.exp(log_g[:, None] * (c - 1.0 - idx)[None, :])
    b_dec = jnp.exp(log_g * c)
    bcast = lambda a: jnp.broadcast_to(a[:, :, None], (HEADS, c, HEAD_W))
    return dmask, bcast(q_dec), bcast(k_dec), jnp.broadcast_to(b_dec[:, None, None], (HEADS, 8, HEAD_W))


def _ret_kernel(q_ref, k_ref, v_ref, g_ref, s0_ref, dmask_ref, qdec_ref, kdec_ref, bdec_ref, gn_ref,
                y_ref, snew_ref, s_scr, *, c, n_c):
    t = pl.program_id(1)

    @pl.when(t == 0)
    def _():
        s_scr[...] = s0_ref[...]

    for h in range(HEADS):
        cols = slice(h * HEAD_W, (h + 1) * HEAD_W)
        state = s_scr[h]
        b_dec = bdec_ref[h][0:1, :]
        for ci in range(n_c):
            rows = slice(ci * c, (ci + 1) * c)
            q = q_ref[rows, cols]
            k = k_ref[rows, cols]
            v = v_ref[rows, cols]
            scores = lax.dot_general(q, k, _NT, preferred_element_type=F32) * dmask_ref[h]
            o = (jnp.dot(scores.astype(BF16), v, preferred_element_type=F32)
                 + qdec_ref[h] * jnp.dot(q, state.astype(BF16), preferred_element_type=F32))
            k_scaled = (k.astype(F32) * kdec_ref[h]).astype(BF16)
            state = b_dec * state + lax.dot_general(k_scaled, v, _TN, preferred_element_type=F32)
            mu = jnp.mean(o, axis=-1, keepdims=True)
            d = o - mu
            var = jnp.mean(d * d, axis=-1, keepdims=True)
            normed = d * lax.rsqrt(var + EPS) * gn_ref[:, cols]
            y_ref[rows, cols] = (_silu(g_ref[rows, cols].astype(F32)) * normed).astype(BF16)
        s_scr[h] = state

    @pl.when(t == pl.num_programs(1) - 1)
    def _():
        snew_ref[...] = s_scr[...]


def _retention(p, s0, gn, seq_len):
    b = p.shape[0]
    c = min(RET_CHUNK, seq_len)
    n_c = min(2, seq_len // c)
    t_rows = c * n_c
    tables = _ret_tables(c)
    col = lambda g: pl.BlockSpec((None, t_rows, GROUP_W), lambda bi, ti, g=g: (bi, ti, g))
    state = pl.BlockSpec((None, HEADS, RET_DK, HEAD_W), lambda bi, ti: (bi, 0, 0, 0))
    return pl.pallas_call(
        functools.partial(_ret_kernel, c=c, n_c=n_c),
        grid=(b, seq_len // t_rows),
        in_specs=[col(0), col(1), col(2), col(3), state,
                  _resident((HEADS, c, c)), _resident((HEADS, c, HEAD_W)), _resident((HEADS, c, HEAD_W)),
                  _resident((HEADS, 8, HEAD_W)), _resident((1, GROUP_W))],
        out_specs=[pl.BlockSpec((None, t_rows, GROUP_W), lambda bi, ti: (bi, ti, 0)), state],
        out_shape=[jax.ShapeDtypeStruct((b, seq_len, GROUP_W), BF16),
                   jax.ShapeDtypeStruct((b, HEADS, RET_DK, HEAD_W), F32)],
        scratch_shapes=[pltpu.VMEM((HEADS, RET_DK, HEAD_W), F32)],
        compiler_params=_params("parallel", "arbitrary"),
        name="retention",
    )(p, p, p, p, s0, *tables, gn)


def _lambda(lq1_ref, lk1_ref, lq2_ref, lk2_ref, lam_init):
    return (jnp.exp(jnp.sum(lq1_ref[...] * lk1_ref[...], axis=-1, keepdims=True))
            - jnp.exp(jnp.sum(lq2_ref[...] * lk2_ref[...], axis=-1, keepdims=True)) + lam_init)


def _split_components(q):
    lane = lax.broadcasted_iota(jnp.int32, q.shape, 1)
    zero = jnp.zeros_like(q)
    return jnp.where(lane < DIFF_DK, q, zero), jnp.where(lane >= DIFF_DK, q, zero)


def _flash_update(s, v, mask, m_ref, l_ref, acc_ref):
    if mask is not None:
        s = jnp.where(mask, s, MASK_VALUE)
    pad = -s.shape[1] % HEAD_W
    if pad:
        s = jnp.concatenate([s, jnp.full((s.shape[0], pad), MASK_VALUE, F32)], axis=1)
        v = jnp.concatenate([v, jnp.zeros((pad, v.shape[1]), v.dtype)], axis=0)
    blocks = [s[:, b * HEAD_W:(b + 1) * HEAD_W] for b in range(s.shape[1] // HEAD_W)]
    m_prev = m_ref[...]
    m_new = jnp.maximum(m_prev, jnp.max(functools.reduce(jnp.maximum, blocks), axis=-1, keepdims=True))
    alpha = jnp.exp2(m_prev - m_new)
    probs = [jnp.exp2(blk - m_new) for blk in blocks]
    l_ref[...] = alpha * l_ref[...] + functools.reduce(jnp.add, probs)
    p = jnp.concatenate([x.astype(BF16) for x in probs], axis=1)
    acc_ref[...] = alpha * acc_ref[...] + jnp.dot(p, v, preferred_element_type=F32)
    m_ref[...] = m_new


def _flash_init(m_ref, l_ref, acc_ref):
    m_ref[...] = jnp.full(m_ref.shape, MASK_VALUE, F32)
    l_ref[...] = jnp.zeros(l_ref.shape, F32)
    acc_ref[...] = jnp.zeros(acc_ref.shape, F32)


def _flash_update_t(s_t, s_max, vt_ones, mask, m_ref, acc_ref):
    if mask is not None:
        s_t = jnp.where(mask, s_t, MASK_VALUE)
        s_max = jnp.max(s_t, axis=0, keepdims=True)
    m_prev = m_ref[...]
    m_new = jnp.maximum(m_prev, s_max)
    alpha = jnp.exp2(m_prev - m_new)
    p_t = jnp.exp2(s_t - m_new).astype(BF16)
    acc_ref[...] = alpha * acc_ref[...] + jnp.dot(vt_ones, p_t, preferred_element_type=F32)
    m_ref[...] = m_new


def _attn_prompt_kernel(lq1_ref, lk1_ref, lq2_ref, lk2_ref, gn_ref, q_ref, k_ref, v_ref, o_ref,
                        *scratch, tile, lam_init):
    i = pl.program_id(1)
    n_sub = tile // ATTN_Q_SUB
    n_groups = HEADS // ATTN_HEAD_GROUP
    chains = [(hh, qs, c) for hh in range(ATTN_HEAD_GROUP) for qs in range(n_sub) for c in range(2)]
    stats = [{ch: scratch[2 * (par * len(chains) + n):2 * (par * len(chains) + n) + 2]
              for n, ch in enumerate(chains)} for par in range(2)]
    base = 4 * len(chains)
    slots = [(scratch[base + n], scratch[base + 5 + n]) for n in range(4)]
    slot_pairs = (slots[0:2], slots[2:4])
    qt_scr = scratch[base + 4]
    lam = _lambda(lq1_ref, lk1_ref, lq2_ref, lk2_ref, lam_init)
    key_chunk = lax.broadcasted_iota(jnp.int32, (tile, tile), 0) // CHUNK
    query_chunk = lax.broadcasted_iota(jnp.int32, (tile, tile), 1) // CHUNK
    diag_mask = key_chunk <= query_chunk

    def head_cols(g, hh):
        h = g * ATTN_HEAD_GROUP + hh
        return slice(h * HEAD_W, (h + 1) * HEAD_W)

    def key_rows(j):
        return pl.ds(pl.multiple_of(j * tile, tile), tile)

    def scores_into(g, slot, j):
        for n, (hh, qs, c) in enumerate(chains):
            queries = slice(qs * ATTN_Q_SUB, (qs + 1) * ATTN_Q_SUB)
            s_t = jnp.dot(k_ref[key_rows(j), head_cols(g, hh)], qt_scr[g % 2, hh, c, :, queries],
                          preferred_element_type=F32)
            slot[0][n] = s_t
            slot[1][n] = jnp.max(s_t, axis=0, keepdims=True)

    def accumulate(g, slot, j, mask):
        ones = jnp.ones((SUM_ROWS, tile), BF16)
        vt_ones = [jnp.concatenate([v_ref[key_rows(j), head_cols(g, hh)].T, ones], axis=0)
                   for hh in range(ATTN_HEAD_GROUP)]
        for n, (hh, qs, c) in enumerate(chains):
            if mask is None:
                _flash_update_t(slot[0][n], slot[1][n], vt_ones[hh], None, *stats[g % 2][(hh, qs, c)])
            else:
                keys = (qs + 1) * ATTN_Q_SUB
                sub_mask = mask[:keys, qs * ATTN_Q_SUB:(qs + 1) * ATTN_Q_SUB]
                _flash_update_t(slot[0][n, :keys, :], None, vt_ones[hh][:, :keys], sub_mask,
                                *stats[g % 2][(hh, qs, c)])

    def start_group(g):
        if g < n_groups:
            for m_ref, acc_ref in stats[g % 2].values():
                m_ref[...] = jnp.full(m_ref.shape, MASK_VALUE, F32)
                acc_ref[...] = jnp.zeros(acc_ref.shape, F32)
            for hh in range(ATTN_HEAD_GROUP):
                for c, comp in enumerate(_split_components(q_ref[:, head_cols(g, hh)])):
                    qt_scr[g % 2, hh, c] = comp.T
            scores_into(g, slot_pairs[g % 2][0], 0)

    def finish_group(g):
        if g < 0:
            return
        for hh in range(ATTN_HEAD_GROUP):
            for qs in range(n_sub):
                (_, acc1_ref), (_, acc2_ref) = stats[g % 2][(hh, qs, 0)], stats[g % 2][(hh, qs, 1)]
                o_t = (acc1_ref[:HEAD_W] / acc1_ref[HEAD_W:HEAD_W + 1]
                       - lam * (acc2_ref[:HEAD_W] / acc2_ref[HEAD_W:HEAD_W + 1]))
                ms = jnp.mean(o_t * o_t, axis=0, keepdims=True)
                o_t = o_t * lax.rsqrt(ms + EPS) * gn_ref[...] * (1.0 - lam_init)
                o_ref[qs * ATTN_Q_SUB:(qs + 1) * ATTN_Q_SUB, head_cols(g, hh)] = o_t.T.astype(BF16)

    start_group(0)
    for g in range(n_groups):
        slot_a, slot_b = slot_pairs[g % 2]

        def body(jj, carry, g=g, slot_a=slot_a, slot_b=slot_b):
            scores_into(g, slot_b, 2 * jj + 1)
            accumulate(g, slot_a, 2 * jj, None)
            scores_into(g, slot_a, 2 * jj + 2)
            accumulate(g, slot_b, 2 * jj + 1, None)
            return carry

        lax.fori_loop(0, i // 2, body, 0)

        @pl.when(i % 2 == 0)
        def _(g=g, slot_a=slot_a):
            finish_group(g - 1)
            start_group(g + 1)
            accumulate(g, slot_a, i, diag_mask)

        @pl.when(i % 2 == 1)
        def _(g=g, slot_a=slot_a, slot_b=slot_b):
            scores_into(g, slot_b, i)
            accumulate(g, slot_a, i - 1, None)
            finish_group(g - 1)
            start_group(g + 1)
            accumulate(g, slot_b, i, diag_mask)

    finish_group(n_groups - 1)


def _attn_prompt(p, lq1, lk1, lq2, lk2, gn, lam_init):
    b, seq_len, _ = p.shape
    tile = ATTN_TILE
    n_chains = ATTN_HEAD_GROUP * 2 * tile // ATTN_Q_SUB
    small = lambda n: pl.BlockSpec((1, n), lambda bi, qi: (0, 0))
    kv = lambda g: pl.BlockSpec((None, seq_len, GROUP_W), lambda bi, qi, g=g: (bi, 0, g))
    return pl.pallas_call(
        functools.partial(_attn_prompt_kernel, tile=tile, lam_init=lam_init),
        grid=(b, seq_len // tile),
        in_specs=[small(DIFF_DK)] * 4 + [pl.BlockSpec((HEAD_W, 1), lambda bi, qi: (0, 0)),
                  pl.BlockSpec((None, tile, GROUP_W), lambda bi, qi: (bi, qi, 4)), kv(5), kv(6)],
        out_specs=pl.BlockSpec((None, tile, GROUP_W), lambda bi, qi: (bi, qi, 0)),
        out_shape=jax.ShapeDtypeStruct((b, seq_len, GROUP_W), BF16),
        scratch_shapes=[pltpu.VMEM((1, ATTN_Q_SUB), F32),
                        pltpu.VMEM((HEAD_W + SUM_ROWS, ATTN_Q_SUB), F32)] * (2 * n_chains)
                       + [pltpu.VMEM((n_chains, tile, ATTN_Q_SUB), F32)] * 4
                       + [pltpu.VMEM((2, ATTN_HEAD_GROUP, 2, HEAD_W, tile), BF16)]
                       + [pltpu.VMEM((n_chains, 1, ATTN_Q_SUB), F32)] * 4,
        compiler_params=_params("parallel", "arbitrary"),
        name="attn_prompt",
    )(lq1, lk1, lq2, lk2, gn.reshape(HEAD_W, 1), p, p, p)


def _attn_sample_kernel(lq1_ref, lk1_ref, lq2_ref, lk2_ref, gn_ref, q_ref, kp_ref, vp_ref, kn_ref, vn_ref,
                        o_ref, m_scr, l_scr, acc_scr, *, n_past, lam_init):
    j = pl.program_id(1)

    @pl.when(j == 0)
    def _():
        _flash_init(m_scr, l_scr, acc_scr)

    n_q = q_ref.shape[0]

    def update(scores_of, values):
        for h in range(HEADS):
            cols = slice(h * HEAD_W, (h + 1) * HEAD_W)
            q_both = jnp.concatenate(_split_components(q_ref[:, cols]), axis=0)
            _flash_update(scores_of(q_both, cols), values[:, cols].astype(BF16), None,
                          m_scr.at[h], l_scr.at[h], acc_scr.at[h])

    @pl.when(j < n_past)
    def _():
        update(lambda q, feats: jnp.dot(q, kp_ref[feats, :].astype(BF16), preferred_element_type=F32),
               vp_ref[...].reshape(vp_ref.shape[0], GROUP_W))

    @pl.when(j == n_past)
    def _():
        update(lambda q, feats: lax.dot_general(q, kn_ref[:, feats], _NT, preferred_element_type=F32),
               vn_ref[...])
        lam = _lambda(lq1_ref, lk1_ref, lq2_ref, lk2_ref, lam_init)
        for h in range(HEADS):
            l = jnp.sum(l_scr[h], axis=-1, keepdims=True)
            o_both = acc_scr[h] / l
            o = o_both[:n_q] - lam * o_both[n_q:]
            o_ref[:, h * HEAD_W:(h + 1) * HEAD_W] = (_rms(o, gn_ref[...]) * (1.0 - lam_init)).astype(BF16)


def _attn_sample(p, k_past, v_past, lq1, lk1, lq2, lk2, gn, lam_init):
    b, seq_len, _ = p.shape
    past_len = k_past.shape[1]
    tk = min(PAST_TILE, past_len)
    n_past = past_len // tk
    k_past = jnp.transpose(k_past, (0, 2, 3, 4, 1)).reshape(b, GROUP_W, past_len)
    small = lambda n: pl.BlockSpec((1, n), lambda bi, ji: (0, 0))
    new = lambda g: pl.BlockSpec((None, seq_len, GROUP_W), lambda bi, ji, g=g: (bi, 0, g))
    past_k = pl.BlockSpec((None, GROUP_W, tk), lambda bi, ji: (bi, 0, jnp.minimum(ji, n_past - 1)))
    past_v = pl.BlockSpec((None, tk, HEADS, HEAD_W), lambda bi, ji: (bi, jnp.minimum(ji, n_past - 1), 0, 0))
    return pl.pallas_call(
        functools.partial(_attn_sample_kernel, n_past=n_past, lam_init=lam_init),
        grid=(b, n_past + 1),
        in_specs=[small(DIFF_DK)] * 4 + [small(HEAD_W), new(4), past_k, past_v, new(5), new(6)],
        out_specs=pl.BlockSpec((None, seq_len, GROUP_W), lambda bi, ji: (bi, 0, 0)),
        out_shape=jax.ShapeDtypeStruct((b, seq_len, GROUP_W), BF16),
        scratch_shapes=[pltpu.VMEM((HEADS, 2 * seq_len, HEAD_W), F32)] * 3,
        compiler_params=_params("parallel", "arbitrary"),
        name="attn_sample",
    )(lq1, lk1, lq2, lk2, gn, p, k_past, v_past, p, p)


def _layer(x, ret_s0, k_past, v_past, w, lam_init):
    (f1_pre, f1_wg, f1_wu, f1_wd, f1_post, mix_pre, w_in, ret_g, lq1, lk1, lq2, lk2, diff_g, w_out,
     mix_post, f2_pre, f2_wg, f2_wu, f2_wd, f2_post) = w
    b, seq_len, _ = x.shape
    past_len = 0 if k_past is None else k_past.shape[1]
    x = x.reshape(b * seq_len, D_MODEL)
    x1 = _ffn(x, f1_pre, f1_wg, f1_wu, f1_wd, f1_post)
    p, k_new, v_new = _mix_in(x1, mix_pre, w_in, seq_len, past_len)
    ret_y, s_new = _retention(p, ret_s0, ret_g, seq_len)
    if k_past is None:
        diff_y = _attn_prompt(p, lq1, lk1, lq2, lk2, diff_g, lam_init)
    else:
        diff_y = _attn_sample(p, k_past, v_past, lq1, lk1, lq2, lk2, diff_g, lam_init)
    x3 = _out_ffn(x1, ret_y.reshape(b * seq_len, GROUP_W), diff_y.reshape(b * seq_len, GROUP_W),
                  w_out, mix_post, f2_pre, f2_wg, f2_wu, f2_wd, f2_post)
    return x3.reshape(b, seq_len, D_MODEL), s_new, k_new, v_new


def kernel(x_prompt, x_sample, state_ret, cache_diff_k, cache_diff_v, ffn1_pre_g, ffn1_w_gate, ffn1_w_up, ffn1_w_down, ffn1_post_g, mix_pre_g, w_in, ret_norm_g, diff_lq1, diff_lk1, diff_lq2, diff_lk2, diff_norm_g, w_out, mix_post_g, ffn2_pre_g, ffn2_w_gate, ffn2_w_up, ffn2_w_down, ffn2_post_g):
    depth = w_in.shape[0]
    xp, xs = x_prompt, x_sample
    outs = [[] for _ in range(6)]
    for li in range(depth):
        lam_init = 0.8 - 0.6 * math.exp(-0.3 * li)
        row = lambda a: a[li].reshape(1, -1)
        mat = lambda a: a[li].astype(BF16)
        w = (row(ffn1_pre_g), mat(ffn1_w_gate), mat(ffn1_w_up), mat(ffn1_w_down), row(ffn1_post_g),
             row(mix_pre_g), mat(w_in), row(ret_norm_g), row(diff_lq1), row(diff_lk1), row(diff_lq2),
             row(diff_lk2), row(diff_norm_g), mat(w_out), row(mix_post_g),
             row(ffn2_pre_g), mat(ffn2_w_gate), mat(ffn2_w_up), mat(ffn2_w_down), row(ffn2_post_g))
        s0 = jnp.zeros((xp.shape[0], HEADS, RET_DK, HEAD_W), F32)
        xp, s_p, k_p, v_p = _layer(xp, s0, None, None, w, lam_init)
        xs, s_s, k_s, v_s = _layer(xs, state_ret[li], cache_diff_k[li], cache_diff_v[li], w, lam_init)
        for acc, val in zip(outs, (s_p, k_p, v_p, s_s, k_s, v_s)):
            acc.append(val)
    return (xp, xs) + tuple(jnp.stack(o) for o in outs)
```

```python
import functools
import math

import jax
import jax.numpy as jnp
from jax import lax
from jax.experimental import pallas as pl
from jax.experimental.pallas import tpu as pltpu

D_MODEL = 1024
D_FF = 2816
CHUNK = 64
HEADS = 4
HEAD_W = 128
GROUP_W = HEADS * HEAD_W
N_GROUPS = 7
IN_COLS = N_GROUPS * GROUP_W
RET_DK = 128
DIFF_DK = 64
ROT_DIM = DIFF_DK // 4
ROPE_THETA = 500000.0
RET_THETA = 10000.0
EPS = 1e-6
MASK_VALUE = -1e30
LOG2E = math.log2(math.e)

ROW_TILE = 1024
FFN_ROW_TILE = 512
FFN_ROW_GROUPS = 2
RET_CHUNK = 256
RET_BLOCKS_PER_STEP = 4
ATTN_TILE = 512
ATTN_Q_SUB = 256
ATTN_HEAD_GROUP = 2
SUM_ROWS = 16
PAST_TILE = 2048
VMEM_LIMIT = 56 * 1024 * 1024

F32 = jnp.float32
BF16 = jnp.bfloat16
_NT = (((1,), (1,)), ((), ()))
_TN = (((0,), (0,)), ((), ()))


def _rms(x, g):
    return x * lax.rsqrt(jnp.mean(x * x, axis=-1, keepdims=True) + EPS) * g


def _silu(x):
    return x * (1.0 / (1.0 + jnp.exp(-x)))


def _resident(shape):
    nd = len(shape)
    return pl.BlockSpec(shape, lambda *_: (0,) * nd, pipeline_mode=pl.Buffered(1))


def _params(*sem):
    return pltpu.CompilerParams(dimension_semantics=sem, vmem_limit_bytes=VMEM_LIMIT)


def _ffn_math(x, pre_g, wg_ref, wu_ref, wd_ref, post_g):
    n = FFN_ROW_GROUPS if x.shape[0] % (8 * FFN_ROW_GROUPS) == 0 else 1
    rows = x.shape[0] // n
    xs = [x[g * rows:(g + 1) * rows] for g in range(n)]
    hs = [_rms(xg, pre_g).astype(BF16) for xg in xs]
    gates_ups = [(jnp.dot(h, wg_ref[...], preferred_element_type=F32),
                  jnp.dot(h, wu_ref[...], preferred_element_type=F32)) for h in hs]
    acts = [(_silu(gate) * up).astype(BF16) for gate, up in gates_ups]
    ys = [jnp.dot(act, wd_ref[...], preferred_element_type=F32) for act in acts]
    outs = [xg + 0.5 * _rms(y, post_g) for xg, y in zip(xs, ys)]
    return outs[0] if n == 1 else jnp.concatenate(outs, axis=0)


def _ffn_kernel(x_ref, pre_ref, wg_ref, wu_ref, wd_ref, post_ref, o_ref):
    o_ref[...] = _ffn_math(x_ref[...], pre_ref[...], wg_ref, wu_ref, wd_ref, post_ref[...])


def _ffn(x, pre_g, wg, wu, wd, post_g):
    m = x.shape[0]
    tm = min(FFN_ROW_TILE, m)
    row = pl.BlockSpec((tm, D_MODEL), lambda i: (i, 0))
    return pl.pallas_call(
        _ffn_kernel,
        grid=(m // tm,),
        in_specs=[row, _resident((1, D_MODEL)), _resident((D_MODEL, D_FF)), _resident((D_MODEL, D_FF)),
                  _resident((D_FF, D_MODEL)), _resident((1, D_MODEL))],
        out_specs=row,
        out_shape=jax.ShapeDtypeStruct((m, D_MODEL), F32),
        compiler_params=_params("parallel"),
        name="ffn",
    )(x, pre_g, wg, wu, wd, post_g)


def _out_ffn_kernel(x_ref, ry_ref, dy_ref, wo_ref, mixpost_ref, pre_ref, wg_ref, wu_ref, wd_ref, post_ref,
                    o_ref):
    mixed = jnp.concatenate([ry_ref[...], dy_ref[...]], axis=1)
    y = jnp.dot(mixed, wo_ref[...], preferred_element_type=F32)
    x2 = x_ref[...] + _rms(y, mixpost_ref[...])
    o_ref[...] = _ffn_math(x2, pre_ref[...], wg_ref, wu_ref, wd_ref, post_ref[...])


def _out_ffn(x, ret_y, diff_y, wo, mixpost_g, pre_g, wg, wu, wd, post_g):
    m = x.shape[0]
    tm = min(FFN_ROW_TILE, m)
    row = pl.BlockSpec((tm, D_MODEL), lambda i: (i, 0))
    half = pl.BlockSpec((tm, GROUP_W), lambda i: (i, 0))
    return pl.pallas_call(
        _out_ffn_kernel,
        grid=(m // tm,),
        in_specs=[row, half, half, _resident((D_MODEL, D_MODEL)), _resident((1, D_MODEL)),
                  _resident((1, D_MODEL)), _resident((D_MODEL, D_FF)), _resident((D_MODEL, D_FF)),
                  _resident((D_FF, D_MODEL)), _resident((1, D_MODEL))],
        out_specs=row,
        out_shape=jax.ShapeDtypeStruct((m, D_MODEL), F32),
        compiler_params=_params("parallel"),
        name="out_ffn",
    )(x, ret_y, diff_y, wo, mixpost_g, pre_g, wg, wu, wd, post_g)


def _rope_tables(pos):
    pos = pos.astype(F32)[:, None]
    half = RET_DK // 2
    inv = jnp.power(jnp.float32(RET_THETA), -jnp.arange(half, dtype=F32) * (2.0 / RET_DK))
    ang = pos * inv[None, :]
    cos, sin = jnp.cos(ang), jnp.sin(ang)
    cr = jnp.concatenate([cos, cos], axis=1)
    sr = jnp.concatenate([-sin, sin], axis=1)
    half = ROT_DIM // 2
    inv = jnp.power(jnp.float32(ROPE_THETA), -jnp.arange(half, dtype=F32) * (2.0 / ROT_DIM))
    ang = pos * inv[None, :]
    cos, sin = jnp.cos(ang), jnp.sin(ang)
    n = pos.shape[0]
    ones = jnp.ones((n, DIFF_DK - ROT_DIM), F32)
    zeros = jnp.zeros((n, DIFF_DK - ROT_DIM), F32)
    z8 = jnp.zeros((n, half), F32)
    c_comp = jnp.concatenate([cos, cos, ones], axis=1)
    lo_comp = jnp.concatenate([-sin, z8, zeros], axis=1)
    hi_comp = jnp.concatenate([z8, sin, zeros], axis=1)
    cd = jnp.concatenate([c_comp, c_comp], axis=1)
    s_lo = jnp.concatenate([lo_comp, lo_comp], axis=1)
    s_hi = jnp.concatenate([hi_comp, hi_comp], axis=1)
    return cr, sr, cd, s_lo, s_hi


def _mix_in_kernel(x_ref, g_ref, w_ref, cr_ref, sr_ref, cd_ref, slo_ref, shi_ref, p_ref, kd_ref, vd_ref,
                   *, transpose_k):
    h = _rms(x_ref[...], g_ref[...]).astype(BF16)
    for grp in (6, 5, 4, 1, 0, 2, 3):
        pg = jnp.dot(h, w_ref[:, grp * GROUP_W:(grp + 1) * GROUP_W], preferred_element_type=F32)
        if grp == 6:
            vd_ref[...] = pg.reshape(pg.shape[0], HEADS, HEAD_W)
        for t in range(HEADS):
            cols = slice(t * HEAD_W, (t + 1) * HEAD_W)
            xt = pg[:, cols]
            if grp in (0, 1):
                xt = xt * cr_ref[...] + pltpu.roll(xt, RET_DK // 2, 1) * sr_ref[...]
                if grp == 1:
                    xt = xt * (RET_DK ** -0.5)
            elif grp in (4, 5):
                xt = (xt * cd_ref[...] + pltpu.roll(xt, HEAD_W - ROT_DIM // 2, 1) * slo_ref[...]
                      + pltpu.roll(xt, ROT_DIM // 2, 1) * shi_ref[...])
                if grp == 5 and transpose_k:
                    kd_ref[cols, :] = xt.T
                elif grp == 5:
                    kd_ref[:, cols] = xt
                else:
                    xt = xt * (DIFF_DK ** -0.5 * LOG2E)
            p_ref[:, grp * GROUP_W + t * HEAD_W:grp * GROUP_W + (t + 1) * HEAD_W] = xt.astype(BF16)


def _mix_in(x, g, w, seq_len, pos0):
    m = x.shape[0]
    tm = min(ROW_TILE, m)
    n_tab = max(seq_len, tm)
    pos = pos0 + (jnp.arange(n_tab, dtype=jnp.int32) % seq_len)
    tables = _rope_tables(pos)
    n_tab_blocks = n_tab // tm
    row = pl.BlockSpec((tm, D_MODEL), lambda i: (i, 0))
    tab = pl.BlockSpec((tm, HEAD_W), lambda i: (i % n_tab_blocks, 0))
    transpose_k = seq_len % tm == 0
    if transpose_k:
        n_seq_blocks = seq_len // tm
        k_spec = pl.BlockSpec((None, GROUP_W, tm), lambda i: (i // n_seq_blocks, 0, i % n_seq_blocks))
        k_shape = jax.ShapeDtypeStruct((m // seq_len, GROUP_W, seq_len), F32)
    else:
        k_spec = pl.BlockSpec((tm, GROUP_W), lambda i: (i, 0))
        k_shape = jax.ShapeDtypeStruct((m, GROUP_W), F32)
    p, k_new, v_new = pl.pallas_call(
        functools.partial(_mix_in_kernel, transpose_k=transpose_k),
        grid=(m // tm,),
        in_specs=[row, _resident((1, D_MODEL)), _resident((D_MODEL, IN_COLS)), tab, tab, tab, tab, tab],
        out_specs=[pl.BlockSpec((tm, IN_COLS), lambda i: (i, 0)), k_spec,
                   pl.BlockSpec((tm, HEADS, HEAD_W), lambda i: (i, 0, 0))],
        out_shape=[jax.ShapeDtypeStruct((m, IN_COLS), BF16), k_shape,
                   jax.ShapeDtypeStruct((m, HEADS, HEAD_W), F32)],
        compiler_params=_params("parallel"),
        name="mix_in",
    )(x, g, w, *tables)
    b = m // seq_len
    if transpose_k:
        k_new = jnp.transpose(k_new.reshape(b, HEADS, 2, DIFF_DK, seq_len), (0, 4, 1, 2, 3))
    else:
        k_new = k_new.reshape(b, seq_len, HEADS, 2, DIFF_DK)
    return p.reshape(b, seq_len, IN_COLS), k_new, v_new.reshape(b, seq_len, HEADS, HEAD_W)


def _ret_tables(c):
    log_g = jnp.log(1.0 - jnp.power(2.0, -5.0 - jnp.arange(HEADS, dtype=F32)))
    idx = jnp.arange(c, dtype=F32)
    rel = idx[:, None] - idx[None, :]
    dmask = jnp.where(rel >= 0, jnp.exp(log_g[:, None, None] * jnp.maximum(rel, 0.0)), 0.0)
    q_dec = jnp.exp(log_g[:, None] * (idx + 1.0)[None, :])
    k_dec = jnp.exp(log_g[:, None] * (c - 1.0 - idx)[None, :])
    b_dec = jnp.exp(log_g * c)
    bcast = lambda a: jnp.broadcast_to(a[:, :, None], (HEADS, c, HEAD_W))
    return dmask, bcast(q_dec), bcast(k_dec), jnp.broadcast_to(b_dec[:, None, None], (HEADS, 8, HEAD_W))


def _ret_kernel(q_ref, k_ref, v_ref, g_ref, s0_ref, dmask_ref, qdec_ref, kdec_ref, bdec_ref, gn_ref,
                y_ref, snew_ref, s_scr, *, c, n_c):
    t = pl.program_id(1)

    @pl.when(t == 0)
    def _():
        s_scr[...] = s0_ref[...]

    for h in range(HEADS):
        cols = slice(h * HEAD_W, (h + 1) * HEAD_W)
        state = s_scr[h]
        b_dec = bdec_ref[h][0:1, :]
        for ci in range(n_c):
            rows = slice(ci * c, (ci + 1) * c)
            q = q_ref[rows, cols]
            k = k_ref[rows, cols]
            v = v_ref[rows, cols]
            scores = lax.dot_general(q, k, _NT, preferred_element_type=F32) * dmask_ref[h]
            o = (jnp.dot(scores.astype(BF16), v, preferred_element_type=F32)
                 + qdec_ref[h] * jnp.dot(q, state.astype(BF16), preferred_element_type=F32))
            k_scaled = (k.astype(F32) * kdec_ref[h]).astype(BF16)
            state = b_dec * state + lax.dot_general(k_scaled, v, _TN, preferred_element_type=F32)
            mu = jnp.mean(o, axis=-1, keepdims=True)
            d = o - mu
            var = jnp.mean(d * d, axis=-1, keepdims=True)
            normed = d * lax.rsqrt(var + EPS) * gn_ref[:, cols]
            y_ref[rows, cols] = (_silu(g_ref[rows, cols].astype(F32)) * normed).astype(BF16)
        s_scr[h] = state

    @pl.when(t == pl.num_programs(1) - 1)
    def _():
        snew_ref[...] = s_scr[...]


def _retention(p, s0, gn, seq_len):
    b = p.shape[0]
    c = min(RET_CHUNK, seq_len)
    n_c = min(RET_BLOCKS_PER_STEP, seq_len // c)
    t_rows = c * n_c
    tables = _ret_tables(c)
    col = lambda g: pl.BlockSpec((None, t_rows, GROUP_W), lambda bi, ti, g=g: (bi, ti, g))
    state = pl.BlockSpec((None, HEADS, RET_DK, HEAD_W), lambda bi, ti: (bi, 0, 0, 0))
    return pl.pallas_call(
        functools.partial(_ret_kernel, c=c, n_c=n_c),
        grid=(b, seq_len // t_rows),
        in_specs=[col(0), col(1), col(2), col(3), state,
                  _resident((HEADS, c, c)), _resident((HEADS, c, HEAD_W)), _resident((HEADS, c, HEAD_W)),
                  _resident((HEADS, 8, HEAD_W)), _resident((1, GROUP_W))],
        out_specs=[pl.BlockSpec((None, t_rows, GROUP_W), lambda bi, ti: (bi, ti, 0)), state],
        out_shape=[jax.ShapeDtypeStruct((b, seq_len, GROUP_W), BF16),
                   jax.ShapeDtypeStruct((b, HEADS, RET_DK, HEAD_W), F32)],
        scratch_shapes=[pltpu.VMEM((HEADS, RET_DK, HEAD_W), F32)],
        compiler_params=_params("parallel", "arbitrary"),
        name="retention",
    )(p, p, p, p, s0, *tables, gn)


def _lambda(lq1_ref, lk1_ref, lq2_ref, lk2_ref, lam_init):
    return (jnp.exp(jnp.sum(lq1_ref[...] * lk1_ref[...], axis=-1, keepdims=True))
            - jnp.exp(jnp.sum(lq2_ref[...] * lk2_ref[...], axis=-1, keepdims=True)) + lam_init)


def _split_components(q):
    lane = lax.broadcasted_iota(jnp.int32, q.shape, 1)
    zero = jnp.zeros_like(q)
    return jnp.where(lane < DIFF_DK, q, zero), jnp.where(lane >= DIFF_DK, q, zero)


def _flash_update(s, v, mask, m_ref, l_ref, acc_ref):
    if mask is not None:
        s = jnp.where(mask, s, MASK_VALUE)
    pad = -s.shape[1] % HEAD_W
    if pad:
        s = jnp.concatenate([s, jnp.full((s.shape[0], pad), MASK_VALUE, F32)], axis=1)
        v = jnp.concatenate([v, jnp.zeros((pad, v.shape[1]), v.dtype)], axis=0)
    blocks = [s[:, b * HEAD_W:(b + 1) * HEAD_W] for b in range(s.shape[1] // HEAD_W)]
    m_prev = m_ref[...]
    m_new = jnp.maximum(m_prev, jnp.max(functools.reduce(jnp.maximum, blocks), axis=-1, keepdims=True))
    alpha = jnp.exp2(m_prev - m_new)
    probs = [jnp.exp2(blk - m_new) for blk in blocks]
    l_ref[...] = alpha * l_ref[...] + functools.reduce(jnp.add, probs)
    p = jnp.concatenate([x.astype(BF16) for x in probs], axis=1)
    acc_ref[...] = alpha * acc_ref[...] + jnp.dot(p, v, preferred_element_type=F32)
    m_ref[...] = m_new


def _flash_init(m_ref, l_ref, acc_ref):
    m_ref[...] = jnp.full(m_ref.shape, MASK_VALUE, F32)
    l_ref[...] = jnp.zeros(l_ref.shape, F32)
    acc_ref[...] = jnp.zeros(acc_ref.shape, F32)


def _flash_update_t(s_t, s_max, vt_ones, mask, m_ref, acc_ref):
    if mask is not None:
        s_t = jnp.where(mask, s_t, MASK_VALUE)
        s_max = jnp.max(s_t, axis=0, keepdims=True)
    m_prev = m_ref[...]
    m_new = jnp.maximum(m_prev, s_max)
    alpha = jnp.exp2(m_prev - m_new)
    p_t = jnp.exp2(s_t - m_new).astype(BF16)
    acc_ref[...] = alpha * acc_ref[...] + jnp.dot(vt_ones, p_t, preferred_element_type=F32)
    m_ref[...] = m_new


def _attn_prompt_kernel(lq1_ref, lk1_ref, lq2_ref, lk2_ref, gn_ref, q_ref, k_ref, v_ref, o_ref,
                        *scratch, tile, lam_init):
    i = pl.program_id(1)
    n_sub = tile // ATTN_Q_SUB
    n_groups = HEADS // ATTN_HEAD_GROUP
    chains = [(hh, qs, c) for hh in range(ATTN_HEAD_GROUP) for qs in range(n_sub) for c in range(2)]
    stats = [{ch: scratch[2 * (par * len(chains) + n):2 * (par * len(chains) + n) + 2]
              for n, ch in enumerate(chains)} for par in range(2)]
    base = 4 * len(chains)
    slots = [(scratch[base + n], scratch[base + 5 + n]) for n in range(4)]
    slot_pairs = (slots[0:2], slots[2:4])
    qt_scr = scratch[base + 4]
    lam = _lambda(lq1_ref, lk1_ref, lq2_ref, lk2_ref, lam_init)
    key_chunk = lax.broadcasted_iota(jnp.int32, (tile, tile), 0) // CHUNK
    query_chunk = lax.broadcasted_iota(jnp.int32, (tile, tile), 1) // CHUNK
    diag_mask = key_chunk <= query_chunk

    def head_cols(g, hh):
        h = g * ATTN_HEAD_GROUP + hh
        return slice(h * HEAD_W, (h + 1) * HEAD_W)

    def key_rows(j):
        return pl.ds(pl.multiple_of(j * tile, tile), tile)

    def scores_into(g, slot, j):
        for n, (hh, qs, c) in enumerate(chains):
            queries = slice(qs * ATTN_Q_SUB, (qs + 1) * ATTN_Q_SUB)
            s_t = jnp.dot(k_ref[key_rows(j), head_cols(g, hh)], qt_scr[g % 2, hh, c, :, queries],
                          preferred_element_type=F32)
            slot[0][n] = s_t
            slot[1][n] = jnp.max(s_t, axis=0, keepdims=True)

    def accumulate(g, slot, j, mask):
        ones = jnp.ones((SUM_ROWS, tile), BF16)
        vt_ones = [jnp.concatenate([v_ref[key_rows(j), head_cols(g, hh)].T, ones], axis=0)
                   for hh in range(ATTN_HEAD_GROUP)]
        for n, (hh, qs, c) in enumerate(chains):
            if mask is None:
                _flash_update_t(slot[0][n], slot[1][n], vt_ones[hh], None, *stats[g % 2][(hh, qs, c)])
            else:
                keys = (qs + 1) * ATTN_Q_SUB
                sub_mask = mask[:keys, qs * ATTN_Q_SUB:(qs + 1) * ATTN_Q_SUB]
                _flash_update_t(slot[0][n, :keys, :], None, vt_ones[hh][:, :keys], sub_mask,
                                *stats[g % 2][(hh, qs, c)])

    def start_group(g):
        if g < n_groups:
            for m_ref, acc_ref in stats[g % 2].values():
                m_ref[...] = jnp.full(m_ref.shape, MASK_VALUE, F32)
                acc_ref[...] = jnp.zeros(acc_ref.shape, F32)
            for hh in range(ATTN_HEAD_GROUP):
                for c, comp in enumerate(_split_components(q_ref[:, head_cols(g, hh)])):
                    qt_scr[g % 2, hh, c] = comp.T
            scores_into(g, slot_pairs[g % 2][0], 0)

    def finish_group(g):
        if g < 0:
            return
        for hh in range(ATTN_HEAD_GROUP):
            for qs in range(n_sub):
                (_, acc1_ref), (_, acc2_ref) = stats[g % 2][(hh, qs, 0)], stats[g % 2][(hh, qs, 1)]
                o_t = (acc1_ref[:HEAD_W] / acc1_ref[HEAD_W:HEAD_W + 1]
                       - lam * (acc2_ref[:HEAD_W] / acc2_ref[HEAD_W:HEAD_W + 1]))
                ms = jnp.mean(o_t * o_t, axis=0, keepdims=True)
                o_t = o_t * lax.rsqrt(ms + EPS) * gn_ref[...] * (1.0 - lam_init)
                o_ref[qs * ATTN_Q_SUB:(qs + 1) * ATTN_Q_SUB, head_cols(g, hh)] = o_t.T.astype(BF16)

    start_group(0)
    for g in range(n_groups):
        slot_a, slot_b = slot_pairs[g % 2]

        def body(jj, carry, g=g, slot_a=slot_a, slot_b=slot_b):
            scores_into(g, slot_b, 2 * jj + 1)
            accumulate(g, slot_a, 2 * jj, None)
            scores_into(g, slot_a, 2 * jj + 2)
            accumulate(g, slot_b, 2 * jj + 1, None)
            return carry

        lax.fori_loop(0, i // 2, body, 0)

        @pl.when(i % 2 == 0)
        def _(g=g, slot_a=slot_a):
            finish_group(g - 1)
            start_group(g + 1)
            accumulate(g, slot_a, i, diag_mask)

        @pl.when(i % 2 == 1)
        def _(g=g, slot_a=slot_a, slot_b=slot_b):
            scores_into(g, slot_b, i)
            accumulate(g, slot_a, i - 1, None)
            finish_group(g - 1)
            start_group(g + 1)
            accumulate(g, slot_b, i, diag_mask)

    finish_group(n_groups - 1)


def _attn_prompt(p, lq1, lk1, lq2, lk2, gn, lam_init):
    b, seq_len, _ = p.shape
    tile = ATTN_TILE
    n_chains = ATTN_HEAD_GROUP * 2 * tile // ATTN_Q_SUB
    small = lambda n: pl.BlockSpec((1, n), lambda bi, qi: (0, 0))
    kv = lambda g: pl.BlockSpec((None, seq_len, GROUP_W), lambda bi, qi, g=g: (bi, 0, g))
    return pl.pallas_call(
        functools.partial(_attn_prompt_kernel, tile=tile, lam_init=lam_init),
        grid=(b, seq_len // tile),
        in_specs=[small(DIFF_DK)] * 4 + [pl.BlockSpec((HEAD_W, 1), lambda bi, qi: (0, 0)),
                  pl.BlockSpec((None, tile, GROUP_W), lambda bi, qi: (bi, qi, 4)), kv(5), kv(6)],
        out_specs=pl.BlockSpec((None, tile, GROUP_W), lambda bi, qi: (bi, qi, 0)),
        out_shape=jax.ShapeDtypeStruct((b, seq_len, GROUP_W), BF16),
        scratch_shapes=[pltpu.VMEM((1, ATTN_Q_SUB), F32),
                        pltpu.VMEM((HEAD_W + SUM_ROWS, ATTN_Q_SUB), F32)] * (2 * n_chains)
                       + [pltpu.VMEM((n_chains, tile, ATTN_Q_SUB), F32)] * 4
                       + [pltpu.VMEM((2, ATTN_HEAD_GROUP, 2, HEAD_W, tile), BF16)]
                       + [pltpu.VMEM((n_chains, 1, ATTN_Q_SUB), F32)] * 4,
        compiler_params=_params("parallel", "arbitrary"),
        name="attn_prompt",
    )(lq1, lk1, lq2, lk2, gn.reshape(HEAD_W, 1), p, p, p)


def _attn_sample_kernel(lq1_ref, lk1_ref, lq2_ref, lk2_ref, gn_ref, q_ref, kp_ref, vp_ref, kn_ref, vn_ref,
                        o_ref, m_scr, l_scr, acc_scr, *, n_past, lam_init):
    j = pl.program_id(1)

    @pl.when(j == 0)
    def _():
        _flash_init(m_scr, l_scr, acc_scr)

    n_q = q_ref.shape[0]

    def update(scores_of, values):
        for h in range(HEADS):
            cols = slice(h * HEAD_W, (h + 1) * HEAD_W)
            q_both = jnp.concatenate(_split_components(q_ref[:, cols]), axis=0)
            _flash_update(scores_of(q_both, cols), values[:, cols].astype(BF16), None,
                          m_scr.at[h], l_scr.at[h], acc_scr.at[h])

    @pl.when(j < n_past)
    def _():
        update(lambda q, feats: jnp.dot(q, kp_ref[feats, :].astype(BF16), preferred_element_type=F32),
               vp_ref[...].reshape(vp_ref.shape[0], GROUP_W))

    @pl.when(j == n_past)
    def _():
        update(lambda q, feats: lax.dot_general(q, kn_ref[:, feats], _NT, preferred_element_type=F32),
               vn_ref[...])
        lam = _lambda(lq1_ref, lk1_ref, lq2_ref, lk2_ref, lam_init)
        for h in range(HEADS):
            l = jnp.sum(l_scr[h], axis=-1, keepdims=True)
            o_both = acc_scr[h] / l
            o = o_both[:n_q] - lam * o_both[n_q:]
            o_ref[:, h * HEAD_W:(h + 1) * HEAD_W] = (_rms(o, gn_ref[...]) * (1.0 - lam_init)).astype(BF16)


def _attn_sample(p, k_past, v_past, lq1, lk1, lq2, lk2, gn, lam_init):
    b, seq_len, _ = p.shape
    past_len = k_past.shape[1]
    tk = min(PAST_TILE, past_len)
    n_past = past_len // tk
    k_past = jnp.transpose(k_past, (0, 2, 3, 4, 1)).reshape(b, GROUP_W, past_len)
    small = lambda n: pl.BlockSpec((1, n), lambda bi, ji: (0, 0))
    new = lambda g: pl.BlockSpec((None, seq_len, GROUP_W), lambda bi, ji, g=g: (bi, 0, g))
    past_k = pl.BlockSpec((None, GROUP_W, tk), lambda bi, ji: (bi, 0, jnp.minimum(ji, n_past - 1)))
    past_v = pl.BlockSpec((None, tk, HEADS, HEAD_W), lambda bi, ji: (bi, jnp.minimum(ji, n_past - 1), 0, 0))
    return pl.pallas_call(
        functools.partial(_attn_sample_kernel, n_past=n_past, lam_init=lam_init),
        grid=(b, n_past + 1),
        in_specs=[small(DIFF_DK)] * 4 + [small(HEAD_W), new(4), past_k, past_v, new(5), new(6)],
        out_specs=pl.BlockSpec((None, seq_len, GROUP_W), lambda bi, ji: (bi, 0, 0)),
        out_shape=jax.ShapeDtypeStruct((b, seq_len, GROUP_W), BF16),
        scratch_shapes=[pltpu.VMEM((HEADS, 2 * seq_len, HEAD_W), F32)] * 3,
        compiler_params=_params("parallel", "arbitrary"),
        name="attn_sample",
    )(lq1, lk1, lq2, lk2, gn, p, k_past, v_past, p, p)


def _layer(x, ret_s0, k_past, v_past, w, lam_init):
    (f1_pre, f1_wg, f1_wu, f1_wd, f1_post, mix_pre, w_in, ret_g, lq1, lk1, lq2, lk2, diff_g, w_out,
     mix_post, f2_pre, f2_wg, f2_wu, f2_wd, f2_post) = w
    b, seq_len, _ = x.shape
    past_len = 0 if k_past is None else k_past.shape[1]
    x = x.reshape(b * seq_len, D_MODEL)
    x1 = _ffn(x, f1_pre, f1_wg, f1_wu, f1_wd, f1_post)
    p, k_new, v_new = _mix_in(x1, mix_pre, w_in, seq_len, past_len)
    ret_y, s_new = _retention(p, ret_s0, ret_g, seq_len)
    if k_past is None:
        diff_y = _attn_prompt(p, lq1, lk1, lq2, lk2, diff_g, lam_init)
    else:
        diff_y = _attn_sample(p, k_past, v_past, lq1, lk1, lq2, lk2, diff_g, lam_init)
    x3 = _out_ffn(x1, ret_y.reshape(b * seq_len, GROUP_W), diff_y.reshape(b * seq_len, GROUP_W),
                  w_out, mix_post, f2_pre, f2_wg, f2_wu, f2_wd, f2_post)
    return x3.reshape(b, seq_len, D_MODEL), s_new, k_new, v_new


def kernel(x_prompt, x_sample, state_ret, cache_diff_k, cache_diff_v, ffn1_pre_g, ffn1_w_gate, ffn1_w_up, ffn1_w_down, ffn1_post_g, mix_pre_g, w_in, ret_norm_g, diff_lq1, diff_lk1, diff_lq2, diff_lk2, diff_norm_g, w_out, mix_post_g, ffn2_pre_g, ffn2_w_gate, ffn2_w_up, ffn2_w_down, ffn2_post_g):
    depth = w_in.shape[0]
    xp, xs = x_prompt, x_sample
    outs = [[] for _ in range(6)]
    for li in range(depth):
        lam_init = 0.8 - 0.6 * math.exp(-0.3 * li)
        row = lambda a: a[li].reshape(1, -1)
        mat = lambda a: a[li].astype(BF16)
        w = (row(ffn1_pre_g), mat(ffn1_w_gate), mat(ffn1_w_up), mat(ffn1_w_down), row(ffn1_post_g),
             row(mix_pre_g), mat(w_in), row(ret_norm_g), row(diff_lq1), row(diff_lk1), row(diff_lq2),
             row(diff_lk2), row(diff_norm_g), mat(w_out), row(mix_post_g),
             row(ffn2_pre_g), mat(ffn2_w_gate), mat(ffn2_w_up), mat(ffn2_w_down), row(ffn2_post_g))
        s0 = jnp.zeros((xp.shape[0], HEADS, RET_DK, HEAD_W), F32)
        xp, s_p, k_p, v_p = _layer(xp, s0, None, None, w, lam_init)
        xs, s_s, k_s, v_s = _layer(xs, state_ret[li], cache_diff_k[li], cache_diff_v[li], w, lam_init)
        for acc, val in zip(outs, (s_p, k_p, v_p, s_s, k_s, v_s)):
            acc.append(val)
    return (xp, xs) + tuple(jnp.stack(o) for o in outs)
```

```python
import functools
import math

import jax
import jax.numpy as jnp
from jax import lax
from jax.experimental import pallas as pl
from jax.experimental.pallas import tpu as pltpu

D_MODEL = 1024
D_FF = 2816
CHUNK = 64
HEADS = 4
HEAD_W = 128
GROUP_W = HEADS * HEAD_W
N_GROUPS = 7
IN_COLS = N_GROUPS * GROUP_W
RET_DK = 128
DIFF_DK = 64
ROT_DIM = DIFF_DK // 4
ROPE_THETA = 500000.0
RET_THETA = 10000.0
EPS = 1e-6
MASK_VALUE = -1e30
LOG2E = math.log2(math.e)

ROW_TILE = 1024
FFN_ROW_TILE = 512
FFN_ROW_GROUPS = 2
RET_CHUNK = 256
RET_BLOCKS_PER_STEP = 8
ATTN_TILE = 512
ATTN_Q_SUB = 256
ATTN_HEAD_GROUP = 2
SUM_ROWS = 16
PAST_TILE = 2048
VMEM_LIMIT = 56 * 1024 * 1024

F32 = jnp.float32
BF16 = jnp.bfloat16
_NT = (((1,), (1,)), ((), ()))
_TN = (((0,), (0,)), ((), ()))


def _rms(x, g):
    return x * lax.rsqrt(jnp.mean(x * x, axis=-1, keepdims=True) + EPS) * g


def _silu(x):
    return x * (1.0 / (1.0 + jnp.exp(-x)))


def _resident(shape):
    nd = len(shape)
    return pl.BlockSpec(shape, lambda *_: (0,) * nd, pipeline_mode=pl.Buffered(1))


def _params(*sem):
    return pltpu.CompilerParams(dimension_semantics=sem, vmem_limit_bytes=VMEM_LIMIT)


def _ffn_math(x, pre_g, wg_ref, wu_ref, wd_ref, post_g):
    n = FFN_ROW_GROUPS if x.shape[0] % (8 * FFN_ROW_GROUPS) == 0 else 1
    rows = x.shape[0] // n
    xs = [x[g * rows:(g + 1) * rows] for g in range(n)]
    hs = [_rms(xg, pre_g).astype(BF16) for xg in xs]
    gates_ups = [(jnp.dot(h, wg_ref[...], preferred_element_type=F32),
                  jnp.dot(h, wu_ref[...], preferred_element_type=F32)) for h in hs]
    acts = [(_silu(gate) * up).astype(BF16) for gate, up in gates_ups]
    ys = [jnp.dot(act, wd_ref[...], preferred_element_type=F32) for act in acts]
    outs = [xg + 0.5 * _rms(y, post_g) for xg, y in zip(xs, ys)]
    return outs[0] if n == 1 else jnp.concatenate(outs, axis=0)


def _row_groups_call(body, groups, shared, name):
    tm = FFN_ROW_TILE
    tiles = [g[0].shape[0] // tm for g in groups]
    starts = [sum(tiles[:n]) for n in range(len(groups))]
    n_rows_in = sum(len(g) for g in groups)

    def kernel_fn(*refs):
        i = pl.program_id(0)
        shared_refs = refs[n_rows_in:n_rows_in + len(shared)]
        out_refs = refs[n_rows_in + len(shared):]
        first = 0
        for n, g in enumerate(groups):
            ins = refs[first:first + len(g)]
            first += len(g)

            @pl.when((i >= starts[n]) & (i < starts[n] + tiles[n]))
            def _(ins=ins, o_ref=out_refs[n]):
                o_ref[...] = body(*ins, *shared_refs)

    def rows_spec(width, n):
        return pl.BlockSpec((tm, width), lambda i: (jnp.clip(i - starts[n], 0, tiles[n] - 1), 0))

    return pl.pallas_call(
        kernel_fn,
        grid=(sum(tiles),),
        in_specs=[rows_spec(a.shape[1], n) for n, g in enumerate(groups) for a in g]
                 + [_resident(a.shape) for a in shared],
        out_specs=[rows_spec(D_MODEL, n) for n in range(len(groups))],
        out_shape=[jax.ShapeDtypeStruct((g[0].shape[0], D_MODEL), F32) for g in groups],
        compiler_params=_params("arbitrary"),
        name=name,
    )(*[a for g in groups for a in g], *shared)


def _ffn_body(x_ref, pre_ref, wg_ref, wu_ref, wd_ref, post_ref):
    return _ffn_math(x_ref[...], pre_ref[...], wg_ref, wu_ref, wd_ref, post_ref[...])


def _ffn(xs, pre_g, wg, wu, wd, post_g):
    return _row_groups_call(_ffn_body, [(x,) for x in xs], (pre_g, wg, wu, wd, post_g), "ffn")


def _out_ffn_body(x_ref, ry_ref, dy_ref, wo_ref, mixpost_ref, pre_ref, wg_ref, wu_ref, wd_ref, post_ref):
    y = (jnp.dot(ry_ref[...], wo_ref[:GROUP_W, :], preferred_element_type=F32)
         + jnp.dot(dy_ref[...], wo_ref[GROUP_W:, :], preferred_element_type=F32))
    x2 = x_ref[...] + _rms(y, mixpost_ref[...])
    return _ffn_math(x2, pre_ref[...], wg_ref, wu_ref, wd_ref, post_ref[...])


def _out_ffn(groups, wo, mixpost_g, pre_g, wg, wu, wd, post_g):
    return _row_groups_call(_out_ffn_body, groups, (wo, mixpost_g, pre_g, wg, wu, wd, post_g), "out_ffn")


def _rope_tables(pos):
    pos = pos.astype(F32)[:, None]
    half = RET_DK // 2
    inv = jnp.power(jnp.float32(RET_THETA), -jnp.arange(half, dtype=F32) * (2.0 / RET_DK))
    ang = pos * inv[None, :]
    cos, sin = jnp.cos(ang), jnp.sin(ang)
    cr = jnp.concatenate([cos, cos], axis=1)
    sr = jnp.concatenate([-sin, sin], axis=1)
    half = ROT_DIM // 2
    inv = jnp.power(jnp.float32(ROPE_THETA), -jnp.arange(half, dtype=F32) * (2.0 / ROT_DIM))
    ang = pos * inv[None, :]
    cos, sin = jnp.cos(ang), jnp.sin(ang)
    n = pos.shape[0]
    ones = jnp.ones((n, DIFF_DK - ROT_DIM), F32)
    zeros = jnp.zeros((n, DIFF_DK - ROT_DIM), F32)
    z8 = jnp.zeros((n, half), F32)
    c_comp = jnp.concatenate([cos, cos, ones], axis=1)
    lo_comp = jnp.concatenate([-sin, z8, zeros], axis=1)
    hi_comp = jnp.concatenate([z8, sin, zeros], axis=1)
    cd = jnp.concatenate([c_comp, c_comp], axis=1)
    s_lo = jnp.concatenate([lo_comp, lo_comp], axis=1)
    s_hi = jnp.concatenate([hi_comp, hi_comp], axis=1)
    return cr, sr, cd, s_lo, s_hi


def _mix_in_kernel(x_ref, g_ref, w_ref, cr_ref, sr_ref, cd_ref, slo_ref, shi_ref, p_ref, kd_ref, vd_ref,
                   *, transpose_k):
    h = _rms(x_ref[...], g_ref[...]).astype(BF16)
    for grp in (6, 5, 4, 1, 0, 2, 3):
        pg = jnp.dot(h, w_ref[:, grp * GROUP_W:(grp + 1) * GROUP_W], preferred_element_type=F32)
        if grp == 6:
            vd_ref[...] = pg.reshape(pg.shape[0], HEADS, HEAD_W)
        for t in range(HEADS):
            cols = slice(t * HEAD_W, (t + 1) * HEAD_W)
            xt = pg[:, cols]
            if grp in (0, 1):
                xt = xt * cr_ref[...] + pltpu.roll(xt, RET_DK // 2, 1) * sr_ref[...]
                if grp == 1:
                    xt = xt * (RET_DK ** -0.5)
            elif grp in (4, 5):
                xt = (xt * cd_ref[...] + pltpu.roll(xt, HEAD_W - ROT_DIM // 2, 1) * slo_ref[...]
                      + pltpu.roll(xt, ROT_DIM // 2, 1) * shi_ref[...])
                if grp == 5 and transpose_k:
                    kd_ref[cols, :] = xt.T
                elif grp == 5:
                    kd_ref[:, cols] = xt
                else:
                    xt = xt * (DIFF_DK ** -0.5 * LOG2E)
            p_ref[:, grp * GROUP_W + t * HEAD_W:grp * GROUP_W + (t + 1) * HEAD_W] = xt.astype(BF16)


def _mix_in(x, g, w, seq_len, pos0):
    m = x.shape[0]
    tm = min(ROW_TILE, m)
    n_tab = max(seq_len, tm)
    pos = pos0 + (jnp.arange(n_tab, dtype=jnp.int32) % seq_len)
    tables = _rope_tables(pos)
    n_tab_blocks = n_tab // tm
    row = pl.BlockSpec((tm, D_MODEL), lambda i: (i, 0))
    tab = pl.BlockSpec((tm, HEAD_W), lambda i: (i % n_tab_blocks, 0))
    transpose_k = seq_len % tm == 0
    if transpose_k:
        n_seq_blocks = seq_len // tm
        k_spec = pl.BlockSpec((None, GROUP_W, tm), lambda i: (i // n_seq_blocks, 0, i % n_seq_blocks))
        k_shape = jax.ShapeDtypeStruct((m // seq_len, GROUP_W, seq_len), F32)
    else:
        k_spec = pl.BlockSpec((tm, GROUP_W), lambda i: (i, 0))
        k_shape = jax.ShapeDtypeStruct((m, GROUP_W), F32)
    p, k_new, v_new = pl.pallas_call(
        functools.partial(_mix_in_kernel, transpose_k=transpose_k),
        grid=(m // tm,),
        in_specs=[row, _resident((1, D_MODEL)), _resident((D_MODEL, IN_COLS)), tab, tab, tab, tab, tab],
        out_specs=[pl.BlockSpec((tm, IN_COLS), lambda i: (i, 0)), k_spec,
                   pl.BlockSpec((tm, HEADS, HEAD_W), lambda i: (i, 0, 0))],
        out_shape=[jax.ShapeDtypeStruct((m, IN_COLS), BF16), k_shape,
                   jax.ShapeDtypeStruct((m, HEADS, HEAD_W), F32)],
        compiler_params=_params("parallel"),
        name="mix_in",
    )(x, g, w, *tables)
    b = m // seq_len
    if transpose_k:
        k_new = jnp.transpose(k_new.reshape(b, HEADS, 2, DIFF_DK, seq_len), (0, 4, 1, 2, 3))
    else:
        k_new = k_new.reshape(b, seq_len, HEADS, 2, DIFF_DK)
    return p.reshape(b, seq_len, IN_COLS), k_new, v_new.reshape(b, seq_len, HEADS, HEAD_W)


def _ret_tables(c):
    log_g = jnp.log(1.0 - jnp.power(2.0, -5.0 - jnp.arange(HEADS, dtype=F32)))
    idx = jnp.arange(c, dtype=F32)
    rel = idx[:, None] - idx[None, :]
    dmask = jnp.where(rel >= 0, jnp.exp(log_g[:, None, None] * jnp.maximum(rel, 0.0)), 0.0)
    q_dec = jnp.exp(log_g[:, None] * (idx + 1.0)[None, :])
    k_dec = jnp.exp(log_g[:, None] * (c - 1.0 - idx)[None, :])
    b_dec = jnp.exp(log_g * c)
    bcast = lambda a: jnp.broadcast_to(a[:, :, None], (HEADS, c, HEAD_W))
    return dmask, bcast(q_dec), bcast(k_dec), jnp.broadcast_to(b_dec[:, None, None], (HEADS, 8, HEAD_W))


def _ret_kernel(q_ref, k_ref, v_ref, g_ref, s0_ref, dmask_ref, qdec_ref, kdec_ref, bdec_ref, gn_ref,
                y_ref, snew_ref, s_scr, *, c, n_c):
    t = pl.program_id(1)

    @pl.when(t == 0)
    def _():
        s_scr[...] = s0_ref[...]

    for h in range(HEADS):
        cols = slice(h * HEAD_W, (h + 1) * HEAD_W)
        state = s_scr[h]
        b_dec = bdec_ref[h][0:1, :]
        for ci in range(n_c):
            rows = slice(ci * c, (ci + 1) * c)
            q = q_ref[rows, cols]
            k = k_ref[rows, cols]
            v = v_ref[rows, cols]
            scores = lax.dot_general(q, k, _NT, preferred_element_type=F32) * dmask_ref[h]
            o = (jnp.dot(scores.astype(BF16), v, preferred_element_type=F32)
                 + qdec_ref[h] * jnp.dot(q, state.astype(BF16), preferred_element_type=F32))
            k_scaled = (k.astype(F32) * kdec_ref[h]).astype(BF16)
            state = b_dec * state + lax.dot_general(k_scaled, v, _TN, preferred_element_type=F32)
            mu = jnp.mean(o, axis=-1, keepdims=True)
            d = o - mu
            var = jnp.mean(d * d, axis=-1, keepdims=True)
            normed = d * lax.rsqrt(var + EPS) * gn_ref[:, cols]
            y_ref[rows, cols] = (_silu(g_ref[rows, cols].astype(F32)) * normed).astype(BF16)
        s_scr[h] = state

    @pl.when(t == pl.num_programs(1) - 1)
    def _():
        snew_ref[...] = s_scr[...]


def _retention(p, s0, gn, seq_len):
    b = p.shape[0]
    c = min(RET_CHUNK, seq_len)
    n_c = min(RET_BLOCKS_PER_STEP, seq_len // c)
    t_rows = c * n_c
    tables = _ret_tables(c)
    col = lambda g: pl.BlockSpec((None, t_rows, GROUP_W), lambda bi, ti, g=g: (bi, ti, g))
    state = pl.BlockSpec((None, HEADS, RET_DK, HEAD_W), lambda bi, ti: (bi, 0, 0, 0))
    return pl.pallas_call(
        functools.partial(_ret_kernel, c=c, n_c=n_c),
        grid=(b, seq_len // t_rows),
        in_specs=[col(0), col(1), col(2), col(3), state,
                  _resident((HEADS, c, c)), _resident((HEADS, c, HEAD_W)), _resident((HEADS, c, HEAD_W)),
                  _resident((HEADS, 8, HEAD_W)), _resident((1, GROUP_W))],
        out_specs=[pl.BlockSpec((None, t_rows, GROUP_W), lambda bi, ti: (bi, ti, 0)), state],
        out_shape=[jax.ShapeDtypeStruct((b, seq_len, GROUP_W), BF16),
                   jax.ShapeDtypeStruct((b, HEADS, RET_DK, HEAD_W), F32)],
        scratch_shapes=[pltpu.VMEM((HEADS, RET_DK, HEAD_W), F32)],
        compiler_params=_params("parallel", "arbitrary"),
        name="retention",
    )(p, p, p, p, s0, *tables, gn)


def _lambda(lq1_ref, lk1_ref, lq2_ref, lk2_ref, lam_init):
    return (jnp.exp(jnp.sum(lq1_ref[...] * lk1_ref[...], axis=-1, keepdims=True))
            - jnp.exp(jnp.sum(lq2_ref[...] * lk2_ref[...], axis=-1, keepdims=True)) + lam_init)


def _split_components(q):
    lane = lax.broadcasted_iota(jnp.int32, q.shape, 1)
    zero = jnp.zeros_like(q)
    return jnp.where(lane < DIFF_DK, q, zero), jnp.where(lane >= DIFF_DK, q, zero)


def _flash_update(s, v, mask, m_ref, l_ref, acc_ref):
    if mask is not None:
        s = jnp.where(mask, s, MASK_VALUE)
    pad = -s.shape[1] % HEAD_W
    if pad:
        s = jnp.concatenate([s, jnp.full((s.shape[0], pad), MASK_VALUE, F32)], axis=1)
        v = jnp.concatenate([v, jnp.zeros((pad, v.shape[1]), v.dtype)], axis=0)
    blocks = [s[:, b * HEAD_W:(b + 1) * HEAD_W] for b in range(s.shape[1] // HEAD_W)]
    m_prev = m_ref[...]
    m_new = jnp.maximum(m_prev, jnp.max(functools.reduce(jnp.maximum, blocks), axis=-1, keepdims=True))
    alpha = jnp.exp2(m_prev - m_new)
    probs = [jnp.exp2(blk - m_new) for blk in blocks]
    l_ref[...] = alpha * l_ref[...] + functools.reduce(jnp.add, probs)
    p = jnp.concatenate([x.astype(BF16) for x in probs], axis=1)
    acc_ref[...] = alpha * acc_ref[...] + jnp.dot(p, v, preferred_element_type=F32)
    m_ref[...] = m_new


def _flash_init(m_ref, l_ref, acc_ref):
    m_ref[...] = jnp.full(m_ref.shape, MASK_VALUE, F32)
    l_ref[...] = jnp.zeros(l_ref.shape, F32)
    acc_ref[...] = jnp.zeros(acc_ref.shape, F32)


def _flash_update_t(s_t, s_max, vt_ones, mask, m_ref, acc_ref):
    if mask is not None:
        s_t = jnp.where(mask, s_t, MASK_VALUE)
        s_max = jnp.max(s_t, axis=0, keepdims=True)
    m_prev = m_ref[...]
    m_new = jnp.maximum(m_prev, s_max)
    alpha = jnp.exp2(m_prev - m_new)
    p_t = jnp.exp2(s_t - m_new).astype(BF16)
    acc_ref[...] = alpha * acc_ref[...] + jnp.dot(vt_ones, p_t, preferred_element_type=F32)
    m_ref[...] = m_new


def _attn_prompt_kernel(lq1_ref, lk1_ref, lq2_ref, lk2_ref, gn_ref, q_ref, k_ref, v_ref, o_ref,
                        *scratch, tile, lam_init):
    i = pl.program_id(1)
    n_sub = tile // ATTN_Q_SUB
    n_groups = HEADS // ATTN_HEAD_GROUP
    chains = [(hh, qs, c) for hh in range(ATTN_HEAD_GROUP) for qs in range(n_sub) for c in range(2)]
    stats = [{ch: scratch[2 * (par * len(chains) + n):2 * (par * len(chains) + n) + 2]
              for n, ch in enumerate(chains)} for par in range(2)]
    base = 4 * len(chains)
    slots = [(scratch[base + n], scratch[base + 5 + n]) for n in range(4)]
    slot_pairs = (slots[0:2], slots[2:4])
    qt_scr = scratch[base + 4]
    lam = _lambda(lq1_ref, lk1_ref, lq2_ref, lk2_ref, lam_init)
    key_chunk = lax.broadcasted_iota(jnp.int32, (tile, tile), 0) // CHUNK
    query_chunk = lax.broadcasted_iota(jnp.int32, (tile, tile), 1) // CHUNK
    diag_mask = key_chunk <= query_chunk

    def head_cols(g, hh):
        h = g * ATTN_HEAD_GROUP + hh
        return slice(h * HEAD_W, (h + 1) * HEAD_W)

    def key_rows(j):
        return pl.ds(pl.multiple_of(j * tile, tile), tile)

    def scores_into(g, slot, j):
        for n, (hh, qs, c) in enumerate(chains):
            queries = slice(qs * ATTN_Q_SUB, (qs + 1) * ATTN_Q_SUB)
            s_t = jnp.dot(k_ref[key_rows(j), head_cols(g, hh)], qt_scr[g % 2, hh, c, :, queries],
                          preferred_element_type=F32)
            slot[0][n] = s_t
            slot[1][n] = jnp.max(s_t, axis=0, keepdims=True)

    def accumulate(g, slot, j, mask):
        ones = jnp.ones((SUM_ROWS, tile), BF16)
        vt_ones = [jnp.concatenate([v_ref[key_rows(j), head_cols(g, hh)].T, ones], axis=0)
                   for hh in range(ATTN_HEAD_GROUP)]
        for n, (hh, qs, c) in enumerate(chains):
            if mask is None:
                _flash_update_t(slot[0][n], slot[1][n], vt_ones[hh], None, *stats[g % 2][(hh, qs, c)])
            else:
                keys = (qs + 1) * ATTN_Q_SUB
                sub_mask = mask[:keys, qs * ATTN_Q_SUB:(qs + 1) * ATTN_Q_SUB]
                _flash_update_t(slot[0][n, :keys, :], None, vt_ones[hh][:, :keys], sub_mask,
                                *stats[g % 2][(hh, qs, c)])

    def start_group(g):
        if g < n_groups:
            for m_ref, acc_ref in stats[g % 2].values():
                m_ref[...] = jnp.full(m_ref.shape, MASK_VALUE, F32)
                acc_ref[...] = jnp.zeros(acc_ref.shape, F32)
            for hh in range(ATTN_HEAD_GROUP):
                for c, comp in enumerate(_split_components(q_ref[:, head_cols(g, hh)])):
                    qt_scr[g % 2, hh, c] = comp.T
            scores_into(g, slot_pairs[g % 2][0], 0)

    def finish_group(g):
        if g < 0:
            return
        for hh in range(ATTN_HEAD_GROUP):
            for qs in range(n_sub):
                (_, acc1_ref), (_, acc2_ref) = stats[g % 2][(hh, qs, 0)], stats[g % 2][(hh, qs, 1)]
                o_t = (acc1_ref[:HEAD_W] / acc1_ref[HEAD_W:HEAD_W + 1]
                       - lam * (acc2_ref[:HEAD_W] / acc2_ref[HEAD_W:HEAD_W + 1]))
                ms = jnp.mean(o_t * o_t, axis=0, keepdims=True)
                o_t = o_t * lax.rsqrt(ms + EPS) * gn_ref[...] * (1.0 - lam_init)
                o_ref[qs * ATTN_Q_SUB:(qs + 1) * ATTN_Q_SUB, head_cols(g, hh)] = o_t.T.astype(BF16)

    start_group(0)
    for g in range(n_groups):
        slot_a, slot_b = slot_pairs[g % 2]

        def body(jj, carry, g=g, slot_a=slot_a, slot_b=slot_b):
            scores_into(g, slot_b, 2 * jj + 1)
            accumulate(g, slot_a, 2 * jj, None)
            scores_into(g, slot_a, 2 * jj + 2)
            accumulate(g, slot_b, 2 * jj + 1, None)
            return carry

        lax.fori_loop(0, i // 2, body, 0)

        @pl.when(i % 2 == 0)
        def _(g=g, slot_a=slot_a):
            finish_group(g - 1)
            start_group(g + 1)
            accumulate(g, slot_a, i, diag_mask)

        @pl.when(i % 2 == 1)
        def _(g=g, slot_a=slot_a, slot_b=slot_b):
            scores_into(g, slot_b, i)
            accumulate(g, slot_a, i - 1, None)
            finish_group(g - 1)
            start_group(g + 1)
            accumulate(g, slot_b, i, diag_mask)

    finish_group(n_groups - 1)


def _attn_prompt(p, lq1, lk1, lq2, lk2, gn, lam_init):
    b, seq_len, _ = p.shape
    tile = ATTN_TILE
    n_chains = ATTN_HEAD_GROUP * 2 * tile // ATTN_Q_SUB
    small = lambda n: pl.BlockSpec((1, n), lambda bi, qi: (0, 0))
    kv = lambda g: pl.BlockSpec((None, seq_len, GROUP_W), lambda bi, qi, g=g: (bi, 0, g))
    return pl.pallas_call(
        functools.partial(_attn_prompt_kernel, tile=tile, lam_init=lam_init),
        grid=(b, seq_len // tile),
        in_specs=[small(DIFF_DK)] * 4 + [pl.BlockSpec((HEAD_W, 1), lambda bi, qi: (0, 0)),
                  pl.BlockSpec((None, tile, GROUP_W), lambda bi, qi: (bi, qi, 4)), kv(5), kv(6)],
        out_specs=pl.BlockSpec((None, tile, GROUP_W), lambda bi, qi: (bi, qi, 0)),
        out_shape=jax.ShapeDtypeStruct((b, seq_len, GROUP_W), BF16),
        scratch_shapes=[pltpu.VMEM((1, ATTN_Q_SUB), F32),
                        pltpu.VMEM((HEAD_W + SUM_ROWS, ATTN_Q_SUB), F32)] * (2 * n_chains)
                       + [pltpu.VMEM((n_chains, tile, ATTN_Q_SUB), F32)] * 4
                       + [pltpu.VMEM((2, ATTN_HEAD_GROUP, 2, HEAD_W, tile), BF16)]
                       + [pltpu.VMEM((n_chains, 1, ATTN_Q_SUB), F32)] * 4,
        compiler_params=_params("parallel", "arbitrary"),
        name="attn_prompt",
    )(lq1, lk1, lq2, lk2, gn.reshape(HEAD_W, 1), p, p, p)


def _attn_sample_kernel(lq1_ref, lk1_ref, lq2_ref, lk2_ref, gn_ref, q_ref, kp_ref, vp_ref, kn_ref, vn_ref,
                        o_ref, m_scr, l_scr, acc_scr, *, n_past, lam_init):
    j = pl.program_id(1)

    @pl.when(j == 0)
    def _():
        _flash_init(m_scr, l_scr, acc_scr)

    n_q = q_ref.shape[0]

    def update(scores_of, values):
        for h in range(HEADS):
            cols = slice(h * HEAD_W, (h + 1) * HEAD_W)
            q_both = jnp.concatenate(_split_components(q_ref[:, cols]), axis=0)
            _flash_update(scores_of(q_both, cols), values[:, cols].astype(BF16), None,
                          m_scr.at[h], l_scr.at[h], acc_scr.at[h])

    @pl.when(j < n_past)
    def _():
        update(lambda q, feats: jnp.dot(q, kp_ref[feats, :].astype(BF16), preferred_element_type=F32),
               vp_ref[...].reshape(vp_ref.shape[0], GROUP_W))

    @pl.when(j == n_past)
    def _():
        update(lambda q, feats: lax.dot_general(q, kn_ref[:, feats], _NT, preferred_element_type=F32),
               vn_ref[...])
        lam = _lambda(lq1_ref, lk1_ref, lq2_ref, lk2_ref, lam_init)
        for h in range(HEADS):
            l = jnp.sum(l_scr[h], axis=-1, keepdims=True)
            o_both = acc_scr[h] / l
            o = o_both[:n_q] - lam * o_both[n_q:]
            o_ref[:, h * HEAD_W:(h + 1) * HEAD_W] = (_rms(o, gn_ref[...]) * (1.0 - lam_init)).astype(BF16)


def _attn_sample(p, k_past, v_past, lq1, lk1, lq2, lk2, gn, lam_init):
    b, seq_len, _ = p.shape
    past_len = k_past.shape[1]
    tk = min(PAST_TILE, past_len)
    n_past = past_len // tk
    k_past = jnp.transpose(k_past, (0, 2, 3, 4, 1)).reshape(b, GROUP_W, past_len)
    small = lambda n: pl.BlockSpec((1, n), lambda bi, ji: (0, 0))
    new = lambda g: pl.BlockSpec((None, seq_len, GROUP_W), lambda bi, ji, g=g: (bi, 0, g))
    past_k = pl.BlockSpec((None, GROUP_W, tk), lambda bi, ji: (bi, 0, jnp.minimum(ji, n_past - 1)))
    past_v = pl.BlockSpec((None, tk, HEADS, HEAD_W), lambda bi, ji: (bi, jnp.minimum(ji, n_past - 1), 0, 0))
    return pl.pallas_call(
        functools.partial(_attn_sample_kernel, n_past=n_past, lam_init=lam_init),
        grid=(b, n_past + 1),
        in_specs=[small(DIFF_DK)] * 4 + [small(HEAD_W), new(4), past_k, past_v, new(5), new(6)],
        out_specs=pl.BlockSpec((None, seq_len, GROUP_W), lambda bi, ji: (bi, 0, 0)),
        out_shape=jax.ShapeDtypeStruct((b, seq_len, GROUP_W), BF16),
        scratch_shapes=[pltpu.VMEM((HEADS, 2 * seq_len, HEAD_W), F32)] * 3,
        compiler_params=_params("parallel", "arbitrary"),
        name="attn_sample",
    )(lq1, lk1, lq2, lk2, gn, p, k_past, v_past, p, p)


def _mixers(x1, ret_s0, k_past, v_past, w, lam_init, b, seq_len):
    (_, _, _, _, _, mix_pre, w_in, ret_g, lq1, lk1, lq2, lk2, diff_g, *_) = w
    past_len = 0 if k_past is None else k_past.shape[1]
    p, k_new, v_new = _mix_in(x1, mix_pre, w_in, seq_len, past_len)
    ret_y, s_new = _retention(p, ret_s0, ret_g, seq_len)
    if k_past is None:
        diff_y = _attn_prompt(p, lq1, lk1, lq2, lk2, diff_g, lam_init)
    else:
        diff_y = _attn_sample(p, k_past, v_past, lq1, lk1, lq2, lk2, diff_g, lam_init)
    return (ret_y.reshape(b * seq_len, GROUP_W), diff_y.reshape(b * seq_len, GROUP_W)), (s_new, k_new, v_new)


def _layer(xs, states, pasts, w, lam_init):
    (f1_pre, f1_wg, f1_wu, f1_wd, f1_post, _, _, _, _, _, _, _, _, w_out,
     mix_post, f2_pre, f2_wg, f2_wu, f2_wd, f2_post) = w
    shapes = [x.shape[:2] for x in xs]
    x1s = _ffn([x.reshape(b * l, D_MODEL) for x, (b, l) in zip(xs, shapes)],
               f1_pre, f1_wg, f1_wu, f1_wd, f1_post)
    mixed, caches = [], []
    for x1, state, past, (b, l) in zip(x1s, states, pasts, shapes):
        ys, new = _mixers(x1, state, *(past or (None, None)), w, lam_init, b, l)
        mixed.append((x1,) + ys)
        caches.append(new)
    x3s = _out_ffn(mixed, w_out, mix_post, f2_pre, f2_wg, f2_wu, f2_wd, f2_post)
    return [x3.reshape(b, l, D_MODEL) for x3, (b, l) in zip(x3s, shapes)], caches


def kernel(x_prompt, x_sample, state_ret, cache_diff_k, cache_diff_v, ffn1_pre_g, ffn1_w_gate, ffn1_w_up, ffn1_w_down, ffn1_post_g, mix_pre_g, w_in, ret_norm_g, diff_lq1, diff_lk1, diff_lq2, diff_lk2, diff_norm_g, w_out, mix_post_g, ffn2_pre_g, ffn2_w_gate, ffn2_w_up, ffn2_w_down, ffn2_post_g):
    depth = w_in.shape[0]
    xp, xs = x_prompt, x_sample
    outs = [[] for _ in range(6)]
    for li in range(depth):
        lam_init = 0.8 - 0.6 * math.exp(-0.3 * li)
        row = lambda a: a[li].reshape(1, -1)
        mat = lambda a: a[li].astype(BF16)
        w = (row(ffn1_pre_g), mat(ffn1_w_gate), mat(ffn1_w_up), mat(ffn1_w_down), row(ffn1_post_g),
             row(mix_pre_g), mat(w_in), row(ret_norm_g), row(diff_lq1), row(diff_lk1), row(diff_lq2),
             row(diff_lk2), row(diff_norm_g), mat(w_out), row(mix_post_g),
             row(ffn2_pre_g), mat(ffn2_w_gate), mat(ffn2_w_up), mat(ffn2_w_down), row(ffn2_post_g))
        s0 = jnp.zeros((xp.shape[0], HEADS, RET_DK, HEAD_W), F32)
        (xp, xs), (new_p, new_s) = _layer((xp, xs), (s0, state_ret[li]),
                                          (None, (cache_diff_k[li], cache_diff_v[li])), w, lam_init)
        for acc, val in zip(outs, new_p + new_s):
            acc.append(val)
    return (xp, xs) + tuple(jnp.stack(o) for o in outs)
```

```python
import functools
import math

import jax
import jax.numpy as jnp
import numpy as np
from jax import lax
from jax.experimental import pallas as pl
from jax.experimental.pallas import tpu as pltpu

D_MODEL = 1024
D_FF = 2816
CHUNK = 64
HEADS = 4
HEAD_W = 128
GROUP_W = HEADS * HEAD_W
N_GROUPS = 7
IN_COLS = N_GROUPS * GROUP_W
RET_DK = 128
DIFF_DK = 64
ROT_DIM = DIFF_DK // 4
ROPE_THETA = 500000.0
RET_THETA = 10000.0
EPS = 1e-6
MASK_VALUE = -1e30
LOG2E = math.log2(math.e)

ROW_TILE = 1024
FFN_ROW_TILE = 512
FFN_ROW_GROUPS = 2
RET_CHUNK = 256
RET_BLOCKS_PER_STEP = 8
ATTN_TILE = 512
ATTN_Q_SUB = 256
ATTN_HEAD_GROUP = 2
SUM_ROWS = 16
PAST_TILE = 2048
VMEM_LIMIT = 56 * 1024 * 1024

F32 = jnp.float32
BF16 = jnp.bfloat16
_NT = (((1,), (1,)), ((), ()))
_TN = (((0,), (0,)), ((), ()))


def _rms(x, g):
    return x * lax.rsqrt(jnp.mean(x * x, axis=-1, keepdims=True) + EPS) * g


def _silu(x):
    return x * (1.0 / (1.0 + jnp.exp(-x)))


def _resident(shape):
    nd = len(shape)
    return pl.BlockSpec(shape, lambda *_: (0,) * nd, pipeline_mode=pl.Buffered(1))


def _params(*sem):
    return pltpu.CompilerParams(dimension_semantics=sem, vmem_limit_bytes=VMEM_LIMIT)


def _ffn_math(x, pre_g, wg_ref, wu_ref, wd_ref, post_g):
    n = FFN_ROW_GROUPS if x.shape[0] % (8 * FFN_ROW_GROUPS) == 0 else 1
    rows = x.shape[0] // n
    xs = [x[g * rows:(g + 1) * rows] for g in range(n)]
    hs = [_rms(xg, pre_g).astype(BF16) for xg in xs]
    gates_ups = [(jnp.dot(h, wg_ref[...], preferred_element_type=F32),
                  jnp.dot(h, wu_ref[...], preferred_element_type=F32)) for h in hs]
    acts = [(_silu(gate) * up).astype(BF16) for gate, up in gates_ups]
    ys = [jnp.dot(act, wd_ref[...], preferred_element_type=F32) for act in acts]
    outs = [xg + 0.5 * _rms(y, post_g) for xg, y in zip(xs, ys)]
    return outs[0] if n == 1 else jnp.concatenate(outs, axis=0)


def _row_groups_call(body, groups, shared, name):
    tm = FFN_ROW_TILE
    tiles = [g[0].shape[0] // tm for g in groups]
    starts = [sum(tiles[:n]) for n in range(len(groups))]
    n_rows_in = sum(len(g) for g in groups)

    def kernel_fn(*refs):
        i = pl.program_id(0)
        shared_refs = refs[n_rows_in:n_rows_in + len(shared)]
        out_refs = refs[n_rows_in + len(shared):]
        first = 0
        for n, g in enumerate(groups):
            ins = refs[first:first + len(g)]
            first += len(g)

            @pl.when((i >= starts[n]) & (i < starts[n] + tiles[n]))
            def _(ins=ins, o_ref=out_refs[n]):
                o_ref[...] = body(*ins, *shared_refs)

    def rows_spec(width, n):
        return pl.BlockSpec((tm, width), lambda i: (jnp.clip(i - starts[n], 0, tiles[n] - 1), 0))

    return pl.pallas_call(
        kernel_fn,
        grid=(sum(tiles),),
        in_specs=[rows_spec(a.shape[1], n) for n, g in enumerate(groups) for a in g]
                 + [_resident(a.shape) for a in shared],
        out_specs=[rows_spec(D_MODEL, n) for n in range(len(groups))],
        out_shape=[jax.ShapeDtypeStruct((g[0].shape[0], D_MODEL), F32) for g in groups],
        compiler_params=_params("arbitrary"),
        name=name,
    )(*[a for g in groups for a in g], *shared)


def _ffn_body(x_ref, pre_ref, wg_ref, wu_ref, wd_ref, post_ref):
    return _ffn_math(x_ref[...], pre_ref[...], wg_ref, wu_ref, wd_ref, post_ref[...])


def _ffn(xs, pre_g, wg, wu, wd, post_g):
    return _row_groups_call(_ffn_body, [(x,) for x in xs], (pre_g, wg, wu, wd, post_g), "ffn")


def _out_ffn_body(x_ref, ry_ref, dy_ref, wo_ref, mixpost_ref, pre_ref, wg_ref, wu_ref, wd_ref, post_ref):
    y = (jnp.dot(ry_ref[...], wo_ref[:GROUP_W, :], preferred_element_type=F32)
         + jnp.dot(dy_ref[...], wo_ref[GROUP_W:, :], preferred_element_type=F32))
    x2 = x_ref[...] + _rms(y, mixpost_ref[...])
    return _ffn_math(x2, pre_ref[...], wg_ref, wu_ref, wd_ref, post_ref[...])


def _out_ffn(groups, wo, mixpost_g, pre_g, wg, wu, wd, post_g):
    return _row_groups_call(_out_ffn_body, groups, (wo, mixpost_g, pre_g, wg, wu, wd, post_g), "out_ffn")


def _rope_tables(pos):
    pos = np.asarray(pos, np.float64)[:, None]
    half = RET_DK // 2
    ang = pos * np.power(RET_THETA, -np.arange(half, dtype=np.float64) * (2.0 / RET_DK))[None, :]
    cos, sin = np.cos(ang), np.sin(ang)
    cr = np.concatenate([cos, cos], axis=1)
    sr = np.concatenate([-sin, sin], axis=1)
    half = ROT_DIM // 2
    ang = pos * np.power(ROPE_THETA, -np.arange(half, dtype=np.float64) * (2.0 / ROT_DIM))[None, :]
    cos, sin = np.cos(ang), np.sin(ang)
    n = pos.shape[0]
    ones = np.ones((n, DIFF_DK - ROT_DIM))
    zeros = np.zeros((n, DIFF_DK - ROT_DIM))
    z8 = np.zeros((n, half))
    c_comp = np.concatenate([cos, cos, ones], axis=1)
    lo_comp = np.concatenate([-sin, z8, zeros], axis=1)
    hi_comp = np.concatenate([z8, sin, zeros], axis=1)
    tables = (cr, sr, np.concatenate([c_comp, c_comp], axis=1), np.concatenate([lo_comp, lo_comp], axis=1),
              np.concatenate([hi_comp, hi_comp], axis=1))
    return tuple(jnp.asarray(t, F32) for t in tables)


def _mix_in_kernel(x_ref, g_ref, w_ref, cr_ref, sr_ref, cd_ref, slo_ref, shi_ref, p_ref, kd_ref, vd_ref,
                   *, transpose_k):
    h = _rms(x_ref[...], g_ref[...]).astype(BF16)
    for grp in (6, 5, 4, 1, 0, 2, 3):
        pg = jnp.dot(h, w_ref[:, grp * GROUP_W:(grp + 1) * GROUP_W], preferred_element_type=F32)
        if grp == 6:
            vd_ref[...] = pg.reshape(pg.shape[0], HEADS, HEAD_W)
        for t in range(HEADS):
            cols = slice(t * HEAD_W, (t + 1) * HEAD_W)
            xt = pg[:, cols]
            if grp in (0, 1):
                xt = xt * cr_ref[...] + pltpu.roll(xt, RET_DK // 2, 1) * sr_ref[...]
                if grp == 1:
                    xt = xt * (RET_DK ** -0.5)
            elif grp in (4, 5):
                xt = (xt * cd_ref[...] + pltpu.roll(xt, HEAD_W - ROT_DIM // 2, 1) * slo_ref[...]
                      + pltpu.roll(xt, ROT_DIM // 2, 1) * shi_ref[...])
                if grp == 5 and transpose_k:
                    kd_ref[cols, :] = xt.T
                elif grp == 5:
                    kd_ref[:, cols] = xt
                else:
                    xt = xt * (DIFF_DK ** -0.5 * LOG2E)
            p_ref[:, grp * GROUP_W + t * HEAD_W:grp * GROUP_W + (t + 1) * HEAD_W] = xt.astype(BF16)


def _mix_in(x, g, w, seq_len, pos0):
    m = x.shape[0]
    tm = min(ROW_TILE, m)
    n_tab = max(seq_len, tm)
    pos = pos0 + (np.arange(n_tab) % seq_len)
    tables = _rope_tables(pos)
    n_tab_blocks = n_tab // tm
    row = pl.BlockSpec((tm, D_MODEL), lambda i: (i, 0))
    tab = pl.BlockSpec((tm, HEAD_W), lambda i: (i % n_tab_blocks, 0))
    transpose_k = seq_len % tm == 0
    if transpose_k:
        n_seq_blocks = seq_len // tm
        k_spec = pl.BlockSpec((None, GROUP_W, tm), lambda i: (i // n_seq_blocks, 0, i % n_seq_blocks))
        k_shape = jax.ShapeDtypeStruct((m // seq_len, GROUP_W, seq_len), F32)
    else:
        k_spec = pl.BlockSpec((tm, GROUP_W), lambda i: (i, 0))
        k_shape = jax.ShapeDtypeStruct((m, GROUP_W), F32)
    p, k_new, v_new = pl.pallas_call(
        functools.partial(_mix_in_kernel, transpose_k=transpose_k),
        grid=(m // tm,),
        in_specs=[row, _resident((1, D_MODEL)), _resident((D_MODEL, IN_COLS)), tab, tab, tab, tab, tab],
        out_specs=[pl.BlockSpec((tm, IN_COLS), lambda i: (i, 0)), k_spec,
                   pl.BlockSpec((tm, HEADS, HEAD_W), lambda i: (i, 0, 0))],
        out_shape=[jax.ShapeDtypeStruct((m, IN_COLS), BF16), k_shape,
                   jax.ShapeDtypeStruct((m, HEADS, HEAD_W), F32)],
        compiler_params=_params("parallel"),
        name="mix_in",
    )(x, g, w, *tables)
    b = m // seq_len
    if transpose_k:
        k_new = jnp.transpose(k_new.reshape(b, HEADS, 2, DIFF_DK, seq_len), (0, 4, 1, 2, 3))
    else:
        k_new = k_new.reshape(b, seq_len, HEADS, 2, DIFF_DK)
    return p.reshape(b, seq_len, IN_COLS), k_new, v_new.reshape(b, seq_len, HEADS, HEAD_W)


def _ret_tables(c):
    log_g = np.log(1.0 - np.power(2.0, -5.0 - np.arange(HEADS, dtype=np.float64)))
    idx = np.arange(c, dtype=np.float64)
    rel = idx[:, None] - idx[None, :]
    dmask = np.where(rel >= 0, np.exp(log_g[:, None, None] * np.maximum(rel, 0.0)), 0.0)
    q_dec = np.exp(log_g[:, None] * (idx + 1.0)[None, :])
    k_dec = np.exp(log_g[:, None] * (c - 1.0 - idx)[None, :])
    b_dec = np.exp(log_g * c)
    bcast = lambda a: np.broadcast_to(a[:, :, None], (HEADS, c, HEAD_W))
    tables = (dmask, bcast(q_dec), bcast(k_dec), np.broadcast_to(b_dec[:, None, None], (HEADS, 8, HEAD_W)))
    return tuple(jnp.asarray(t, F32) for t in tables)


def _ret_kernel(q_ref, k_ref, v_ref, g_ref, s0_ref, dmask_ref, qdec_ref, kdec_ref, bdec_ref, gn_ref,
                y_ref, snew_ref, s_scr, *, c, n_c):
    t = pl.program_id(1)

    @pl.when(t == 0)
    def _():
        s_scr[...] = s0_ref[...]

    for h in range(HEADS):
        cols = slice(h * HEAD_W, (h + 1) * HEAD_W)
        state = s_scr[h]
        b_dec = bdec_ref[h][0:1, :]
        for ci in range(n_c):
            rows = slice(ci * c, (ci + 1) * c)
            q = q_ref[rows, cols]
            k = k_ref[rows, cols]
            v = v_ref[rows, cols]
            scores = lax.dot_general(q, k, _NT, preferred_element_type=F32) * dmask_ref[h]
            o = (jnp.dot(scores.astype(BF16), v, preferred_element_type=F32)
                 + qdec_ref[h] * jnp.dot(q, state.astype(BF16), preferred_element_type=F32))
            k_scaled = (k.astype(F32) * kdec_ref[h]).astype(BF16)
            state = b_dec * state + lax.dot_general(k_scaled, v, _TN, preferred_element_type=F32)
            mu = jnp.mean(o, axis=-1, keepdims=True)
            d = o - mu
            var = jnp.mean(d * d, axis=-1, keepdims=True)
            normed = d * lax.rsqrt(var + EPS) * gn_ref[:, cols]
            y_ref[rows, cols] = (_silu(g_ref[rows, cols].astype(F32)) * normed).astype(BF16)
        s_scr[h] = state

    @pl.when(t == pl.num_programs(1) - 1)
    def _():
        snew_ref[...] = s_scr[...]


def _retention(p, s0, gn, seq_len):
    b = p.shape[0]
    c = min(RET_CHUNK, seq_len)
    n_c = min(RET_BLOCKS_PER_STEP, seq_len // c)
    t_rows = c * n_c
    tables = _ret_tables(c)
    col = lambda g: pl.BlockSpec((None, t_rows, GROUP_W), lambda bi, ti, g=g: (bi, ti, g))
    state = pl.BlockSpec((None, HEADS, RET_DK, HEAD_W), lambda bi, ti: (bi, 0, 0, 0))
    return pl.pallas_call(
        functools.partial(_ret_kernel, c=c, n_c=n_c),
        grid=(b, seq_len // t_rows),
        in_specs=[col(0), col(1), col(2), col(3), state,
                  _resident((HEADS, c, c)), _resident((HEADS, c, HEAD_W)), _resident((HEADS, c, HEAD_W)),
                  _resident((HEADS, 8, HEAD_W)), _resident((1, GROUP_W))],
        out_specs=[pl.BlockSpec((None, t_rows, GROUP_W), lambda bi, ti: (bi, ti, 0)), state],
        out_shape=[jax.ShapeDtypeStruct((b, seq_len, GROUP_W), BF16),
                   jax.ShapeDtypeStruct((b, HEADS, RET_DK, HEAD_W), F32)],
        scratch_shapes=[pltpu.VMEM((HEADS, RET_DK, HEAD_W), F32)],
        compiler_params=_params("parallel", "arbitrary"),
        name="retention",
    )(p, p, p, p, s0, *tables, gn)


def _lambda(lq1_ref, lk1_ref, lq2_ref, lk2_ref, lam_init):
    return (jnp.exp(jnp.sum(lq1_ref[...] * lk1_ref[...], axis=-1, keepdims=True))
            - jnp.exp(jnp.sum(lq2_ref[...] * lk2_ref[...], axis=-1, keepdims=True)) + lam_init)


def _split_components(q):
    lane = lax.broadcasted_iota(jnp.int32, q.shape, 1)
    zero = jnp.zeros_like(q)
    return jnp.where(lane < DIFF_DK, q, zero), jnp.where(lane >= DIFF_DK, q, zero)


def _flash_update(s, v, mask, m_ref, l_ref, acc_ref):
    if mask is not None:
        s = jnp.where(mask, s, MASK_VALUE)
    pad = -s.shape[1] % HEAD_W
    if pad:
        s = jnp.concatenate([s, jnp.full((s.shape[0], pad), MASK_VALUE, F32)], axis=1)
        v = jnp.concatenate([v, jnp.zeros((pad, v.shape[1]), v.dtype)], axis=0)
    blocks = [s[:, b * HEAD_W:(b + 1) * HEAD_W] for b in range(s.shape[1] // HEAD_W)]
    m_prev = m_ref[...]
    m_new = jnp.maximum(m_prev, jnp.max(functools.reduce(jnp.maximum, blocks), axis=-1, keepdims=True))
    alpha = jnp.exp2(m_prev - m_new)
    probs = [jnp.exp2(blk - m_new) for blk in blocks]
    l_ref[...] = alpha * l_ref[...] + functools.reduce(jnp.add, probs)
    p = jnp.concatenate([x.astype(BF16) for x in probs], axis=1)
    acc_ref[...] = alpha * acc_ref[...] + jnp.dot(p, v, preferred_element_type=F32)
    m_ref[...] = m_new


def _flash_init(m_ref, l_ref, acc_ref):
    m_ref[...] = jnp.full(m_ref.shape, MASK_VALUE, F32)
    l_ref[...] = jnp.zeros(l_ref.shape, F32)
    acc_ref[...] = jnp.zeros(acc_ref.shape, F32)


def _flash_update_t(s_t, s_max, vt_ones, mask, m_ref, acc_ref):
    if mask is not None:
        s_t = jnp.where(mask, s_t, MASK_VALUE)
        s_max = jnp.max(s_t, axis=0, keepdims=True)
    m_prev = m_ref[...]
    m_new = jnp.maximum(m_prev, s_max)
    alpha = jnp.exp2(m_prev - m_new)
    p_t = jnp.exp2(s_t - m_new).astype(BF16)
    acc_ref[...] = alpha * acc_ref[...] + jnp.dot(vt_ones, p_t, preferred_element_type=F32)
    m_ref[...] = m_new


def _attn_prompt_kernel(lq1_ref, lk1_ref, lq2_ref, lk2_ref, gn_ref, q_ref, k_ref, v_ref, o_ref,
                        *scratch, tile, lam_init):
    i = pl.program_id(1)
    n_sub = tile // ATTN_Q_SUB
    n_groups = HEADS // ATTN_HEAD_GROUP
    chains = [(hh, qs, c) for hh in range(ATTN_HEAD_GROUP) for qs in range(n_sub) for c in range(2)]
    stats = [{ch: scratch[2 * (par * len(chains) + n):2 * (par * len(chains) + n) + 2]
              for n, ch in enumerate(chains)} for par in range(2)]
    base = 4 * len(chains)
    slots = [(scratch[base + n], scratch[base + 5 + n]) for n in range(4)]
    slot_pairs = (slots[0:2], slots[2:4])
    qt_scr = scratch[base + 4]
    lam = _lambda(lq1_ref, lk1_ref, lq2_ref, lk2_ref, lam_init)
    key_chunk = lax.broadcasted_iota(jnp.int32, (tile, tile), 0) // CHUNK
    query_chunk = lax.broadcasted_iota(jnp.int32, (tile, tile), 1) // CHUNK
    diag_mask = key_chunk <= query_chunk

    def head_cols(g, hh):
        h = g * ATTN_HEAD_GROUP + hh
        return slice(h * HEAD_W, (h + 1) * HEAD_W)

    def key_rows(j):
        return pl.ds(pl.multiple_of(j * tile, tile), tile)

    def scores_into(g, slot, j):
        for n, (hh, qs, c) in enumerate(chains):
            queries = slice(qs * ATTN_Q_SUB, (qs + 1) * ATTN_Q_SUB)
            s_t = jnp.dot(k_ref[key_rows(j), head_cols(g, hh)], qt_scr[g % 2, hh, c, :, queries],
                          preferred_element_type=F32)
            slot[0][n] = s_t
            slot[1][n] = jnp.max(s_t, axis=0, keepdims=True)

    def accumulate(g, slot, j, mask):
        ones = jnp.ones((SUM_ROWS, tile), BF16)
        vt_ones = [jnp.concatenate([v_ref[key_rows(j), head_cols(g, hh)].T, ones], axis=0)
                   for hh in range(ATTN_HEAD_GROUP)]
        for n, (hh, qs, c) in enumerate(chains):
            if mask is None:
                _flash_update_t(slot[0][n], slot[1][n], vt_ones[hh], None, *stats[g % 2][(hh, qs, c)])
            else:
                keys = (qs + 1) * ATTN_Q_SUB
                sub_mask = mask[:keys, qs * ATTN_Q_SUB:(qs + 1) * ATTN_Q_SUB]
                _flash_update_t(slot[0][n, :keys, :], None, vt_ones[hh][:, :keys], sub_mask,
                                *stats[g % 2][(hh, qs, c)])

    def start_group(g):
        if g < n_groups:
            for m_ref, acc_ref in stats[g % 2].values():
                m_ref[...] = jnp.full(m_ref.shape, MASK_VALUE, F32)
                acc_ref[...] = jnp.zeros(acc_ref.shape, F32)
            for hh in range(ATTN_HEAD_GROUP):
                for c, comp in enumerate(_split_components(q_ref[:, head_cols(g, hh)])):
                    qt_scr[g % 2, hh, c] = comp.T
            scores_into(g, slot_pairs[g % 2][0], 0)

    def finish_group(g):
        if g < 0:
            return
        for hh in range(ATTN_HEAD_GROUP):
            for qs in range(n_sub):
                (_, acc1_ref), (_, acc2_ref) = stats[g % 2][(hh, qs, 0)], stats[g % 2][(hh, qs, 1)]
                o_t = (acc1_ref[:HEAD_W] / acc1_ref[HEAD_W:HEAD_W + 1]
                       - lam * (acc2_ref[:HEAD_W] / acc2_ref[HEAD_W:HEAD_W + 1]))
                ms = jnp.mean(o_t * o_t, axis=0, keepdims=True)
                o_t = o_t * lax.rsqrt(ms + EPS) * gn_ref[...] * (1.0 - lam_init)
                o_ref[qs * ATTN_Q_SUB:(qs + 1) * ATTN_Q_SUB, head_cols(g, hh)] = o_t.T.astype(BF16)

    start_group(0)
    for g in range(n_groups):
        slot_a, slot_b = slot_pairs[g % 2]

        def body(jj, carry, g=g, slot_a=slot_a, slot_b=slot_b):
            scores_into(g, slot_b, 2 * jj + 1)
            accumulate(g, slot_a, 2 * jj, None)
            scores_into(g, slot_a, 2 * jj + 2)
            accumulate(g, slot_b, 2 * jj + 1, None)
            return carry

        lax.fori_loop(0, i // 2, body, 0)

        @pl.when(i % 2 == 0)
        def _(g=g, slot_a=slot_a):
            finish_group(g - 1)
            start_group(g + 1)
            accumulate(g, slot_a, i, diag_mask)

        @pl.when(i % 2 == 1)
        def _(g=g, slot_a=slot_a, slot_b=slot_b):
            scores_into(g, slot_b, i)
            accumulate(g, slot_a, i - 1, None)
            finish_group(g - 1)
            start_group(g + 1)
            accumulate(g, slot_b, i, diag_mask)

    finish_group(n_groups - 1)


def _attn_prompt(p, lq1, lk1, lq2, lk2, gn, lam_init):
    b, seq_len, _ = p.shape
    tile = ATTN_TILE
    n_chains = ATTN_HEAD_GROUP * 2 * tile // ATTN_Q_SUB
    small = lambda n: pl.BlockSpec((1, n), lambda bi, qi: (0, 0))
    kv = lambda g: pl.BlockSpec((None, seq_len, GROUP_W), lambda bi, qi, g=g: (bi, 0, g))
    return pl.pallas_call(
        functools.partial(_attn_prompt_kernel, tile=tile, lam_init=lam_init),
        grid=(b, seq_len // tile),
        in_specs=[small(DIFF_DK)] * 4 + [pl.BlockSpec((HEAD_W, 1), lambda bi, qi: (0, 0)),
                  pl.BlockSpec((None, tile, GROUP_W), lambda bi, qi: (bi, qi, 4)), kv(5), kv(6)],
        out_specs=pl.BlockSpec((None, tile, GROUP_W), lambda bi, qi: (bi, qi, 0)),
        out_shape=jax.ShapeDtypeStruct((b, seq_len, GROUP_W), BF16),
        scratch_shapes=[pltpu.VMEM((1, ATTN_Q_SUB), F32),
                        pltpu.VMEM((HEAD_W + SUM_ROWS, ATTN_Q_SUB), F32)] * (2 * n_chains)
                       + [pltpu.VMEM((n_chains, tile, ATTN_Q_SUB), F32)] * 4
                       + [pltpu.VMEM((2, ATTN_HEAD_GROUP, 2, HEAD_W, tile), BF16)]
                       + [pltpu.VMEM((n_chains, 1, ATTN_Q_SUB), F32)] * 4,
        compiler_params=_params("parallel", "arbitrary"),
        name="attn_prompt",
    )(lq1, lk1, lq2, lk2, gn.reshape(HEAD_W, 1), p, p, p)


def _attn_sample_kernel(lq1_ref, lk1_ref, lq2_ref, lk2_ref, gn_ref, q_ref, kp_ref, vp_ref, kn_ref, vn_ref,
                        o_ref, m_scr, l_scr, acc_scr, *, n_past, lam_init):
    j = pl.program_id(1)

    @pl.when(j == 0)
    def _():
        _flash_init(m_scr, l_scr, acc_scr)

    n_q = q_ref.shape[0]

    def update(scores_of, values):
        for h in range(HEADS):
            cols = slice(h * HEAD_W, (h + 1) * HEAD_W)
            q_both = jnp.concatenate(_split_components(q_ref[:, cols]), axis=0)
            _flash_update(scores_of(q_both, cols), values[:, cols].astype(BF16), None,
                          m_scr.at[h], l_scr.at[h], acc_scr.at[h])

    @pl.when(j < n_past)
    def _():
        update(lambda q, feats: jnp.dot(q, kp_ref[feats, :].astype(BF16), preferred_element_type=F32),
               vp_ref[...].reshape(vp_ref.shape[0], GROUP_W))

    @pl.when(j == n_past)
    def _():
        update(lambda q, feats: lax.dot_general(q, kn_ref[:, feats], _NT, preferred_element_type=F32),
               vn_ref[...])
        lam = _lambda(lq1_ref, lk1_ref, lq2_ref, lk2_ref, lam_init)
        for h in range(HEADS):
            l = jnp.sum(l_scr[h], axis=-1, keepdims=True)
            o_both = acc_scr[h] / l
            o = o_both[:n_q] - lam * o_both[n_q:]
            o_ref[:, h * HEAD_W:(h + 1) * HEAD_W] = (_rms(o, gn_ref[...]) * (1.0 - lam_init)).astype(BF16)


def _attn_sample(p, k_past, v_past, lq1, lk1, lq2, lk2, gn, lam_init):
    b, seq_len, _ = p.shape
    past_len = k_past.shape[1]
    tk = min(PAST_TILE, past_len)
    n_past = past_len // tk
    k_past = jnp.transpose(k_past, (0, 2, 3, 4, 1)).reshape(b, GROUP_W, past_len)
    small = lambda n: pl.BlockSpec((1, n), lambda bi, ji: (0, 0))
    new = lambda g: pl.BlockSpec((None, seq_len, GROUP_W), lambda bi, ji, g=g: (bi, 0, g))
    past_k = pl.BlockSpec((None, GROUP_W, tk), lambda bi, ji: (bi, 0, jnp.minimum(ji, n_past - 1)))
    past_v = pl.BlockSpec((None, tk, HEADS, HEAD_W), lambda bi, ji: (bi, jnp.minimum(ji, n_past - 1), 0, 0))
    return pl.pallas_call(
        functools.partial(_attn_sample_kernel, n_past=n_past, lam_init=lam_init),
        grid=(b, n_past + 1),
        in_specs=[small(DIFF_DK)] * 4 + [small(HEAD_W), new(4), past_k, past_v, new(5), new(6)],
        out_specs=pl.BlockSpec((None, seq_len, GROUP_W), lambda bi, ji: (bi, 0, 0)),
        out_shape=jax.ShapeDtypeStruct((b, seq_len, GROUP_W), BF16),
        scratch_shapes=[pltpu.VMEM((HEADS, 2 * seq_len, HEAD_W), F32)] * 3,
        compiler_params=_params("parallel", "arbitrary"),
        name="attn_sample",
    )(lq1, lk1, lq2, lk2, gn, p, k_past, v_past, p, p)


def _mixers(x1, ret_s0, k_past, v_past, w, lam_init, b, seq_len):
    (_, _, _, _, _, mix_pre, w_in, ret_g, lq1, lk1, lq2, lk2, diff_g, *_) = w
    past_len = 0 if k_past is None else k_past.shape[1]
    p, k_new, v_new = _mix_in(x1, mix_pre, w_in, seq_len, past_len)
    ret_y, s_new = _retention(p, ret_s0, ret_g, seq_len)
    if k_past is None:
        diff_y = _attn_prompt(p, lq1, lk1, lq2, lk2, diff_g, lam_init)
    else:
        diff_y = _attn_sample(p, k_past, v_past, lq1, lk1, lq2, lk2, diff_g, lam_init)
    return (ret_y.reshape(b * seq_len, GROUP_W), diff_y.reshape(b * seq_len, GROUP_W)), (s_new, k_new, v_new)


def _layer(xs, states, pasts, w, lam_init):
    (f1_pre, f1_wg, f1_wu, f1_wd, f1_post, _, _, _, _, _, _, _, _, w_out,
     mix_post, f2_pre, f2_wg, f2_wu, f2_wd, f2_post) = w
    shapes = [x.shape[:2] for x in xs]
    x1s = _ffn([x.reshape(b * l, D_MODEL) for x, (b, l) in zip(xs, shapes)],
               f1_pre, f1_wg, f1_wu, f1_wd, f1_post)
    mixed, caches = [], []
    for x1, state, past, (b, l) in zip(x1s, states, pasts, shapes):
        ys, new = _mixers(x1, state, *(past or (None, None)), w, lam_init, b, l)
        mixed.append((x1,) + ys)
        caches.append(new)
    x3s = _out_ffn(mixed, w_out, mix_post, f2_pre, f2_wg, f2_wu, f2_wd, f2_post)
    return [x3.reshape(b, l, D_MODEL) for x3, (b, l) in zip(x3s, shapes)], caches


def kernel(x_prompt, x_sample, state_ret, cache_diff_k, cache_diff_v, ffn1_pre_g, ffn1_w_gate, ffn1_w_up, ffn1_w_down, ffn1_post_g, mix_pre_g, w_in, ret_norm_g, diff_lq1, diff_lk1, diff_lq2, diff_lk2, diff_norm_g, w_out, mix_post_g, ffn2_pre_g, ffn2_w_gate, ffn2_w_up, ffn2_w_down, ffn2_post_g):
    depth = w_in.shape[0]
    xp, xs = x_prompt, x_sample
    outs = [[] for _ in range(6)]
    for li in range(depth):
        lam_init = 0.8 - 0.6 * math.exp(-0.3 * li)
        row = lambda a: a[li].reshape(1, -1)
        mat = lambda a: a[li].astype(BF16)
        w = (row(ffn1_pre_g), mat(ffn1_w_gate), mat(ffn1_w_up), mat(ffn1_w_down), row(ffn1_post_g),
             row(mix_pre_g), mat(w_in), row(ret_norm_g), row(diff_lq1), row(diff_lk1), row(diff_lq2),
             row(diff_lk2), row(diff_norm_g), mat(w_out), row(mix_post_g),
             row(ffn2_pre_g), mat(ffn2_w_gate), mat(ffn2_w_up), mat(ffn2_w_down), row(ffn2_post_g))
        s0 = jnp.zeros((xp.shape[0], HEADS, RET_DK, HEAD_W), F32)
        (xp, xs), (new_p, new_s) = _layer((xp, xs), (s0, state_ret[li]),
                                          (None, (cache_diff_k[li], cache_diff_v[li])), w, lam_init)
        for acc, val in zip(outs, new_p + new_s):
            acc.append(val)
    return (xp, xs) + tuple(jnp.stack(o) for o in outs)
```

```python
import functools
import math

import jax
import jax.numpy as jnp
import numpy as np
from jax import lax
from jax.experimental import pallas as pl
from jax.experimental.pallas import tpu as pltpu

D_MODEL = 1024
D_FF = 2816
CHUNK = 64
HEADS = 4
HEAD_W = 128
GROUP_W = HEADS * HEAD_W
N_GROUPS = 7
IN_COLS = N_GROUPS * GROUP_W
RET_DK = 128
DIFF_DK = 64
ROT_DIM = DIFF_DK // 4
ROPE_THETA = 500000.0
RET_THETA = 10000.0
EPS = 1e-6
MASK_VALUE = -1e30
LOG2E = math.log2(math.e)

ROW_TILE = 1024
FFN_ROW_TILE = 512
FFN_ROW_GROUPS = 2
RET_CHUNK = 256
RET_BLOCKS_PER_STEP = 8
ATTN_TILE = 512
ATTN_Q_SUB = 256
ATTN_HEAD_GROUP = 2
SUM_ROWS = 16
PAST_TILE = 2048
VMEM_LIMIT = 56 * 1024 * 1024

F32 = jnp.float32
BF16 = jnp.bfloat16
_NT = (((1,), (1,)), ((), ()))
_TN = (((0,), (0,)), ((), ()))


def _rms(x, g):
    return x * lax.rsqrt(jnp.mean(x * x, axis=-1, keepdims=True) + EPS) * g


def _silu(x):
    return x * (1.0 / (1.0 + jnp.exp(-x)))


def _resident(shape):
    nd = len(shape)
    return pl.BlockSpec(shape, lambda *_: (0,) * nd, pipeline_mode=pl.Buffered(1))


def _params(*sem):
    return pltpu.CompilerParams(dimension_semantics=sem, vmem_limit_bytes=VMEM_LIMIT)


def _ffn_math(x, pre_g, wg_ref, wu_ref, wd_ref, post_g):
    n = FFN_ROW_GROUPS if x.shape[0] % (8 * FFN_ROW_GROUPS) == 0 else 1
    rows = x.shape[0] // n
    xs = [x[g * rows:(g + 1) * rows] for g in range(n)]
    hs = [_rms(xg, pre_g).astype(BF16) for xg in xs]
    gates_ups = [(jnp.dot(h, wg_ref[...], preferred_element_type=F32),
                  jnp.dot(h, wu_ref[...], preferred_element_type=F32)) for h in hs]
    acts = [(_silu(gate) * up).astype(BF16) for gate, up in gates_ups]
    ys = [jnp.dot(act, wd_ref[...], preferred_element_type=F32) for act in acts]
    outs = [xg + 0.5 * _rms(y, post_g) for xg, y in zip(xs, ys)]
    return outs[0] if n == 1 else jnp.concatenate(outs, axis=0)


def _row_groups_call(body, groups, shared, name):
    tm = FFN_ROW_TILE
    tiles = [g[0].shape[0] // tm for g in groups]
    starts = [sum(tiles[:n]) for n in range(len(groups))]
    n_rows_in = sum(len(g) for g in groups)

    def kernel_fn(*refs):
        i = pl.program_id(0)
        shared_refs = refs[n_rows_in:n_rows_in + len(shared)]
        out_refs = refs[n_rows_in + len(shared):]
        first = 0
        for n, g in enumerate(groups):
            ins = refs[first:first + len(g)]
            first += len(g)

            @pl.when((i >= starts[n]) & (i < starts[n] + tiles[n]))
            def _(ins=ins, o_ref=out_refs[n]):
                o_ref[...] = body(*ins, *shared_refs)

    def rows_spec(width, n):
        return pl.BlockSpec((tm, width), lambda i: (jnp.clip(i - starts[n], 0, tiles[n] - 1), 0))

    return pl.pallas_call(
        kernel_fn,
        grid=(sum(tiles),),
        in_specs=[rows_spec(a.shape[1], n) for n, g in enumerate(groups) for a in g]
                 + [_resident(a.shape) for a in shared],
        out_specs=[rows_spec(D_MODEL, n) for n in range(len(groups))],
        out_shape=[jax.ShapeDtypeStruct((g[0].shape[0], D_MODEL), F32) for g in groups],
        compiler_params=_params("arbitrary"),
        name=name,
    )(*[a for g in groups for a in g], *shared)


def _ffn_body(x_ref, pre_ref, wg_ref, wu_ref, wd_ref, post_ref):
    return _ffn_math(x_ref[...], pre_ref[...], wg_ref, wu_ref, wd_ref, post_ref[...])


def _ffn(xs, pre_g, wg, wu, wd, post_g):
    return _row_groups_call(_ffn_body, [(x,) for x in xs], (pre_g, wg, wu, wd, post_g), "ffn")


def _out_ffn_body(x_ref, ry_ref, dy_ref, wo_ref, mixpost_ref, pre_ref, wg_ref, wu_ref, wd_ref, post_ref):
    y = (jnp.dot(ry_ref[...], wo_ref[:GROUP_W, :], preferred_element_type=F32)
         + jnp.dot(dy_ref[...], wo_ref[GROUP_W:, :], preferred_element_type=F32))
    x2 = x_ref[...] + _rms(y, mixpost_ref[...])
    return _ffn_math(x2, pre_ref[...], wg_ref, wu_ref, wd_ref, post_ref[...])


def _out_ffn(groups, wo, mixpost_g, pre_g, wg, wu, wd, post_g):
    return _row_groups_call(_out_ffn_body, groups, (wo, mixpost_g, pre_g, wg, wu, wd, post_g), "out_ffn")


def _rope_tables(pos):
    pos = np.asarray(pos, np.float64)[:, None]
    half = RET_DK // 2
    ang = pos * np.power(RET_THETA, -np.arange(half, dtype=np.float64) * (2.0 / RET_DK))[None, :]
    cos, sin = np.cos(ang), np.sin(ang)
    cr = np.concatenate([cos, cos], axis=1)
    sr = np.concatenate([-sin, sin], axis=1)
    half = ROT_DIM // 2
    ang = pos * np.power(ROPE_THETA, -np.arange(half, dtype=np.float64) * (2.0 / ROT_DIM))[None, :]
    cos, sin = np.cos(ang), np.sin(ang)
    n = pos.shape[0]
    ones = np.ones((n, DIFF_DK - ROT_DIM))
    zeros = np.zeros((n, DIFF_DK - ROT_DIM))
    z8 = np.zeros((n, half))
    c_comp = np.concatenate([cos, cos, ones], axis=1)
    lo_comp = np.concatenate([-sin, z8, zeros], axis=1)
    hi_comp = np.concatenate([z8, sin, zeros], axis=1)
    tables = (cr, sr, np.concatenate([c_comp, c_comp], axis=1), np.concatenate([lo_comp, lo_comp], axis=1),
              np.concatenate([hi_comp, hi_comp], axis=1))
    return tuple(jnp.asarray(t, F32) for t in tables)


def _mix_in_kernel(x_ref, g_ref, w_ref, cr_ref, sr_ref, cd_ref, slo_ref, shi_ref, p_ref, kd_ref, vd_ref,
                   *, transpose_k):
    h = _rms(x_ref[...], g_ref[...]).astype(BF16)
    for grp in (6, 5, 4, 1, 0, 2, 3):
        pg = jnp.dot(h, w_ref[:, grp * GROUP_W:(grp + 1) * GROUP_W], preferred_element_type=F32)
        if grp == 6:
            vd_ref[...] = pg.reshape(pg.shape[0], HEADS, HEAD_W)
        for t in range(HEADS):
            cols = slice(t * HEAD_W, (t + 1) * HEAD_W)
            xt = pg[:, cols]
            if grp in (0, 1):
                xt = xt * cr_ref[...] + pltpu.roll(xt, RET_DK // 2, 1) * sr_ref[...]
                if grp == 1:
                    xt = xt * (RET_DK ** -0.5)
            elif grp in (4, 5):
                xt = (xt * cd_ref[...] + pltpu.roll(xt, HEAD_W - ROT_DIM // 2, 1) * slo_ref[...]
                      + pltpu.roll(xt, ROT_DIM // 2, 1) * shi_ref[...])
                if grp == 5 and transpose_k:
                    kd_ref[cols, :] = xt.T
                elif grp == 5:
                    kd_ref[:, cols] = xt
                else:
                    xt = xt * (DIFF_DK ** -0.5 * LOG2E)
            p_ref[:, grp * GROUP_W + t * HEAD_W:grp * GROUP_W + (t + 1) * HEAD_W] = xt.astype(BF16)


def _mix_in(x, g, w, seq_len, pos0):
    m = x.shape[0]
    tm = min(ROW_TILE, m)
    n_tab = max(seq_len, tm)
    pos = pos0 + (np.arange(n_tab) % seq_len)
    tables = _rope_tables(pos)
    n_tab_blocks = n_tab // tm
    row = pl.BlockSpec((tm, D_MODEL), lambda i: (i, 0))
    tab = pl.BlockSpec((tm, HEAD_W), lambda i: (i % n_tab_blocks, 0))
    transpose_k = seq_len % tm == 0
    if transpose_k:
        n_seq_blocks = seq_len // tm
        k_spec = pl.BlockSpec((None, GROUP_W, tm), lambda i: (i // n_seq_blocks, 0, i % n_seq_blocks))
        k_shape = jax.ShapeDtypeStruct((m // seq_len, GROUP_W, seq_len), F32)
    else:
        k_spec = pl.BlockSpec((tm, GROUP_W), lambda i: (i, 0))
        k_shape = jax.ShapeDtypeStruct((m, GROUP_W), F32)
    p, k_new, v_new = pl.pallas_call(
        functools.partial(_mix_in_kernel, transpose_k=transpose_k),
        grid=(m // tm,),
        in_specs=[row, _resident((1, D_MODEL)), _resident((D_MODEL, IN_COLS)), tab, tab, tab, tab, tab],
        out_specs=[pl.BlockSpec((tm, IN_COLS), lambda i: (i, 0)), k_spec,
                   pl.BlockSpec((tm, HEADS, HEAD_W), lambda i: (i, 0, 0))],
        out_shape=[jax.ShapeDtypeStruct((m, IN_COLS), BF16), k_shape,
                   jax.ShapeDtypeStruct((m, HEADS, HEAD_W), F32)],
        compiler_params=_params("parallel"),
        name="mix_in",
    )(x, g, w, *tables)
    b = m // seq_len
    if transpose_k:
        k_new = jnp.transpose(k_new.reshape(b, HEADS, 2, DIFF_DK, seq_len), (0, 4, 1, 2, 3))
    else:
        k_new = k_new.reshape(b, seq_len, HEADS, 2, DIFF_DK)
    return p.reshape(b, seq_len, IN_COLS), k_new, v_new.reshape(b, seq_len, HEADS, HEAD_W)


def _ret_tables(c):
    log_g = np.log(1.0 - np.power(2.0, -5.0 - np.arange(HEADS, dtype=np.float64)))
    idx = np.arange(c, dtype=np.float64)
    rel = idx[:, None] - idx[None, :]
    dmask = np.where(rel >= 0, np.exp(log_g[:, None, None] * np.maximum(rel, 0.0)), 0.0)
    q_dec = np.exp(log_g[:, None] * (idx + 1.0)[None, :])
    k_dec = np.exp(log_g[:, None] * (c - 1.0 - idx)[None, :])
    b_dec = np.exp(log_g * c)
    bcast = lambda a: np.broadcast_to(a[:, :, None], (HEADS, c, HEAD_W))
    tables = (dmask, bcast(q_dec), bcast(k_dec), np.broadcast_to(b_dec[:, None, None], (HEADS, 8, HEAD_W)))
    return tuple(jnp.asarray(t, F32) for t in tables)


def _ret_kernel(q_ref, k_ref, v_ref, g_ref, s0_ref, dmask_ref, qdec_ref, kdec_ref, bdec_ref, gn_ref,
                y_ref, snew_ref, s_scr, *, c, n_c):
    t = pl.program_id(1)

    @pl.when(t == 0)
    def _():
        s_scr[...] = s0_ref[...]

    for h in range(HEADS):
        cols = slice(h * HEAD_W, (h + 1) * HEAD_W)
        state = s_scr[h]
        b_dec = bdec_ref[h][0:1, :]
        for ci in range(n_c):
            rows = slice(ci * c, (ci + 1) * c)
            q = q_ref[rows, cols]
            k = k_ref[rows, cols]
            v = v_ref[rows, cols]
            scores = lax.dot_general(q, k, _NT, preferred_element_type=F32) * dmask_ref[h]
            o = (jnp.dot(scores.astype(BF16), v, preferred_element_type=F32)
                 + qdec_ref[h] * jnp.dot(q, state.astype(BF16), preferred_element_type=F32))
            k_scaled = (k.astype(F32) * kdec_ref[h]).astype(BF16)
            state = b_dec * state + lax.dot_general(k_scaled, v, _TN, preferred_element_type=F32)
            mu = jnp.mean(o, axis=-1, keepdims=True)
            d = o - mu
            var = jnp.mean(d * d, axis=-1, keepdims=True)
            normed = d * lax.rsqrt(var + EPS) * gn_ref[:, cols]
            y_ref[rows, cols] = (_silu(g_ref[rows, cols].astype(F32)) * normed).astype(BF16)
        s_scr[h] = state

    @pl.when(t == pl.num_programs(1) - 1)
    def _():
        snew_ref[...] = s_scr[...]


def _retention(p, s0, gn, seq_len):
    b = p.shape[0]
    c = min(RET_CHUNK, seq_len)
    n_c = min(RET_BLOCKS_PER_STEP, seq_len // c)
    t_rows = c * n_c
    tables = _ret_tables(c)
    col = lambda g: pl.BlockSpec((None, t_rows, GROUP_W), lambda bi, ti, g=g: (bi, ti, g))
    state = pl.BlockSpec((None, HEADS, RET_DK, HEAD_W), lambda bi, ti: (bi, 0, 0, 0))
    return pl.pallas_call(
        functools.partial(_ret_kernel, c=c, n_c=n_c),
        grid=(b, seq_len // t_rows),
        in_specs=[col(0), col(1), col(2), col(3), state,
                  _resident((HEADS, c, c)), _resident((HEADS, c, HEAD_W)), _resident((HEADS, c, HEAD_W)),
                  _resident((HEADS, 8, HEAD_W)), _resident((1, GROUP_W))],
        out_specs=[pl.BlockSpec((None, t_rows, GROUP_W), lambda bi, ti: (bi, ti, 0)), state],
        out_shape=[jax.ShapeDtypeStruct((b, seq_len, GROUP_W), BF16),
                   jax.ShapeDtypeStruct((b, HEADS, RET_DK, HEAD_W), F32)],
        scratch_shapes=[pltpu.VMEM((HEADS, RET_DK, HEAD_W), F32)],
        compiler_params=_params("parallel", "arbitrary"),
        name="retention",
    )(p, p, p, p, s0, *tables, gn)


def _lambda(lq1_ref, lk1_ref, lq2_ref, lk2_ref, lam_init):
    return (jnp.exp(jnp.sum(lq1_ref[...] * lk1_ref[...], axis=-1, keepdims=True))
            - jnp.exp(jnp.sum(lq2_ref[...] * lk2_ref[...], axis=-1, keepdims=True)) + lam_init)


def _split_components(q):
    lane = lax.broadcasted_iota(jnp.int32, q.shape, 1)
    zero = jnp.zeros_like(q)
    return jnp.where(lane < DIFF_DK, q, zero), jnp.where(lane >= DIFF_DK, q, zero)


def _flash_update(s, v, mask, m_ref, l_ref, acc_ref):
    if mask is not None:
        s = jnp.where(mask, s, MASK_VALUE)
    pad = -s.shape[1] % HEAD_W
    if pad:
        s = jnp.concatenate([s, jnp.full((s.shape[0], pad), MASK_VALUE, F32)], axis=1)
        v = jnp.concatenate([v, jnp.zeros((pad, v.shape[1]), v.dtype)], axis=0)
    blocks = [s[:, b * HEAD_W:(b + 1) * HEAD_W] for b in range(s.shape[1] // HEAD_W)]
    m_prev = m_ref[...]
    m_new = jnp.maximum(m_prev, jnp.max(functools.reduce(jnp.maximum, blocks), axis=-1, keepdims=True))
    alpha = jnp.exp2(m_prev - m_new)
    probs = [jnp.exp2(blk - m_new) for blk in blocks]
    l_ref[...] = alpha * l_ref[...] + functools.reduce(jnp.add, probs)
    p = jnp.concatenate([x.astype(BF16) for x in probs], axis=1)
    acc_ref[...] = alpha * acc_ref[...] + jnp.dot(p, v, preferred_element_type=F32)
    m_ref[...] = m_new


def _flash_init(m_ref, l_ref, acc_ref):
    m_ref[...] = jnp.full(m_ref.shape, MASK_VALUE, F32)
    l_ref[...] = jnp.zeros(l_ref.shape, F32)
    acc_ref[...] = jnp.zeros(acc_ref.shape, F32)


def _flash_update_t(s_t, s_max, vt_ones, mask, m_ref, acc_ref):
    if mask is not None:
        s_t = jnp.where(mask, s_t, MASK_VALUE)
        s_max = jnp.max(s_t, axis=0, keepdims=True)
    m_prev = m_ref[...]
    m_new = jnp.maximum(m_prev, s_max)
    alpha = jnp.exp2(m_prev - m_new)
    p_t = jnp.exp2(s_t - m_new).astype(BF16)
    acc_ref[...] = alpha * acc_ref[...] + jnp.dot(vt_ones, p_t, preferred_element_type=F32)
    m_ref[...] = m_new


def _attn_prompt_kernel(lq1_ref, lk1_ref, lq2_ref, lk2_ref, gn_ref, q_ref, k_ref, v_ref, o_ref,
                        *scratch, tile, lam_init):
    i = pl.program_id(1)
    n_sub = tile // ATTN_Q_SUB
    n_groups = HEADS // ATTN_HEAD_GROUP
    chains = [(hh, qs, c) for hh in range(ATTN_HEAD_GROUP) for qs in range(n_sub) for c in range(2)]
    stats = [{ch: scratch[2 * (par * len(chains) + n):2 * (par * len(chains) + n) + 2]
              for n, ch in enumerate(chains)} for par in range(2)]
    base = 4 * len(chains)
    slots = [(scratch[base + n], scratch[base + 5 + n]) for n in range(4)]
    slot_pairs = (slots[0:2], slots[2:4])
    qt_scr = scratch[base + 4]
    lam = _lambda(lq1_ref, lk1_ref, lq2_ref, lk2_ref, lam_init)
    key_chunk = lax.broadcasted_iota(jnp.int32, (tile, tile), 0) // CHUNK
    query_chunk = lax.broadcasted_iota(jnp.int32, (tile, tile), 1) // CHUNK
    diag_mask = key_chunk <= query_chunk

    def head_cols(g, hh):
        h = g * ATTN_HEAD_GROUP + hh
        return slice(h * HEAD_W, (h + 1) * HEAD_W)

    def key_rows(j):
        return pl.ds(pl.multiple_of(j * tile, tile), tile)

    def scores_into(g, slot, j):
        for n, (hh, qs, c) in enumerate(chains):
            queries = slice(qs * ATTN_Q_SUB, (qs + 1) * ATTN_Q_SUB)
            s_t = jnp.dot(k_ref[key_rows(j), head_cols(g, hh)], qt_scr[g % 2, hh, c, :, queries],
                          preferred_element_type=F32)
            slot[0][n] = s_t
            slot[1][n] = jnp.max(s_t, axis=0, keepdims=True)

    def accumulate(g, slot, j, mask):
        ones = jnp.ones((SUM_ROWS, tile), BF16)
        vt_ones = [jnp.concatenate([v_ref[key_rows(j), head_cols(g, hh)].T, ones], axis=0)
                   for hh in range(ATTN_HEAD_GROUP)]
        for n, (hh, qs, c) in enumerate(chains):
            if mask is None:
                _flash_update_t(slot[0][n], slot[1][n], vt_ones[hh], None, *stats[g % 2][(hh, qs, c)])
            else:
                keys = (qs + 1) * ATTN_Q_SUB
                sub_mask = mask[:keys, qs * ATTN_Q_SUB:(qs + 1) * ATTN_Q_SUB]
                _flash_update_t(slot[0][n, :keys, :], None, vt_ones[hh][:, :keys], sub_mask,
                                *stats[g % 2][(hh, qs, c)])

    def start_group(g):
        if g < n_groups:
            for m_ref, acc_ref in stats[g % 2].values():
                m_ref[...] = jnp.full(m_ref.shape, MASK_VALUE, F32)
                acc_ref[...] = jnp.zeros(acc_ref.shape, F32)
            for hh in range(ATTN_HEAD_GROUP):
                for c, comp in enumerate(_split_components(q_ref[:, head_cols(g, hh)])):
                    qt_scr[g % 2, hh, c] = comp.T
            scores_into(g, slot_pairs[g % 2][0], 0)

    def finish_group(g):
        if g < 0:
            return
        for hh in range(ATTN_HEAD_GROUP):
            for qs in range(n_sub):
                (_, acc1_ref), (_, acc2_ref) = stats[g % 2][(hh, qs, 0)], stats[g % 2][(hh, qs, 1)]
                o_t = (acc1_ref[:HEAD_W] / acc1_ref[HEAD_W:HEAD_W + 1]
                       - lam * (acc2_ref[:HEAD_W] / acc2_ref[HEAD_W:HEAD_W + 1]))
                ms = jnp.mean(o_t * o_t, axis=0, keepdims=True)
                o_t = o_t * lax.rsqrt(ms + EPS) * gn_ref[...] * (1.0 - lam_init)
                o_ref[qs * ATTN_Q_SUB:(qs + 1) * ATTN_Q_SUB, head_cols(g, hh)] = o_t.T.astype(BF16)

    start_group(0)
    for g in range(n_groups):
        slot_a, slot_b = slot_pairs[g % 2]

        def body(jj, carry, g=g, slot_a=slot_a, slot_b=slot_b):
            scores_into(g, slot_b, 2 * jj + 1)
            accumulate(g, slot_a, 2 * jj, None)
            scores_into(g, slot_a, 2 * jj + 2)
            accumulate(g, slot_b, 2 * jj + 1, None)
            return carry

        lax.fori_loop(0, i // 2, body, 0)

        @pl.when(i % 2 == 0)
        def _(g=g, slot_a=slot_a):
            finish_group(g - 1)
            start_group(g + 1)
            accumulate(g, slot_a, i, diag_mask)

        @pl.when(i % 2 == 1)
        def _(g=g, slot_a=slot_a, slot_b=slot_b):
            scores_into(g, slot_b, i)
            accumulate(g, slot_a, i - 1, None)
            finish_group(g - 1)
            start_group(g + 1)
            accumulate(g, slot_b, i, diag_mask)

    finish_group(n_groups - 1)


def _attn_prompt(p, lq1, lk1, lq2, lk2, gn, lam_init):
    b, seq_len, _ = p.shape
    tile = ATTN_TILE
    n_chains = ATTN_HEAD_GROUP * 2 * tile // ATTN_Q_SUB
    small = lambda n: pl.BlockSpec((1, n), lambda bi, qi: (0, 0))
    kv = lambda g: pl.BlockSpec((None, seq_len, GROUP_W), lambda bi, qi, g=g: (bi, 0, g))
    return pl.pallas_call(
        functools.partial(_attn_prompt_kernel, tile=tile, lam_init=lam_init),
        grid=(b, seq_len // tile),
        in_specs=[small(DIFF_DK)] * 4 + [pl.BlockSpec((HEAD_W, 1), lambda bi, qi: (0, 0)),
                  pl.BlockSpec((None, tile, GROUP_W), lambda bi, qi: (bi, qi, 4)), kv(5), kv(6)],
        out_specs=pl.BlockSpec((None, tile, GROUP_W), lambda bi, qi: (bi, qi, 0)),
        out_shape=jax.ShapeDtypeStruct((b, seq_len, GROUP_W), BF16),
        scratch_shapes=[pltpu.VMEM((1, ATTN_Q_SUB), F32),
                        pltpu.VMEM((HEAD_W + SUM_ROWS, ATTN_Q_SUB), F32)] * (2 * n_chains)
                       + [pltpu.VMEM((n_chains, tile, ATTN_Q_SUB), F32)] * 4
                       + [pltpu.VMEM((2, ATTN_HEAD_GROUP, 2, HEAD_W, tile), BF16)]
                       + [pltpu.VMEM((n_chains, 1, ATTN_Q_SUB), F32)] * 4,
        compiler_params=_params("parallel", "arbitrary"),
        name="attn_prompt",
    )(lq1, lk1, lq2, lk2, gn.reshape(HEAD_W, 1), p, p, p)


def _attn_sample_kernel(lq1_ref, lk1_ref, lq2_ref, lk2_ref, gn_ref, q_ref, kp_ref, vp_ref, kn_ref, vn_ref,
                        o_ref, m_scr, l_scr, acc_scr, *, n_past, lam_init):
    j = pl.program_id(1)

    @pl.when(j == 0)
    def _():
        _flash_init(m_scr, l_scr, acc_scr)

    n_q = q_ref.shape[0]

    def update(scores_of, values_of):
        for h in range(HEADS):
            cols = slice(h * HEAD_W, (h + 1) * HEAD_W)
            q_both = jnp.concatenate(_split_components(q_ref[:, cols]), axis=0)
            _flash_update(scores_of(q_both, cols), values_of(h, cols).astype(BF16), None,
                          m_scr.at[h], l_scr.at[h], acc_scr.at[h])

    tk = vp_ref.shape[0] // HEADS

    @pl.when(j < n_past)
    def _():
        update(lambda q, feats: jnp.dot(q, kp_ref[feats, :].astype(BF16), preferred_element_type=F32),
               lambda h, cols: vp_ref[pl.ds(h, tk, stride=HEADS), :])

    @pl.when(j == n_past)
    def _():
        update(lambda q, feats: lax.dot_general(q, kn_ref[:, feats], _NT, preferred_element_type=F32),
               lambda h, cols: vn_ref[:, cols])
        lam = _lambda(lq1_ref, lk1_ref, lq2_ref, lk2_ref, lam_init)
        for h in range(HEADS):
            l = jnp.sum(l_scr[h], axis=-1, keepdims=True)
            o_both = acc_scr[h] / l
            o = o_both[:n_q] - lam * o_both[n_q:]
            o_ref[:, h * HEAD_W:(h + 1) * HEAD_W] = (_rms(o, gn_ref[...]) * (1.0 - lam_init)).astype(BF16)


def _attn_sample(p, k_past, v_past, lq1, lk1, lq2, lk2, gn, lam_init):
    b, seq_len, _ = p.shape
    past_len = k_past.shape[1]
    tk = min(PAST_TILE, past_len)
    n_past = past_len // tk
    k_past = jnp.transpose(k_past, (0, 2, 3, 4, 1)).reshape(b, GROUP_W, past_len)
    small = lambda n: pl.BlockSpec((1, n), lambda bi, ji: (0, 0))
    new = lambda g: pl.BlockSpec((None, seq_len, GROUP_W), lambda bi, ji, g=g: (bi, 0, g))
    past_k = pl.BlockSpec((None, GROUP_W, tk), lambda bi, ji: (bi, 0, jnp.minimum(ji, n_past - 1)))
    v_past = v_past.reshape(b, past_len * HEADS, HEAD_W)
    past_v = pl.BlockSpec((None, tk * HEADS, HEAD_W), lambda bi, ji: (bi, jnp.minimum(ji, n_past - 1), 0))
    return pl.pallas_call(
        functools.partial(_attn_sample_kernel, n_past=n_past, lam_init=lam_init),
        grid=(b, n_past + 1),
        in_specs=[small(DIFF_DK)] * 4 + [small(HEAD_W), new(4), past_k, past_v, new(5), new(6)],
        out_specs=pl.BlockSpec((None, seq_len, GROUP_W), lambda bi, ji: (bi, 0, 0)),
        out_shape=jax.ShapeDtypeStruct((b, seq_len, GROUP_W), BF16),
        scratch_shapes=[pltpu.VMEM((HEADS, 2 * seq_len, HEAD_W), F32)] * 3,
        compiler_params=_params("parallel", "arbitrary"),
        name="attn_sample",
    )(lq1, lk1, lq2, lk2, gn, p, k_past, v_past, p, p)


def _mixers(x1, ret_s0, k_past, v_past, w, lam_init, b, seq_len):
    (_, _, _, _, _, mix_pre, w_in, ret_g, lq1, lk1, lq2, lk2, diff_g, *_) = w
    past_len = 0 if k_past is None else k_past.shape[1]
    p, k_new, v_new = _mix_in(x1, mix_pre, w_in, seq_len, past_len)
    ret_y, s_new = _retention(p, ret_s0, ret_g, seq_len)
    if k_past is None:
        diff_y = _attn_prompt(p, lq1, lk1, lq2, lk2, diff_g, lam_init)
    else:
        diff_y = _attn_sample(p, k_past, v_past, lq1, lk1, lq2, lk2, diff_g, lam_init)
    return (ret_y.reshape(b * seq_len, GROUP_W), diff_y.reshape(b * seq_len, GROUP_W)), (s_new, k_new, v_new)


def _layer(xs, states, pasts, w, lam_init):
    (f1_pre, f1_wg, f1_wu, f1_wd, f1_post, _, _, _, _, _, _, _, _, w_out,
     mix_post, f2_pre, f2_wg, f2_wu, f2_wd, f2_post) = w
    shapes = [x.shape[:2] for x in xs]
    x1s = _ffn([x.reshape(b * l, D_MODEL) for x, (b, l) in zip(xs, shapes)],
               f1_pre, f1_wg, f1_wu, f1_wd, f1_post)
    mixed, caches = [], []
    for x1, state, past, (b, l) in zip(x1s, states, pasts, shapes):
        ys, new = _mixers(x1, state, *(past or (None, None)), w, lam_init, b, l)
        mixed.append((x1,) + ys)
        caches.append(new)
    x3s = _out_ffn(mixed, w_out, mix_post, f2_pre, f2_wg, f2_wu, f2_wd, f2_post)
    return [x3.reshape(b, l, D_MODEL) for x3, (b, l) in zip(x3s, shapes)], caches


def kernel(x_prompt, x_sample, state_ret, cache_diff_k, cache_diff_v, ffn1_pre_g, ffn1_w_gate, ffn1_w_up, ffn1_w_down, ffn1_post_g, mix_pre_g, w_in, ret_norm_g, diff_lq1, diff_lk1, diff_lq2, diff_lk2, diff_norm_g, w_out, mix_post_g, ffn2_pre_g, ffn2_w_gate, ffn2_w_up, ffn2_w_down, ffn2_post_g):
    depth = w_in.shape[0]
    xp, xs = x_prompt, x_sample
    outs = [[] for _ in range(6)]
    for li in range(depth):
        lam_init = 0.8 - 0.6 * math.exp(-0.3 * li)
        row = lambda a: a[li].reshape(1, -1)
        mat = lambda a: a[li].astype(BF16)
        w = (row(ffn1_pre_g), mat(ffn1_w_gate), mat(ffn1_w_up), mat(ffn1_w_down), row(ffn1_post_g),
             row(mix_pre_g), mat(w_in), row(ret_norm_g), row(diff_lq1), row(diff_lk1), row(diff_lq2),
             row(diff_lk2), row(diff_norm_g), mat(w_out), row(mix_post_g),
             row(ffn2_pre_g), mat(ffn2_w_gate), mat(ffn2_w_up), mat(ffn2_w_down), row(ffn2_post_g))
        s0 = jnp.zeros((xp.shape[0], HEADS, RET_DK, HEAD_W), F32)
        (xp, xs), (new_p, new_s) = _layer((xp, xs), (s0, state_ret[li]),
                                          (None, (cache_diff_k[li], cache_diff_v[li])), w, lam_init)
        for acc, val in zip(outs, new_p + new_s):
            acc.append(val)
    return (xp, xs) + tuple(jnp.stack(o) for o in outs)
```
